```python
import jax
import jax.numpy as jnp
from jax import lax
import numpy as np

D_MODEL = 2048
BATCH = 1
SEQ = 16384
DEPTH = 1

CTX_LEN = 256
GRID_W = 64
CHUNK = 128
CHUNK_ROWS = CHUNK // GRID_W
A_WIDTH = D_MODEL // 2
A_GROUPS = 8
A_GDIM = A_WIDTH // A_GROUPS
B_WIDTH = D_MODEL // 2
B_HEADS = 8
B_HDIM = B_WIDTH // B_HEADS
CONV_K = 3
N_GROUPS = 4
EXP_PER_GROUP = 8
N_EXPERTS = N_GROUPS * EXP_PER_GROUP
TOP_K = 2
D_EXPERT = D_MODEL // 2
MOE_BLOCK = 128
ALPHA = (2 * DEPTH) ** 0.25
BETA = (8 * DEPTH) ** -0.25
LN_EPS = 1e-5
NEG = -1e30
OFF_VA = A_WIDTH
OFF_XM = 2 * A_WIDTH
OFF_VB = OFF_XM + B_WIDTH
OFF_OB = OFF_VB + B_WIDTH
OFF_GA = OFF_OB + B_WIDTH
OFF_GB = OFF_GA + D_MODEL
OFF_GATES = OFF_GB + D_MODEL
D_IN = OFF_GATES + 4 * B_HEADS
SPLITS = (OFF_VA, OFF_XM, OFF_VB, OFF_OB, OFF_GA, OFF_GB, OFF_GATES)

kernel_name = 'hybrid_gmlp_mlstm_hmoe_dit'


def layer_norm(x, g=None, b=None):
    xf = x.astype(jnp.float32)
    mu = xf.mean(-1, keepdims=True)
    var = jnp.square(xf - mu).mean(-1, keepdims=True)
    y = (xf - mu) * lax.rsqrt(var + LN_EPS)
    if g is not None:
        y = y * g.astype(jnp.float32) + b.astype(jnp.float32)
    return y.astype(x.dtype)


def modulate(x, shift, scale):
    return layer_norm(x) * (1 + scale[:, None]) + shift[:, None]


def chunk_mlp(u, v, n_chunks, w_s, b_s, g, b):
    bsz, t, _ = v.shape
    vn = layer_norm(v, g, b).reshape(bsz, n_chunks, CHUNK, A_GROUPS, A_GDIM)
    mixed = jnp.einsum('gqp,bnpgc->bnqgc', w_s, vn) + b_s.T[None, None, :, :, None]
    return u * mixed.reshape(bsz, t, A_WIDTH)


def short_conv(x, w, b):
    y = lax.conv_general_dilated(x, w[:, None, :], window_strides=(1,),
                                 padding=((CONV_K // 2, CONV_K // 2),),
                                 dimension_numbers=('NWC', 'WIO', 'NWC'),
                                 feature_group_count=x.shape[-1])
    return y + b


def mlstm_qkv(xm, vb, gates, conv_w, conv_b, w_q, w_k):
    bsz, t, _ = xm.shape
    xc = jax.nn.silu(short_conv(xm, conv_w, conv_b))
    xh = xc.reshape(bsz, t, B_HEADS, B_HDIM)
    q = jnp.einsum('bthd,hde->bhte', xh, w_q).astype(jnp.float32)
    k = jnp.einsum('bthd,hde->bhte', xh, w_k).astype(jnp.float32) * (B_HDIM ** -0.5)
    v = vb.reshape(bsz, t, B_HEADS, B_HDIM).transpose(0, 2, 1, 3).astype(jnp.float32)
    g = gates.astype(jnp.float32).reshape(bsz, t, 4, B_HEADS).transpose(2, 0, 3, 1)
    fwd = (g[0], jax.nn.log_sigmoid(g[1]))
    bwd = (g[2], jax.nn.log_sigmoid(g[3]))
    return xc, q, k, v, fwd, bwd


def rev(*ts):
    return tuple(jnp.flip(t, axis=2) for t in ts)


def zero_state(bsz):
    return (jnp.zeros((bsz, B_HEADS, B_HDIM, B_HDIM), jnp.float32),
            jnp.zeros((bsz, B_HEADS, B_HDIM), jnp.float32),
            jnp.full((bsz, B_HEADS), NEG, jnp.float32))


def final_state(k, v, li, lf):
    f_cum = jnp.cumsum(lf, axis=-1)
    logw = f_cum[..., -1:] - f_cum + li
    m = logw.max(-1)
    w = jnp.exp(logw - m[..., None])
    c_mat = jnp.einsum('bhtd,bhte->bhde', k * w[..., None], v)
    n_vec = jnp.einsum('bht,bhtd->bhd', w, k)
    return (c_mat, n_vec, m)


def mlstm_chunkwise(q, k, v, li, lf, state):
    bsz, nh, t, dk = q.shape
    dv = v.shape[-1]
    nc = t // CHUNK
    causal = jnp.tril(jnp.ones((CHUNK, CHUNK), bool))

    def to_chunks(a):
        return jnp.moveaxis(a.reshape(bsz, nh, nc, CHUNK, *a.shape[3:]), 2, 0)

    def step(carry, inp):
        c_mat, n_vec, m = carry
        qc, kc, vc, lic, lfc = inp
        b = jnp.cumsum(lfc, axis=-1)
        d = jnp.where(causal, b[..., :, None] - b[..., None, :] + lic[..., None, :], NEG)
        inter = b + m[..., None]
        m_row = jnp.maximum(inter, d.max(-1))
        s = jnp.einsum('bhqd,bhkd->bhqk', qc, kc) * jnp.exp(d - m_row[..., None])
        a = jnp.exp(inter - m_row)
        num = a[..., None] * jnp.einsum('bhqd,bhde->bhqe', qc, c_mat) + jnp.einsum('bhqk,bhke->bhqe', s, vc)
        den = a * jnp.einsum('bhqd,bhd->bhq', qc, n_vec) + s.sum(-1)
        h = num / jnp.maximum(jnp.abs(den), jnp.exp(-m_row))[..., None]
        b_last = b[..., -1]
        logw = b_last[..., None] - b + lic
        m_new = jnp.maximum(b_last + m, logw.max(-1))
        wk = jnp.exp(logw - m_new[..., None])
        decay = jnp.exp(b_last + m - m_new)
        c_mat = decay[..., None, None] * c_mat + jnp.einsum('bhsd,bhse->bhde', kc * wk[..., None], vc)
        n_vec = decay[..., None] * n_vec + jnp.einsum('bhs,bhsd->bhd', wk, kc)
        return (c_mat, n_vec, m_new), h

    state, h = lax.scan(step, state, tuple(to_chunks(a) for a in (q, k, v, li, lf)))
    h = jnp.moveaxis(h, 0, 2).reshape(bsz, nh, t, dv)
    return h, state


def mlstm_out(h, xc, o_pre, mh_g, skip):
    bsz, nh, t, dv = h.shape
    mu = h.mean(-1, keepdims=True)
    var = jnp.square(h - mu).mean(-1, keepdims=True)
    hn = ((h - mu) * lax.rsqrt(var + LN_EPS)).transpose(0, 2, 1, 3).reshape(bsz, t, nh * dv).astype(xc.dtype)
    return jax.nn.sigmoid(o_pre) * (hn * mh_g + skip * xc)


def merge(a_pre, yb_pre, ga, gb, p_a, p_b, w_o):
    return (jax.nn.sigmoid(ga) * (a_pre @ p_a) + jax.nn.sigmoid(gb) * (yb_pre @ p_b)) @ w_o


def token_mixer(u, uc, n_lat_chunks, need_ctx, w_in, b_in, w_s, b_s, sgu_g, sgu_b,
                conv_w, conv_b, w_q, w_k, mh_g, skip, p_a, p_b, w_o):
    ua, va, xm, vb, ob, ga, gb, gates = jnp.split(u @ w_in + b_in, SPLITS, axis=-1)
    uac, vac, xmc, vbc, obc, gac, gbc, gatesc = jnp.split(uc @ w_in + b_in, SPLITS, axis=-1)
    xc, q, k, v, fw, bw = mlstm_qkv(xm, vb, gates, conv_w, conv_b, w_q, w_k)
    xcc, qc, kc, vc, fwc, bwc = mlstm_qkv(xmc, vbc, gatesc, conv_w, conv_b, w_q, w_k)
    yc = None
    if need_ctx:
        zero = zero_state(uc.shape[0])
        hcf, st_f = mlstm_chunkwise(qc, kc, vc, *fwc, zero)
        hcb, st_b = mlstm_chunkwise(*rev(qc, kc, vc, *bwc), zero)
        ac = chunk_mlp(jax.nn.gelu(uac), jax.nn.gelu(vac), uc.shape[1] // CHUNK, w_s, b_s, sgu_g, sgu_b)
        yc = merge(ac, mlstm_out(hcf + jnp.flip(hcb, 2), xcc, obc, mh_g, skip), gac, gbc, p_a, p_b, w_o)
    else:
        st_f = final_state(kc, vc, *fwc)
        st_b = final_state(*rev(kc, vc, *bwc))
    hf, _ = mlstm_chunkwise(q, k, v, *fw, st_f)
    hb, _ = mlstm_chunkwise(*rev(q, k, v, *bw), st_b)
    a = chunk_mlp(jax.nn.gelu(ua), jax.nn.gelu(va), n_lat_chunks, w_s, b_s, sgu_g, sgu_b)
    y = merge(a, mlstm_out(hf + jnp.flip(hb, 2), xc, ob, mh_g, skip), ga, gb, p_a, p_b, w_o)
    return y, yc


def moe(t, w_rg, b_rg, w_re, b_re, w1, w3, w2):
    n_tok, d = t.shape
    tok_ids = jnp.arange(n_tok)
    gl = (t @ w_rg + b_rg).astype(jnp.float32)
    g_sel = jnp.argmax(gl, axis=-1)
    p_g = jax.nn.softmax(gl, axis=-1)[tok_ids, g_sel]
    el = (t @ w_re + b_re).astype(jnp.float32).reshape(n_tok, N_GROUPS, EXP_PER_GROUP)
    ep = jax.nn.softmax(el[tok_ids, g_sel], axis=-1)
    top_p, top_i = lax.top_k(ep, TOP_K)
    weights = (p_g[:, None] * top_p / top_p.sum(-1, keepdims=True)).reshape(-1)
    expert = (g_sel[:, None] * EXP_PER_GROUP + top_i).reshape(-1)
    tok = jnp.repeat(tok_ids, TOP_K)
    n_asg = n_tok * TOP_K
    order = jnp.argsort(expert)
    e_sorted = expert[order]
    counts = jnp.bincount(expert, length=N_EXPERTS)
    padded = (counts + MOE_BLOCK - 1) // MOE_BLOCK * MOE_BLOCK
    pad_end = jnp.cumsum(padded)
    pad_start = pad_end - padded
    start = jnp.cumsum(counts) - counts
    dest = pad_start[e_sorted] + jnp.arange(n_asg) - start[e_sorted]
    n_blocks = -(-n_asg // MOE_BLOCK) + N_EXPERTS
    rows = n_blocks * MOE_BLOCK
    buf_tok = jnp.full((rows,), n_tok, jnp.int32).at[dest].set(tok[order].astype(jnp.int32))
    buf_w = jnp.zeros((rows,), jnp.float32).at[dest].set(weights[order])
    block_e = jnp.minimum(jnp.searchsorted(pad_end, jnp.arange(n_blocks) * MOE_BLOCK, side='right'), N_EXPERTS - 1)
    xs = jnp.concatenate([t, jnp.zeros((1, d), t.dtype)], axis=0)[buf_tok].reshape(n_blocks, MOE_BLOCK, d)

    def expert_block(args):
        xb, e = args
        return (jax.nn.silu(xb @ w1[e]) * (xb @ w3[e])) @ w2[e]

    ys = lax.map(expert_block, (xs, block_e)).reshape(rows, d)
    out = jnp.zeros((n_tok + 1, d), ys.dtype).at[buf_tok].add(ys * buf_w[:, None].astype(ys.dtype))
    return out[:n_tok]


def setup_inputs(seed: int = 0) -> dict:
    key = jax.random.key(seed)
    ks = jax.random.split(key, 32)

    def nrm(k, shape, scale):
        return jax.random.normal(k, shape, jnp.float32) * scale

    D, L = D_MODEL, DEPTH
    gate_bias = jnp.concatenate([jnp.zeros((B_HEADS,)), jnp.linspace(3.0, 6.0, B_HEADS),
                                 jnp.zeros((B_HEADS,)), jnp.linspace(3.0, 6.0, B_HEADS)]).astype(jnp.float32)
    return {
        'x': nrm(ks[0], (BATCH, SEQ, D), 1.0),
        'c': nrm(ks[1], (BATCH, D), 1.0),
        'ctx': nrm(ks[2], (BATCH, CTX_LEN, D), 1.0),
        'c_ctx': nrm(ks[3], (D,), 1.0),
        'w_ada': nrm(ks[4], (L, D, 6 * D), D ** -0.5),
        'b_ada': nrm(ks[5], (L, 6 * D), 0.02),
        'w_in': nrm(ks[6], (L, D, D_IN), D ** -0.5),
        'b_in': nrm(ks[7], (L, D_IN), 0.02).at[:, OFF_GATES:].add(gate_bias),
        'w_s': nrm(ks[8], (L, A_GROUPS, CHUNK, CHUNK), CHUNK ** -0.5),
        'b_s': 1.0 + nrm(ks[9], (L, A_GROUPS, CHUNK), 0.02),
        'sgu_g': 1.0 + nrm(ks[10], (L, A_WIDTH), 0.02),
        'sgu_b': nrm(ks[11], (L, A_WIDTH), 0.02),
        'conv_w': nrm(ks[12], (L, CONV_K, B_WIDTH), CONV_K ** -0.5),
        'conv_b': nrm(ks[13], (L, B_WIDTH), 0.02),
        'w_q': nrm(ks[14], (L, B_HEADS, B_HDIM, B_HDIM), B_HDIM ** -0.5),
        'w_k': nrm(ks[15], (L, B_HEADS, B_HDIM, B_HDIM), B_HDIM ** -0.5),
        'mh_g': 1.0 + nrm(ks[16], (L, B_WIDTH), 0.02),
        'skip': 1.0 + nrm(ks[17], (L, B_WIDTH), 0.02),
        'p_a': nrm(ks[18], (L, A_WIDTH, D), BETA * A_WIDTH ** -0.5),
        'p_b': nrm(ks[19], (L, B_WIDTH, D), BETA * B_WIDTH ** -0.5),
        'w_o': nrm(ks[20], (L, D, D), BETA * D ** -0.5),
        'ln1_g': 1.0 + nrm(ks[21], (L, D), 0.02),
        'ln1_b': nrm(ks[22], (L, D), 0.02),
        'w_rg': nrm(ks[23], (L, D, N_GROUPS), D ** -0.5),
        'b_rg': nrm(ks[24], (L, N_GROUPS), 0.01),
        'w_re': nrm(ks[25], (L, D, N_EXPERTS), D ** -0.5),
        'b_re': nrm(ks[26], (L, N_EXPERTS), 0.01),
        'w1': nrm(ks[27], (L, N_EXPERTS, D, D_EXPERT), D ** -0.5),
        'w3': nrm(ks[28], (L, N_EXPERTS, D, D_EXPERT), D ** -0.5),
        'w2': nrm(ks[29], (L, N_EXPERTS, D_EXPERT, D), BETA * D_EXPERT ** -0.5),
        'ln2_g': 1.0 + nrm(ks[30], (L, D), 0.02),
        'ln2_b': nrm(ks[31], (L, D), 0.02),
    }


def reference(x, c, ctx, c_ctx, w_ada, b_ada, w_in, b_in, w_s, b_s, sgu_g, sgu_b, conv_w, conv_b,
              w_q, w_k, mh_g, skip, p_a, p_b, w_o, ln1_g, ln1_b, w_rg, b_rg, w_re, b_re,
              w1, w3, w2, ln2_g, ln2_b):
    bsz, seq, d = x.shape
    rows = seq // GRID_W
    n_lat_chunks = rows // CHUNK_ROWS
    for l in range(DEPTH):
        need_ctx = l < DEPTH - 1
        mod = jax.nn.silu(c) @ w_ada[l] + b_ada[l]
        mod_c = jax.nn.silu(c_ctx)[None] @ w_ada[l] + b_ada[l]
        sh1, sc1, g1, sh2, sc2, g2 = jnp.split(mod, 6, axis=-1)
        sh1c, sc1c, g1c, sh2c, sc2c, g2c = jnp.split(mod_c, 6, axis=-1)
        y, yc = token_mixer(modulate(x, sh1, sc1), modulate(ctx, sh1c, sc1c), n_lat_chunks, need_ctx,
                            w_in[l], b_in[l], w_s[l], b_s[l], sgu_g[l], sgu_b[l], conv_w[l], conv_b[l],
                            w_q[l], w_k[l], mh_g[l], skip[l], p_a[l], p_b[l], w_o[l])
        x = layer_norm(ALPHA * x + g1[:, None] * y, ln1_g[l], ln1_b[l])
        u2 = modulate(x, sh2, sc2).reshape(bsz * seq, d)
        if need_ctx:
            ctx = layer_norm(ALPHA * ctx + g1c[:, None] * yc, ln1_g[l], ln1_b[l])
            u2c = modulate(ctx, sh2c, sc2c).reshape(-1, d)
            n_c = u2c.shape[0]
            f_all = moe(jnp.concatenate([u2c, u2], axis=0), w_rg[l], b_rg[l], w_re[l], b_re[l], w1[l], w3[l], w2[l])
            fc, f = f_all[:n_c], f_all[n_c:]
            ctx = layer_norm(ALPHA * ctx + g2c[:, None] * fc.reshape(ctx.shape), ln2_g[l], ln2_b[l])
        else:
            f = moe(u2, w_rg[l], b_rg[l], w_re[l], b_re[l], w1[l], w3[l], w2[l])
        x = layer_norm(ALPHA * x + g2[:, None] * f.reshape(bsz, seq, d), ln2_g[l], ln2_b[l])
    return x
```

```python
import functools

import jax
import jax.numpy as jnp
import numpy as np
from jax import lax
from jax.experimental import pallas as pl
from jax.experimental.pallas import tpu as pltpu

F32 = jnp.float32
BF16 = jnp.bfloat16

CHUNK = 128
N_HEADS = 8
HEAD_DIM = 128
N_GROUPS = 4
EXP_PER_GROUP = 8
N_EXPERTS = N_GROUPS * EXP_PER_GROUP
LN_EPS = 1e-5
NEG = -1e30
DEPTH = 1
ALPHA = (2 * DEPTH) ** 0.25
LANES = 128
VMEM_LIMIT = 56 * 1024 * 1024

MOE_BLOCK = 512
MOE_FCHUNK = 256


def _cparams(sem):
    return pltpu.CompilerParams(dimension_semantics=sem, vmem_limit_bytes=VMEM_LIMIT)


def _ln(x):
    mu = jnp.mean(x, axis=-1, keepdims=True)
    xc = x - mu
    var = jnp.mean(xc * xc, axis=-1, keepdims=True)
    return xc * lax.rsqrt(var + LN_EPS)


def _sigmoid(x):
    return 1.0 / (1.0 + jnp.exp(-x))


def _silu(x):
    return x * _sigmoid(x)


def _gelu_tanh(x):
    return 0.5 * x * (1.0 + jnp.tanh(0.7978845608028654 * (x + 0.044715 * (x * x * x))))


def _log_sigmoid(x):
    return jnp.minimum(x, 0.0) - jnp.log(1.0 + jnp.exp(-jnp.abs(x)))


def _dot(a, b):
    return jnp.dot(a, b, preferred_element_type=F32)


def _split2(a):
    hi = a.astype(BF16)
    lo = (a - hi.astype(F32)).astype(BF16)
    return hi, lo


def _split3(a):
    hi = a.astype(BF16)
    r = a - hi.astype(F32)
    mid = r.astype(BF16)
    lo = (r - mid.astype(F32)).astype(BF16)
    return hi, mid, lo


def _ada_kernel(c_ref, w_ref, b_ref, o_ref):
    s_hi, s_lo = _split2(_silu(c_ref[...]))
    w_hi, w_lo = _split2(w_ref[...])
    o_ref[...] = _dot(s_hi, w_hi) + _dot(s_lo, w_hi) + _dot(s_hi, w_lo) + b_ref[...]


def _ada(cc, w, b):
    d, n = w.shape
    tn = 1024
    return pl.pallas_call(
        _ada_kernel,
        grid=(n // tn,),
        in_specs=[pl.BlockSpec((8, d), lambda j: (0, 0)),
                  pl.BlockSpec((d, tn), lambda j: (0, j)),
                  pl.BlockSpec((1, tn), lambda j: (0, j))],
        out_specs=pl.BlockSpec((8, tn), lambda j: (0, j)),
        out_shape=jax.ShapeDtypeStruct((8, n), F32),
        compiler_params=_cparams(("arbitrary",)),
        name="ada",
    )(cc, w, b)


def _inproj_kernel(x_ref, sc_ref, sh_ref, w_ref, b_ref, wgh_ref, wgl_ref, bg_ref,
                   z_ref, g_ref, u_scr):
    @pl.when(pl.program_id(1) == 0)
    def _():
        u = _ln(x_ref[...]) * (1.0 + sc_ref[...]) + sh_ref[...]
        u_hi, u_lo = _split2(u)
        u_scr[...] = u_hi
        wgh = wgh_ref[...]
        g_ref[...] = _dot(u_hi, wgh) + _dot(u_lo, wgh) + _dot(u_hi, wgl_ref[...]) + bg_ref[...]

    z_ref[...] = (_dot(u_scr[...], w_ref[...]) + b_ref[...]).astype(z_ref.dtype)


def _in_proj(x, sc, sh, w, b, wgh, wgl, bg, tm):
    n, d = x.shape
    nz = w.shape[1]
    tn = 1024
    ng = wgh.shape[1]
    return pl.pallas_call(
        _inproj_kernel,
        grid=(n // tm, nz // tn),
        in_specs=[pl.BlockSpec((tm, d), lambda i, j: (i, 0)),
                  pl.BlockSpec((1, d), lambda i, j: (0, 0)),
                  pl.BlockSpec((1, d), lambda i, j: (0, 0)),
                  pl.BlockSpec((d, tn), lambda i, j: (0, j)),
                  pl.BlockSpec((1, tn), lambda i, j: (0, j)),
                  pl.BlockSpec((d, ng), lambda i, j: (0, 0)),
                  pl.BlockSpec((d, ng), lambda i, j: (0, 0)),
                  pl.BlockSpec((1, ng), lambda i, j: (0, 0))],
        out_specs=[pl.BlockSpec((tm, tn), lambda i, j: (i, j)),
                   pl.BlockSpec((tm, ng), lambda i, j: (i, 0))],
        out_shape=[jax.ShapeDtypeStruct((n, nz), BF16),
                   jax.ShapeDtypeStruct((n, ng), F32)],
        scratch_shapes=[pltpu.VMEM((tm, d), BF16)],
        compiler_params=_cparams(("arbitrary", "arbitrary")),
        name="in_proj",
    )(x, sc, sh, w, b, wgh, wgl, bg)


def _convqk_kernel(xm_ref, prev_ref, next_ref, cw_ref, cb_ref, wq_ref, wk_ref,
                   xc_ref, q_ref, k_ref):
    i = pl.program_id(0)
    last = pl.num_programs(0) - 1
    tm = xm_ref.shape[0]
    xm = xm_ref[...].astype(F32)
    prev_row = jnp.where(i == 0, 0.0, prev_ref[...].astype(F32)[-1:, :])
    next_row = jnp.where(i == last, 0.0, next_ref[...].astype(F32)[:1, :])
    row = lax.broadcasted_iota(jnp.int32, xm.shape, 0)
    x_prev = jnp.where(row == 0, prev_row, pltpu.roll(xm, 1, 0))
    x_next = jnp.where(row == tm - 1, next_row, pltpu.roll(xm, tm - 1, 0))
    cw = cw_ref[...]
    y = cw[0:1, :] * x_prev + cw[1:2, :] * xm + cw[2:3, :] * x_next + cb_ref[...]
    xc = _silu(y).astype(BF16)
    xc_ref[...] = xc
    for h in range(N_HEADS):
        sl = slice(h * HEAD_DIM, (h + 1) * HEAD_DIM)
        q_ref[:, sl] = _dot(xc[:, sl], wq_ref[h]).astype(BF16)
        k_ref[:, sl] = (_dot(xc[:, sl], wk_ref[h]) * (HEAD_DIM ** -0.5)).astype(BF16)


def _conv_qk(z, xm_blk, cw, cb, wq, wk, tm):
    n = z.shape[0]
    w = N_HEADS * HEAD_DIM
    halo = 16
    nb = n // halo
    per = tm // halo
    out = jax.ShapeDtypeStruct((n, w), BF16)
    return pl.pallas_call(
        _convqk_kernel,
        grid=(n // tm,),
        in_specs=[pl.BlockSpec((tm, w), lambda i: (i, xm_blk)),
                  pl.BlockSpec((halo, w), lambda i: (jnp.maximum(i * per - 1, 0), xm_blk)),
                  pl.BlockSpec((halo, w), lambda i: (jnp.minimum((i + 1) * per, nb - 1), xm_blk)),
                  pl.BlockSpec((3, w), lambda i: (0, 0)),
                  pl.BlockSpec((1, w), lambda i: (0, 0)),
                  pl.BlockSpec((N_HEADS, HEAD_DIM, HEAD_DIM), lambda i: (0, 0, 0)),
                  pl.BlockSpec((N_HEADS, HEAD_DIM, HEAD_DIM), lambda i: (0, 0, 0))],
        out_specs=[pl.BlockSpec((tm, w), lambda i: (i, 0))] * 3,
        out_shape=[out, out, out],
        compiler_params=_cparams(("arbitrary",)),
        name="conv_qk",
    )(z, z, z, cw, cb, wq, wk)


def _dot3_left(t_bf16, a):
    hi, mid, lo = _split3(a)
    return _dot(t_bf16, hi) + _dot(t_bf16, mid) + _dot(t_bf16, lo)


def _dot3_right(a, t_bf16):
    hi, mid, lo = _split3(a)
    return _dot(hi, t_bf16) + _dot(mid, t_bf16) + _dot(lo, t_bf16)


def _mlstm_kernel(reverse, mode, *refs):
    q_ref, k_ref, v_ref, gi_ref, gf_ref, gt_ref, c0_ref, n0_ref, m0_ref = refs[:9]
    rest = refs[9:]
    if mode == "state":
        c_out, n_out, m_out, c_scr, n_scr, m_scr = rest
    elif mode == "h":
        h_out, c_scr, n_scr, m_scr = rest
    else:
        hb_ref, xc_ref, ob_ref, mhg_ref, skip_ref, y_out, c_scr, n_scr, m_scr = rest

    @pl.when(pl.program_id(0) == 0)
    def _():
        c_scr[...] = c0_ref[...]
        n_scr[...] = n0_ref[...]
        m_scr[...] = m0_ref[...]

    L = CHUNK
    ri = lax.broadcasted_iota(jnp.int32, (L, L), 0)
    ci = lax.broadcasted_iota(jnp.int32, (L, L), 1)
    lower = ri >= ci
    upper = ri <= ci
    seen = upper if reverse else lower
    t_col = seen.astype(BF16)
    t_row = (lower if reverse else upper).astype(BF16)
    lane0 = N_HEADS if reverse else 0
    last = 0 if reverse else L - 1

    gi_col = gi_ref[...]
    b_col = _dot3_left(t_col, _log_sigmoid(gf_ref[...]))
    r_col = gi_col - b_col
    gt = gt_ref[...]
    li_row = gt[lane0:lane0 + N_HEADS, :]
    lf_row = _log_sigmoid(gt[2 * N_HEADS + lane0:3 * N_HEADS + lane0, :])
    r_row = li_row - _dot3_right(lf_row, t_row)

    for h in range(N_HEADS):
        sl = slice(h * HEAD_DIM, (h + 1) * HEAD_DIM)
        ln = lane0 + h
        qh = q_ref[:, sl]
        kh = k_ref[:, sl]
        vh = v_ref[:, sl]
        m_h = m_scr[h:h + 1, 0:1]
        d0 = jnp.where(seen, r_row[h:h + 1, :], NEG)
        big_m = jnp.maximum(jnp.max(d0, axis=-1, keepdims=True), m_h)
        m_last = big_m[last:last + 1, :]
        c_prev = c_scr[h]
        n_prev = n_scr[h:h + 1, :]
        if mode != "state":
            p = jnp.exp(d0 - big_m)
            s = lax.dot_general(qh, kh, (((1,), (1,)), ((), ())), preferred_element_type=F32) * p
            a = jnp.exp(m_h - big_m)
            num = a * _dot(qh, c_prev.astype(BF16)) + _dot(s.astype(BF16), vh)
            qn = jnp.sum(qh.astype(F32) * n_prev, axis=-1, keepdims=True)
            den = a * qn + jnp.sum(s, axis=-1, keepdims=True)
            b_q = b_col[:, ln:ln + 1]
            hh = num / jnp.maximum(jnp.abs(den), jnp.exp(-(b_q + big_m)))
            if mode == "h":
                h_out[:, sl] = hh.astype(h_out.dtype)
            else:
                hs = hh + hb_ref[:, sl].astype(F32)
                hn = _ln(hs)
                y = _sigmoid(ob_ref[:, sl].astype(F32)) * (
                    hn * mhg_ref[:, sl] + skip_ref[:, sl] * xc_ref[:, sl].astype(F32))
                y_out[:, sl] = y.astype(y_out.dtype)
        wk = jnp.exp(r_col[:, ln:ln + 1] - m_last)
        decay = jnp.exp(m_h - m_last)
        kw = kh.astype(F32) * wk
        c_scr[h] = decay * c_prev + lax.dot_general(
            kw.astype(BF16), vh, (((0,), (0,)), ((), ())), preferred_element_type=F32)
        n_scr[h:h + 1, :] = decay * n_prev + jnp.sum(kw, axis=0, keepdims=True)
        m_new = b_col[last:last + 1, ln:ln + 1] + m_last
        m_scr[h:h + 1, :] = jnp.broadcast_to(m_new, (1, LANES))

    if mode == "state":
        c_out[...] = c_scr[...]
        n_out[...] = n_scr[...]
        m_out[...] = m_scr[...]


def _mlstm(mode, reverse, q, k, z, v_blk, g, gt, state, extra=()):
    n = q.shape[0]
    nc = n // CHUNK
    w = N_HEADS * HEAD_DIM
    c0, n0, m0 = state
    pos = (lambda c: nc - 1 - c) if reverse else (lambda c: c)
    row = lambda blk: pl.BlockSpec((CHUNK, w), lambda c: (pos(c), blk))
    full = lambda a: pl.BlockSpec(a.shape, lambda c: (0,) * a.ndim)
    in_specs = [row(0), row(0), row(v_blk),
                pl.BlockSpec((CHUNK, LANES), lambda c: (pos(c), 0)),
                pl.BlockSpec((CHUNK, LANES), lambda c: (pos(c), 1)),
                pl.BlockSpec((4 * N_HEADS, CHUNK), lambda c: (0, pos(c))),
                full(c0), full(n0), full(m0)]
    args = [q, k, z, g, g, gt, c0, n0, m0]
    scratch = [pltpu.VMEM(c0.shape, F32), pltpu.VMEM(n0.shape, F32), pltpu.VMEM(m0.shape, F32)]
    if mode == "state":
        out_specs = [full(c0), full(n0), full(m0)]
        out_shape = [jax.ShapeDtypeStruct(a.shape, F32) for a in state]
    else:
        out_specs = row(0)
        out_shape = jax.ShapeDtypeStruct((n, w), BF16)
    if mode == "out":
        hb, xc, ob_blk, mhg, skip = extra
        in_specs += [row(0), row(0), row(ob_blk), full(mhg), full(skip)]
        args += [hb, xc, z, mhg, skip]
    return pl.pallas_call(
        functools.partial(_mlstm_kernel, reverse, mode),
        grid=(nc,),
        in_specs=in_specs, out_specs=out_specs, out_shape=out_shape,
        scratch_shapes=scratch,
        compiler_params=_cparams(("arbitrary",)),
        name="mlstm_%s_%s" % (mode, "bwd" if reverse else "fwd"),
    )(*args)


def _merge_kernel(ua_ref, va_ref, yb_ref, ga0_ref, ga1_ref, gb0_ref, gb1_ref, x_ref,
                  ws_ref, bs_ref, sg_ref, sb_ref, pa_ref, pb_ref, wo_ref,
                  g1_ref, l1g_ref, l1b_ref, sc2_ref, sh2_ref, wrh_ref, wrl_ref, br_ref,
                  x1_ref, u2_ref, route_ref, cnt_ref, a_scr, run_scr):
    tm = x_ref.shape[0]
    half = pa_ref.shape[1] // 2

    @pl.when(pl.program_id(0) == 0)
    def _():
        run_scr[...] = jnp.zeros_like(run_scr)

    vn = (_ln(_gelu_tanh(va_ref[...].astype(F32))) * sg_ref[...] + sb_ref[...]).astype(BF16)
    for c in range(tm // CHUNK):
        rows = slice(c * CHUNK, (c + 1) * CHUNK)
        for g in range(N_HEADS):
            cols = slice(g * HEAD_DIM, (g + 1) * HEAD_DIM)
            mixed = _dot(ws_ref[g], vn[rows, cols]) + bs_ref[:, cols]
            a_scr[rows, cols] = (_gelu_tanh(ua_ref[rows, cols].astype(F32)) * mixed).astype(BF16)

    pa = _dot(a_scr[...], pa_ref[...])
    pb = _dot(yb_ref[...], pb_ref[...])
    ga = jnp.concatenate([ga0_ref[...], ga1_ref[...]], axis=1).astype(F32)
    gb = jnp.concatenate([gb0_ref[...], gb1_ref[...]], axis=1).astype(F32)
    mrg = (_sigmoid(ga) * pa + _sigmoid(gb) * pb).astype(BF16)
    y = _dot(mrg, wo_ref[...])
    x1 = _ln(ALPHA * x_ref[...] + g1_ref[...] * y) * l1g_ref[...] + l1b_ref[...]
    x1_ref[...] = x1
    u2 = _ln(x1) * (1.0 + sc2_ref[...]) + sh2_ref[...]
    u2_ref[...] = u2

    u_hi, u_lo = _split2(u2)
    wrh = wrh_ref[...]
    logit = _dot(u_hi, wrh) + _dot(u_lo, wrh) + _dot(u_hi, wrl_ref[...]) + br_ref[...]
    lane = lax.broadcasted_iota(jnp.int32, logit.shape, 1)
    lane_f = lane.astype(F32)
    is_g = lane < N_GROUPS
    gmax = jnp.max(jnp.where(is_g, logit, NEG), axis=-1, keepdims=True)
    g_sel = jnp.min(jnp.where(is_g & (logit == gmax), lane_f, 1e9), axis=-1, keepdims=True)
    p_g = 1.0 / jnp.sum(jnp.where(is_g, jnp.exp(logit - gmax), 0.0), axis=-1, keepdims=True)
    lo = N_GROUPS + EXP_PER_GROUP * g_sel
    in_grp = (lane_f >= lo) & (lane_f < lo + EXP_PER_GROUP)
    el = jnp.where(in_grp, logit, NEG)
    e1max = jnp.max(el, axis=-1, keepdims=True)
    l1 = jnp.min(jnp.where(in_grp & (el == e1max), lane_f, 1e9), axis=-1, keepdims=True)
    el2 = jnp.where(lane_f == l1, NEG, el)
    e2max = jnp.max(el2, axis=-1, keepdims=True)
    l2 = jnp.min(jnp.where(in_grp & (el2 == e2max), lane_f, 1e9), axis=-1, keepdims=True)
    zsum = jnp.sum(jnp.where(in_grp, jnp.exp(el - e1max), 0.0), axis=-1, keepdims=True)
    p1 = 1.0 / zsum
    p2 = jnp.exp(e2max - e1max) / zsum
    w1 = p_g * p1 / (p1 + p2)
    w2 = p_g * p2 / (p1 + p2)
    e1 = l1 - N_GROUPS
    e2 = l2 - N_GROUPS

    oh1 = (lane_f == e1).astype(BF16)
    oh2 = (lane_f == e2).astype(BF16)
    ri = lax.broadcasted_iota(jnp.int32, (tm, tm), 0)
    ci = lax.broadcasted_iota(jnp.int32, (tm, tm), 1)
    strict = (ci < ri).astype(BF16)
    run = run_scr[0:1, :]
    cnt1 = jnp.sum(oh1.astype(F32), axis=0, keepdims=True)
    cnt2 = jnp.sum(oh2.astype(F32), axis=0, keepdims=True)
    pre1 = _dot(strict, oh1) + run
    pre2 = _dot(strict, oh2) + run + cnt1
    rank1 = jnp.sum(oh1.astype(F32) * pre1, axis=-1, keepdims=True)
    rank2 = jnp.sum(oh2.astype(F32) * pre2, axis=-1, keepdims=True)
    new_run = run + cnt1 + cnt2
    run_scr[...] = jnp.broadcast_to(new_run, run_scr.shape)
    cnt_ref[...] = jnp.broadcast_to(new_run, cnt_ref.shape)
    route = jnp.where(lane == 0, e1, 0.0)
    route = jnp.where(lane == 1, e2, route)
    route = jnp.where(lane == 2, w1, route)
    route = jnp.where(lane == 3, w2, route)
    route = jnp.where(lane == 4, rank1, route)
    route = jnp.where(lane == 5, rank2, route)
    route_ref[...] = route


def _merge(z, yb, x, ws, bs, sg, sb, pa, pb, wo, g1, l1g, l1b, sc2, sh2, wrh, wrl, br, tm):
    n, d = x.shape
    w = N_HEADS * HEAD_DIM
    zc = lambda blk: pl.BlockSpec((tm, w), lambda i: (i, blk))
    full = lambda a: pl.BlockSpec(a.shape, lambda i: (0,) * a.ndim)
    consts = [ws, bs, sg, sb, pa, pb, wo, g1, l1g, l1b, sc2, sh2, wrh, wrl, br]
    return pl.pallas_call(
        _merge_kernel,
        grid=(n // tm,),
        in_specs=[zc(0), zc(1), pl.BlockSpec((tm, w), lambda i: (i, 0)),
                  zc(5), zc(6), zc(7), zc(8),
                  pl.BlockSpec((tm, d), lambda i: (i, 0))] + [full(a) for a in consts],
        out_specs=[pl.BlockSpec((tm, d), lambda i: (i, 0)),
                   pl.BlockSpec((tm, d), lambda i: (i, 0)),
                   pl.BlockSpec((tm, LANES), lambda i: (i, 0)),
                   pl.BlockSpec((8, LANES), lambda i: (0, 0))],
        out_shape=[jax.ShapeDtypeStruct((n, d), F32),
                   jax.ShapeDtypeStruct((n, d), F32),
                   jax.ShapeDtypeStruct((n, LANES), F32),
                   jax.ShapeDtypeStruct((8, LANES), F32)],
        scratch_shapes=[pltpu.VMEM((tm, w), BF16), pltpu.VMEM((8, LANES), F32)],
        compiler_params=_cparams(("arbitrary",)),
        name="merge",
    )(z, z, yb, z, z, z, z, x, *consts)


def _dispatch_kernel(d1_ref, d2_ref, u_ref, xs_in_ref, xs_ref, sem):
    del xs_in_ref
    tm = u_ref.shape[0]
    base = pl.program_id(0) * tm

    def copies(t):
        src = u_ref.at[pl.ds(t, 1), :]
        return (pltpu.make_async_copy(src, xs_ref.at[pl.ds(d1_ref[base + t], 1), :], sem),
                pltpu.make_async_copy(src, xs_ref.at[pl.ds(d2_ref[base + t], 1), :], sem))

    def start(t, carry):
        for cp in copies(t):
            cp.start()
        return carry

    def wait(t, carry):
        for cp in copies(t):
            cp.wait()
        return carry

    lax.fori_loop(0, tm, start, 0)
    lax.fori_loop(0, tm, wait, 0)


def _dispatch(dest1, dest2, u2, xs0, tm):
    n, d = u2.shape
    return pl.pallas_call(
        _dispatch_kernel,
        grid_spec=pltpu.PrefetchScalarGridSpec(
            num_scalar_prefetch=2,
            grid=(n // tm,),
            in_specs=[pl.BlockSpec((tm, d), lambda i, d1, d2: (i, 0)),
                      pl.BlockSpec(memory_space=pl.ANY)],
            out_specs=pl.BlockSpec(memory_space=pl.ANY),
            scratch_shapes=[pltpu.SemaphoreType.DMA(())]),
        out_shape=jax.ShapeDtypeStruct(xs0.shape, xs0.dtype),
        input_output_aliases={3: 0},
        compiler_params=_cparams(("arbitrary",)),
        name="dispatch",
    )(dest1, dest2, u2, xs0)


def _experts_kernel(be_ref, nb_ref, xs_ref, w1_ref, w3_ref, w2_ref, ys_ref, xb_scr, acc_scr):
    b = pl.program_id(0)
    f = pl.program_id(1)

    @pl.when(b < nb_ref[0])
    def _():
        @pl.when(f == 0)
        def _():
            xb_scr[...] = xs_ref[...].astype(BF16)

        xb = xb_scr[...]
        h1 = _dot(xb, w1_ref[...].astype(BF16))
        h3 = _dot(xb, w3_ref[...].astype(BF16))
        part = _dot((_silu(h1) * h3).astype(BF16), w2_ref[...].astype(BF16))

        @pl.when(f == 0)
        def _():
            acc_scr[...] = part

        @pl.when(f > 0)
        def _():
            acc_scr[...] += part

        @pl.when(f == pl.num_programs(1) - 1)
        def _():
            ys_ref[...] = acc_scr[...]


def _experts(block_e, nblk, xs, w1, w3, w2, max_blocks):
    d = xs.shape[1]
    de = w1.shape[2]
    nf = de // MOE_FCHUNK
    bm = MOE_BLOCK

    def live(b, nb):
        return b < nb[0]

    def x_map(b, f, be, nb):
        return (jnp.minimum(b, nb[0] - 1), 0)

    def w13_map(b, f, be, nb):
        return (be[jnp.minimum(b, nb[0] - 1)], 0, jnp.where(live(b, nb), f, nf - 1))

    def w2_map(b, f, be, nb):
        return (be[jnp.minimum(b, nb[0] - 1)], jnp.where(live(b, nb), f, nf - 1), 0)

    return pl.pallas_call(
        _experts_kernel,
        grid_spec=pltpu.PrefetchScalarGridSpec(
            num_scalar_prefetch=2,
            grid=(max_blocks, nf),
            in_specs=[pl.BlockSpec((bm, d), x_map),
                      pl.BlockSpec((None, d, MOE_FCHUNK), w13_map),
                      pl.BlockSpec((None, d, MOE_FCHUNK), w13_map),
                      pl.BlockSpec((None, MOE_FCHUNK, d), w2_map)],
            out_specs=pl.BlockSpec((bm, d), x_map),
            scratch_shapes=[pltpu.VMEM((bm, d), BF16), pltpu.VMEM((bm, d), F32)]),
        out_shape=jax.ShapeDtypeStruct(xs.shape, F32),
        input_output_aliases={2: 0},
        compiler_params=_cparams(("arbitrary", "arbitrary")),
        name="experts",
    )(block_e, nblk, xs, w1, w3, w2)


def _combine_kernel(d1_ref, d2_ref, x1_ref, route_ref, g2_ref, lg_ref, lb_ref, ys_ref,
                    o_ref, buf, sem):
    tm = x1_ref.shape[0]
    i = pl.program_id(0)
    nsteps = pl.num_programs(0)

    def copies(step, slot, t):
        tok = step * tm + t
        return (pltpu.make_async_copy(ys_ref.at[pl.ds(d1_ref[tok], 1), :],
                                      buf.at[slot, 0, pl.ds(t, 1), :], sem.at[slot]),
                pltpu.make_async_copy(ys_ref.at[pl.ds(d2_ref[tok], 1), :],
                                      buf.at[slot, 1, pl.ds(t, 1), :], sem.at[slot]))

    def issue(step, slot):
        def body(t, carry):
            for cp in copies(step, slot, t):
                cp.start()
            return carry
        lax.fori_loop(0, tm, body, 0)

    def drain(step, slot):
        def body(t, carry):
            for cp in copies(step, slot, t):
                cp.wait()
            return carry
        lax.fori_loop(0, tm, body, 0)

    slot = lax.rem(i, 2)

    @pl.when(i == 0)
    def _():
        issue(0, 0)

    @pl.when(i + 1 < nsteps)
    def _():
        issue(i + 1, 1 - slot)

    drain(i, slot)
    route = route_ref[...]
    f = route[:, 2:3] * buf[slot, 0] + route[:, 3:4] * buf[slot, 1]
    o_ref[...] = _ln(ALPHA * x1_ref[...] + g2_ref[...] * f) * lg_ref[...] + lb_ref[...]


def _combine(dest1, dest2, x1, route, g2, lg, lb, ys, tm):
    n, d = x1.shape
    vec = pl.BlockSpec((1, d), lambda i, d1, d2: (0, 0))
    return pl.pallas_call(
        _combine_kernel,
        grid_spec=pltpu.PrefetchScalarGridSpec(
            num_scalar_prefetch=2,
            grid=(n // tm,),
            in_specs=[pl.BlockSpec((tm, d), lambda i, d1, d2: (i, 0)),
                      pl.BlockSpec((tm, LANES), lambda i, d1, d2: (i, 0)),
                      vec, vec, vec,
                      pl.BlockSpec(memory_space=pl.ANY)],
            out_specs=pl.BlockSpec((tm, d), lambda i, d1, d2: (i, 0)),
            scratch_shapes=[pltpu.VMEM((2, 2, tm, d), F32), pltpu.SemaphoreType.DMA((2,))]),
        out_shape=jax.ShapeDtypeStruct((n, d), F32),
        compiler_params=_cparams(("arbitrary",)),
        name="combine",
    )(dest1, dest2, x1, route, g2, lg, lb, ys)


def _layer(x, ctx, c, c_ctx, w_ada, b_ada, w_in, b_in, w_s, b_s, sgu_g, sgu_b, conv_w, conv_b,
           w_q, w_k, mh_g, skip, p_a, p_b, w_o, ln1_g, ln1_b, w_rg, b_rg, w_re, b_re,
           w1, w3, w2, ln2_g, ln2_b):
    n, d = x.shape
    w = N_HEADS * HEAD_DIM
    nz = 9 * w
    row = lambda a: a.reshape(1, -1)

    cc = jnp.zeros((8, d), F32).at[0].set(c[0]).at[1].set(c_ctx)
    mod = _ada(cc, w_ada, row(b_ada))
    sh1, sc1, g1, sh2, sc2, g2 = [mod[0:1, i * d:(i + 1) * d] for i in range(6)]
    sh1c, sc1c = mod[1:2, 0:d], mod[1:2, d:2 * d]

    wg = w_in[:, nz:]
    bg = b_in[nz:]
    H = N_HEADS
    order_i = np.r_[0:H, 2 * H:3 * H]
    order_f = np.r_[H:2 * H, 3 * H:4 * H]
    wg2 = jnp.zeros((d, 2 * LANES), F32).at[:, 0:2 * H].set(wg[:, order_i]).at[:, LANES:LANES + 2 * H].set(wg[:, order_f])
    bg2 = jnp.zeros((1, 2 * LANES), F32).at[0, 0:2 * H].set(bg[order_i]).at[0, LANES:LANES + 2 * H].set(bg[order_f])
    wgh = wg2.astype(BF16)
    wgl = (wg2 - wgh.astype(F32)).astype(BF16)
    w_main = w_in[:, :nz].astype(BF16)
    b_main = row(b_in[:nz])

    z, g = _in_proj(x, sc1, sh1, w_main, b_main, wgh, wgl, bg2, tm=min(n, 1024))
    zc, gc = _in_proj(ctx, sc1c, sh1c, w_main, b_main, wgh, wgl, bg2, tm=ctx.shape[0])

    cw, cb = conv_w, row(conv_b)
    wq, wk = w_q.astype(BF16), w_k.astype(BF16)
    xc, q, k = _conv_qk(z, 2, cw, cb, wq, wk, tm=min(n, 512))
    _, qc, kc = _conv_qk(zc, 2, cw, cb, wq, wk, tm=ctx.shape[0])

    def gates_t(ga):
        return jnp.concatenate([ga[:, 0:2 * H], ga[:, LANES:LANES + 2 * H]], axis=1).T

    zero = (jnp.zeros((H, HEAD_DIM, HEAD_DIM), F32), jnp.zeros((H, HEAD_DIM), F32),
            jnp.full((H, LANES), NEG, F32))
    gct = gates_t(gc)
    st_f = _mlstm("state", False, qc, kc, zc, 3, gc, gct, zero)
    st_b = _mlstm("state", True, qc, kc, zc, 3, gc, gct, zero)
    gt = gates_t(g)
    hb = _mlstm("h", True, q, k, z, 3, g, gt, st_b)
    yb = _mlstm("out", False, q, k, z, 3, g, gt, st_f, extra=(hb, xc, 4, row(mh_g), row(skip)))

    bs_full = jnp.repeat(b_s.T, HEAD_DIM, axis=1)
    wr = jnp.zeros((d, LANES), F32).at[:, :N_GROUPS].set(w_rg).at[:, N_GROUPS:N_GROUPS + N_EXPERTS].set(w_re)
    br = jnp.zeros((1, LANES), F32).at[0, :N_GROUPS].set(b_rg).at[0, N_GROUPS:N_GROUPS + N_EXPERTS].set(b_re)
    wrh = wr.astype(BF16)
    wrl = (wr - wrh.astype(F32)).astype(BF16)
    x1, u2, route, cnt = _merge(z, yb, x, w_s.astype(BF16), bs_full, row(sgu_g), row(sgu_b),
                                p_a.astype(BF16), p_b.astype(BF16), w_o.astype(BF16),
                                g1, row(ln1_g), row(ln1_b), sc2, sh2, wrh, wrl, br, tm=min(n, 256))

    bm = MOE_BLOCK
    counts = cnt[0, :N_EXPERTS].astype(jnp.int32)
    nblk_e = (counts + bm - 1) // bm
    blk_end = jnp.cumsum(nblk_e)
    row_start = (blk_end - nblk_e) * bm
    e1 = route[:, 0].astype(jnp.int32)
    e2 = route[:, 1].astype(jnp.int32)
    dest1 = row_start[e1] + route[:, 4].astype(jnp.int32)
    dest2 = row_start[e2] + route[:, 5].astype(jnp.int32)
    max_blocks = (2 * n + N_EXPERTS * (bm - 1)) // bm
    block_e = jnp.minimum(jnp.searchsorted(blk_end, jnp.arange(max_blocks), side='right'),
                          N_EXPERTS - 1).astype(jnp.int32)
    nblk = blk_end[-1:].astype(jnp.int32)

    xs = _dispatch(dest1, dest2, u2, jnp.zeros((max_blocks * bm, d), F32), tm=min(n, 256))
    ys = _experts(block_e, nblk, xs, w1, w3, w2, max_blocks)
    return _combine(dest1, dest2, x1, route, g2, row(ln2_g), row(ln2_b), ys, tm=min(n, 256))


def kernel(x, c, ctx, c_ctx, w_ada, b_ada, w_in, b_in, w_s, b_s, sgu_g, sgu_b, conv_w, conv_b, w_q, w_k, mh_g, skip, p_a, p_b, w_o, ln1_g, ln1_b, w_rg, b_rg, w_re, b_re, w1, w3, w2, ln2_g, ln2_b):
    assert x.shape[0] == 1 and w_ada.shape[0] == DEPTH == 1
    out = _layer(x[0], ctx[0], c, c_ctx, w_ada[0], b_ada[0], w_in[0], b_in[0], w_s[0], b_s[0],
                 sgu_g[0], sgu_b[0], conv_w[0], conv_b[0], w_q[0], w_k[0], mh_g[0], skip[0],
                 p_a[0], p_b[0], w_o[0], ln1_g[0], ln1_b[0], w_rg[0], b_rg[0], w_re[0], b_re[0],
                 w1[0], w3[0], w2[0], ln2_g[0], ln2_b[0])
    return out[None]
```

```python
import functools

import jax
import jax.numpy as jnp
from jax import lax
from jax.experimental import pallas as pl
from jax.experimental.pallas import tpu as pltpu

F32 = jnp.float32
BF16 = jnp.bfloat16

CHUNK = 128
N_HEADS = 8
HEAD_DIM = 128
N_GROUPS = 4
EXP_PER_GROUP = 8
N_EXPERTS = N_GROUPS * EXP_PER_GROUP
LN_EPS = 1e-5
NEG = -1e30
DEPTH = 1
ALPHA = (2 * DEPTH) ** 0.25
LANES = 128
VMEM_LIMIT = 56 * 1024 * 1024

MOE_BLOCK = 512
MOE_FCHUNK = 256


def _cparams(sem):
    return pltpu.CompilerParams(dimension_semantics=sem, vmem_limit_bytes=VMEM_LIMIT)


def _ln(x):
    mu = jnp.mean(x, axis=-1, keepdims=True)
    xc = x - mu
    var = jnp.mean(xc * xc, axis=-1, keepdims=True)
    return xc * lax.rsqrt(var + LN_EPS)


def _sigmoid(x):
    return 1.0 / (1.0 + jnp.exp(-x))


def _silu(x):
    return x * _sigmoid(x)


def _gelu_tanh(x):
    return 0.5 * x * (1.0 + jnp.tanh(0.7978845608028654 * (x + 0.044715 * (x * x * x))))


def _log_sigmoid(x):
    return jnp.minimum(x, 0.0) - jnp.log(1.0 + jnp.exp(-jnp.abs(x)))


def _dot(a, b):
    return jnp.dot(a, b, preferred_element_type=F32)


def _dot_nt(a, b):
    return lax.dot_general(a, b, (((1,), (1,)), ((), ())), preferred_element_type=F32)


def _split2(a):
    hi = a.astype(BF16)
    lo = (a - hi.astype(F32)).astype(BF16)
    return hi, lo


def _split3(a):
    hi = a.astype(BF16)
    r = a - hi.astype(F32)
    mid = r.astype(BF16)
    lo = (r - mid.astype(F32)).astype(BF16)
    return hi, mid, lo


def _ada_kernel(c_ref, w_ref, b_ref, o_ref):
    s_hi, s_lo = _split2(_silu(c_ref[...]))
    w_hi, w_lo = _split2(w_ref[...])
    o_ref[...] = _dot(s_hi, w_hi) + _dot(s_lo, w_hi) + _dot(s_hi, w_lo) + b_ref[...]


def _ada(cc, w, b):
    d, n = w.shape
    tn = 1024
    return pl.pallas_call(
        _ada_kernel,
        grid=(n // tn,),
        in_specs=[pl.BlockSpec((8, d), lambda j: (0, 0)),
                  pl.BlockSpec((d, tn), lambda j: (0, j)),
                  pl.BlockSpec((1, tn), lambda j: (0, j))],
        out_specs=pl.BlockSpec((8, tn), lambda j: (0, j)),
        out_shape=jax.ShapeDtypeStruct((8, n), F32),
        compiler_params=_cparams(("arbitrary",)),
        name="ada",
    )(cc, w, b)


def _inproj_kernel(x_ref, sc_ref, sh_ref, w_ref, b_ref, wgh_ref, wgl_ref, bg_ref,
                   z_ref, g_ref, u_scr):
    @pl.when(pl.program_id(1) == 0)
    def _():
        u = _ln(x_ref[...]) * (1.0 + sc_ref[...]) + sh_ref[...]
        u_hi, u_lo = _split2(u)
        u_scr[...] = u_hi
        wgh = wgh_ref[...]
        g = _dot(u_hi, wgh) + _dot(u_lo, wgh) + _dot(u_hi, wgl_ref[...]) + bg_ref[...]
        g_ref[...] = g.T[:g_ref.shape[0], :]

    z_ref[...] = (_dot(u_scr[...], w_ref[...]) + b_ref[...]).astype(z_ref.dtype)


def _in_proj(x, sc, sh, w, b, wgh, wgl, bg, tm):
    n, d = x.shape
    tn = 1024
    nz = w.shape[1] // tn * tn
    ng = wgh.shape[1]
    return pl.pallas_call(
        _inproj_kernel,
        grid=(n // tm, nz // tn),
        in_specs=[pl.BlockSpec((tm, d), lambda i, j: (i, 0)),
                  pl.BlockSpec((1, d), lambda i, j: (0, 0)),
                  pl.BlockSpec((1, d), lambda i, j: (0, 0)),
                  pl.BlockSpec((d, tn), lambda i, j: (0, j)),
                  pl.BlockSpec((1, tn), lambda i, j: (0, j)),
                  pl.BlockSpec((d, ng), lambda i, j: (0, 0)),
                  pl.BlockSpec((d, ng), lambda i, j: (0, 0)),
                  pl.BlockSpec((1, ng), lambda i, j: (0, 0))],
        out_specs=[pl.BlockSpec((tm, tn), lambda i, j: (i, j)),
                   pl.BlockSpec((4 * N_HEADS, tm), lambda i, j: (0, i))],
        out_shape=[jax.ShapeDtypeStruct((n, nz), BF16),
                   jax.ShapeDtypeStruct((4 * N_HEADS, n), F32)],
        scratch_shapes=[pltpu.VMEM((tm, d), BF16)],
        compiler_params=_cparams(("arbitrary", "arbitrary")),
        name="in_proj",
    )(x, sc, sh, w, b, wgh, wgl, bg)


def _convqk_kernel(xm_ref, prev_ref, next_ref, cw_ref, cb_ref, wq_ref, wk_ref,
                   xc_ref, q_ref, k_ref):
    i = pl.program_id(0)
    last = pl.num_programs(0) - 1
    tm = xm_ref.shape[0]
    xm = xm_ref[...].astype(F32)
    prev_row = jnp.where(i == 0, 0.0, prev_ref[...].astype(F32)[-1:, :])
    next_row = jnp.where(i == last, 0.0, next_ref[...].astype(F32)[:1, :])
    row = lax.broadcasted_iota(jnp.int32, xm.shape, 0)
    x_prev = jnp.where(row == 0, prev_row, pltpu.roll(xm, 1, 0))
    x_next = jnp.where(row == tm - 1, next_row, pltpu.roll(xm, tm - 1, 0))
    cw = cw_ref[...]
    y = cw[0:1, :] * x_prev + cw[1:2, :] * xm + cw[2:3, :] * x_next + cb_ref[...]
    xc = _silu(y).astype(BF16)
    xc_ref[...] = xc
    for h in range(N_HEADS):
        sl = slice(h * HEAD_DIM, (h + 1) * HEAD_DIM)
        q_ref[:, sl] = _dot(xc[:, sl], wq_ref[h]).astype(BF16)
        k_ref[:, sl] = (_dot(xc[:, sl], wk_ref[h]) * (HEAD_DIM ** -0.5)).astype(BF16)


def _conv_qk(z, xm_blk, cw, cb, wq, wk, tm):
    n = z.shape[0]
    w = N_HEADS * HEAD_DIM
    halo = 16
    nb = n // halo
    per = tm // halo
    out = jax.ShapeDtypeStruct((n, w), BF16)
    return pl.pallas_call(
        _convqk_kernel,
        grid=(n // tm,),
        in_specs=[pl.BlockSpec((tm, w), lambda i: (i, xm_blk)),
                  pl.BlockSpec((halo, w), lambda i: (jnp.maximum(i * per - 1, 0), xm_blk)),
                  pl.BlockSpec((halo, w), lambda i: (jnp.minimum((i + 1) * per, nb - 1), xm_blk)),
                  pl.BlockSpec((3, w), lambda i: (0, 0)),
                  pl.BlockSpec((1, w), lambda i: (0, 0)),
                  pl.BlockSpec((N_HEADS, HEAD_DIM, HEAD_DIM), lambda i: (0, 0, 0)),
                  pl.BlockSpec((N_HEADS, HEAD_DIM, HEAD_DIM), lambda i: (0, 0, 0))],
        out_specs=[pl.BlockSpec((tm, w), lambda i: (i, 0))] * 3,
        out_shape=[out, out, out],
        compiler_params=_cparams(("arbitrary",)),
        name="conv_qk",
    )(z, z, z, cw, cb, wq, wk)


def _dot3_right(a, t_bf16):
    hi, mid, lo = _split3(a)
    return _dot(hi, t_bf16) + _dot(mid, t_bf16) + _dot(lo, t_bf16)


def _cummax_lanes(x, reverse):
    n = x.shape[1]
    lane = lax.broadcasted_iota(jnp.int32, x.shape, 1)
    s = 1
    while s < n:
        if reverse:
            x = jnp.maximum(x, jnp.where(lane < n - s, pltpu.roll(x, n - s, 1), NEG))
        else:
            x = jnp.maximum(x, jnp.where(lane >= s, pltpu.roll(x, s, 1), NEG))
        s *= 2
    return x


def _mlstm_kernel(reverse, mode, *refs):
    q_ref, k_ref, v_ref, gt_ref, c0_ref, n0_ref, m0_ref = refs[:7]
    rest = refs[7:]
    if mode == "state":
        c_out, n_out, m_out, c_scr, n_scr, m_scr = rest
    elif mode == "h":
        h_out, c_scr, n_scr, m_scr = rest
    else:
        hb_ref, xc_ref, ob_ref, mhg_ref, skip_ref, y_out, c_scr, n_scr, m_scr = rest

    @pl.when(pl.program_id(0) == 0)
    def _():
        c_scr[...] = c0_ref[...]
        n_scr[...] = n0_ref[...]
        m_scr[...] = m0_ref[...]

    L, H = CHUNK, N_HEADS
    ri = lax.broadcasted_iota(jnp.int32, (L, L), 0)
    ci = lax.broadcasted_iota(jnp.int32, (L, L), 1)
    seen_t = (ri >= ci) if reverse else (ri <= ci)
    row0 = H if reverse else 0
    last = 0 if reverse else L - 1

    gt = gt_ref[...]
    li = gt[row0:row0 + H, :]
    lf = _log_sigmoid(gt[2 * H + row0:3 * H + row0, :])
    b = _dot3_right(lf, seen_t.astype(BF16))
    r = li - b
    big_r = _cummax_lanes(r, reverse)
    m = m_scr[...]
    r_last = big_r[:, last:last + 1]
    big_m = jnp.maximum(big_r, m)
    a = jnp.exp(m - big_m)
    sc = jnp.exp(big_r - big_m)
    floor = jnp.exp(-(b + big_m))
    m_last = jnp.maximum(r_last, m)
    d1 = jnp.exp(m - m_last)
    d2 = jnp.exp(r_last - m_last)
    wk = jnp.exp(r - r_last)
    m_scr[...] = b[:, last:last + 1] + m_last

    ones8 = jnp.ones((H, L), F32)
    r_terms = jnp.concatenate([t.astype(F32) for t in _split3(r)] + [ones8], axis=0).astype(BF16)
    a_mat = _dot_nt((ri == ci).astype(BF16), r_terms).astype(BF16)
    nr_terms = [t.astype(F32) for t in _split3(-big_r)]
    sub = lax.broadcasted_iota(jnp.int32, (H, L), 0)
    ones16 = jnp.ones((16, L), BF16)

    for h in range(H):
        sl = slice(h * HEAD_DIM, (h + 1) * HEAD_DIM)
        row = slice(h, h + 1)
        qh = q_ref[:, sl]
        kh = k_ref[:, sl]
        vt = v_ref[:, sl].T
        sel = (sub == h).astype(F32)
        dyn = jnp.where(sub == 0, nr_terms[0][row], jnp.where(sub == 1, nr_terms[1][row],
                        jnp.where(sub == 2, nr_terms[2][row], 0.0)))
        b_mat = jnp.concatenate([sel, sel, sel, dyn], axis=0).astype(BF16)
        arg = _dot(a_mat, b_mat)
        st = _dot_nt(kh, qh) * jnp.exp(jnp.where(seen_t, arg, NEG))
        c_prev = c_scr[h]
        n_prev = n_scr[row, :]
        if mode != "state":
            n16 = jnp.broadcast_to(n_prev, (16, HEAD_DIM)).astype(BF16)
            intra = _dot(jnp.concatenate([vt, ones16], axis=0), st.astype(BF16))
            inter = _dot_nt(jnp.concatenate([c_prev.astype(BF16), n16], axis=0), qh)
            num = a[row] * inter[:L] + sc[row] * intra[:L]
            den = a[row] * inter[L:L + 1] + sc[row] * intra[L:L + 1]
            ht = num * (1.0 / jnp.maximum(jnp.abs(den), floor[row]))
            if mode == "h":
                h_out[h] = ht.astype(h_out.dtype)
            else:
                hs = ht + hb_ref[h].astype(F32)
                mu = jnp.mean(hs, axis=0, keepdims=True)
                hc = hs - mu
                var = jnp.mean(hc * hc, axis=0, keepdims=True)
                hn = (hc * lax.rsqrt(var + LN_EPS)).T
                y = _sigmoid(ob_ref[:, sl].astype(F32)) * (
                    hn * mhg_ref[:, sl] + skip_ref[:, sl] * xc_ref[:, sl].astype(F32))
                y_out[:, sl] = y.astype(y_out.dtype)
        wk16 = jnp.broadcast_to(wk[row], (16, L)).astype(BF16)
        vtw = (vt.astype(F32) * wk[row]).astype(BF16)
        upd = _dot(jnp.concatenate([vtw, wk16], axis=0), kh)
        c_scr[h] = d1[row] * c_prev + d2[row] * upd[:L]
        n_scr[row, :] = d1[row] * n_prev + d2[row] * upd[L:L + 1]

    if mode == "state":
        c_out[...] = c_scr[...]
        n_out[...] = n_scr[...]
        m_out[...] = m_scr[...]


def _mlstm(mode, reverse, q, k, z, v_blk, gt, state, extra=()):
    n = q.shape[0]
    nc = n // CHUNK
    w = N_HEADS * HEAD_DIM
    c0, n0, m0 = state
    pos = (lambda c: nc - 1 - c) if reverse else (lambda c: c)
    row = lambda blk: pl.BlockSpec((CHUNK, w), lambda c: (pos(c), blk))
    full = lambda a: pl.BlockSpec(a.shape, lambda c: (0,) * a.ndim)
    ht_spec = pl.BlockSpec((None, N_HEADS, HEAD_DIM, CHUNK), lambda c: (pos(c), 0, 0, 0))
    in_specs = [row(0), row(0), row(v_blk),
                pl.BlockSpec((4 * N_HEADS, CHUNK), lambda c: (0, pos(c))),
                full(c0), full(n0), full(m0)]
    args = [q, k, z, gt, c0, n0, m0]
    scratch = [pltpu.VMEM(c0.shape, F32), pltpu.VMEM(n0.shape, F32), pltpu.VMEM(m0.shape, F32)]
    if mode == "state":
        out_specs = [full(c0), full(n0), full(m0)]
        out_shape = [jax.ShapeDtypeStruct(a.shape, F32) for a in state]
    elif mode == "h":
        out_specs = ht_spec
        out_shape = jax.ShapeDtypeStruct((nc, N_HEADS, HEAD_DIM, CHUNK), BF16)
    else:
        out_specs = row(0)
        out_shape = jax.ShapeDtypeStruct((n, w), BF16)
        hb, xc, ob_blk, mhg, skip = extra
        in_specs += [ht_spec, row(0), row(ob_blk), full(mhg), full(skip)]
        args += [hb, xc, z, mhg, skip]
    return pl.pallas_call(
        functools.partial(_mlstm_kernel, reverse, mode),
        grid=(nc,),
        in_specs=in_specs, out_specs=out_specs, out_shape=out_shape,
        scratch_shapes=scratch,
        compiler_params=_cparams(("arbitrary",)),
        name="mlstm_%s_%s" % (mode, "bwd" if reverse else "fwd"),
    )(*args)


def _merge_kernel(ua_ref, va_ref, yb_ref, ga0_ref, ga1_ref, gb0_ref, gb1_ref, x_ref,
                  ws_ref, bs_ref, sg_ref, sb_ref, pa_ref, pb_ref, wo_ref,
                  g1_ref, l1g_ref, l1b_ref, sc2_ref, sh2_ref, wrh_ref, wrl_ref, br_ref,
                  x1_ref, u2_ref, route_ref, route_t_ref, cnt_ref, a_scr, run_scr):
    tm = x_ref.shape[0]

    @pl.when(pl.program_id(0) == 0)
    def _():
        run_scr[...] = jnp.zeros_like(run_scr)

    vn = (_ln(_gelu_tanh(va_ref[...].astype(F32))) * sg_ref[...] + sb_ref[...]).astype(BF16)
    for c in range(tm // CHUNK):
        rows = slice(c * CHUNK, (c + 1) * CHUNK)
        for g in range(N_HEADS):
            cols = slice(g * HEAD_DIM, (g + 1) * HEAD_DIM)
            mixed = _dot(ws_ref[g], vn[rows, cols]) + bs_ref[:, cols]
            a_scr[rows, cols] = (_gelu_tanh(ua_ref[rows, cols].astype(F32)) * mixed).astype(BF16)

    pa = _dot(a_scr[...], pa_ref[...])
    pb = _dot(yb_ref[...], pb_ref[...])
    ga = jnp.concatenate([ga0_ref[...], ga1_ref[...]], axis=1).astype(F32)
    gb = jnp.concatenate([gb0_ref[...], gb1_ref[...]], axis=1).astype(F32)
    mrg = (_sigmoid(ga) * pa + _sigmoid(gb) * pb).astype(BF16)
    y = _dot(mrg, wo_ref[...])
    x1 = _ln(ALPHA * x_ref[...] + g1_ref[...] * y) * l1g_ref[...] + l1b_ref[...]
    x1_ref[...] = x1
    u2 = _ln(x1) * (1.0 + sc2_ref[...]) + sh2_ref[...]
    u2_ref[...] = u2

    u_hi, u_lo = _split2(u2)
    wrh = wrh_ref[...]
    logit = _dot(u_hi, wrh) + _dot(u_lo, wrh) + _dot(u_hi, wrl_ref[...]) + br_ref[...]
    lane = lax.broadcasted_iota(jnp.int32, logit.shape, 1)
    lane_f = lane.astype(F32)
    is_g = lane < N_GROUPS
    gmax = jnp.max(jnp.where(is_g, logit, NEG), axis=-1, keepdims=True)
    g_sel = jnp.min(jnp.where(is_g & (logit == gmax), lane_f, 1e9), axis=-1, keepdims=True)
    p_g = 1.0 / jnp.sum(jnp.where(is_g, jnp.exp(logit - gmax), 0.0), axis=-1, keepdims=True)
    lo = N_GROUPS + EXP_PER_GROUP * g_sel
    in_grp = (lane_f >= lo) & (lane_f < lo + EXP_PER_GROUP)
    el = jnp.where(in_grp, logit, NEG)
    e1max = jnp.max(el, axis=-1, keepdims=True)
    l1 = jnp.min(jnp.where(in_grp & (el == e1max), lane_f, 1e9), axis=-1, keepdims=True)
    el2 = jnp.where(lane_f == l1, NEG, el)
    e2max = jnp.max(el2, axis=-1, keepdims=True)
    l2 = jnp.min(jnp.where(in_grp & (el2 == e2max), lane_f, 1e9), axis=-1, keepdims=True)
    zsum = jnp.sum(jnp.where(in_grp, jnp.exp(el - e1max), 0.0), axis=-1, keepdims=True)
    p1 = 1.0 / zsum
    p2 = jnp.exp(e2max - e1max) / zsum
    w1 = p_g * p1 / (p1 + p2)
    w2 = p_g * p2 / (p1 + p2)
    e1 = l1 - N_GROUPS
    e2 = l2 - N_GROUPS

    oh1 = (lane_f == e1).astype(BF16)
    oh2 = (lane_f == e2).astype(BF16)
    ri = lax.broadcasted_iota(jnp.int32, (tm, tm), 0)
    ci = lax.broadcasted_iota(jnp.int32, (tm, tm), 1)
    strict = (ci < ri).astype(BF16)
    run = run_scr[0:1, :]
    cnt1 = jnp.sum(oh1.astype(F32), axis=0, keepdims=True)
    cnt2 = jnp.sum(oh2.astype(F32), axis=0, keepdims=True)
    pre1 = _dot(strict, oh1) + run
    pre2 = _dot(strict, oh2) + run + cnt1
    rank1 = jnp.sum(oh1.astype(F32) * pre1, axis=-1, keepdims=True)
    rank2 = jnp.sum(oh2.astype(F32) * pre2, axis=-1, keepdims=True)
    new_run = run + cnt1 + cnt2
    run_scr[...] = jnp.broadcast_to(new_run, run_scr.shape)
    cnt_ref[...] = jnp.broadcast_to(new_run, cnt_ref.shape)
    route = jnp.where(lane == 0, e1, 0.0)
    route = jnp.where(lane == 1, e2, route)
    route = jnp.where(lane == 2, w1, route)
    route = jnp.where(lane == 3, w2, route)
    route = jnp.where(lane == 4, rank1, route)
    route = jnp.where(lane == 5, rank2, route)
    route_ref[...] = route
    route_t_ref[...] = route.T[:route_t_ref.shape[0], :]


def _merge(z, yb, x, ws, bs, sg, sb, pa, pb, wo, g1, l1g, l1b, sc2, sh2, wrh, wrl, br, tm):
    n, d = x.shape
    w = N_HEADS * HEAD_DIM
    zc = lambda blk: pl.BlockSpec((tm, w), lambda i: (i, blk))
    full = lambda a: pl.BlockSpec(a.shape, lambda i: (0,) * a.ndim)
    consts = [ws, bs, sg, sb, pa, pb, wo, g1, l1g, l1b, sc2, sh2, wrh, wrl, br]
    return pl.pallas_call(
        _merge_kernel,
        grid=(n // tm,),
        in_specs=[zc(0), zc(1), pl.BlockSpec((tm, w), lambda i: (i, 0)),
                  zc(5), zc(6), zc(7), zc(8),
                  pl.BlockSpec((tm, d), lambda i: (i, 0))] + [full(a) for a in consts],
        out_specs=[pl.BlockSpec((tm, d), lambda i: (i, 0)),
                   pl.BlockSpec((tm, d), lambda i: (i, 0)),
                   pl.BlockSpec((tm, LANES), lambda i: (i, 0)),
                   pl.BlockSpec((8, tm), lambda i: (0, i)),
                   pl.BlockSpec((8, LANES), lambda i: (0, 0))],
        out_shape=[jax.ShapeDtypeStruct((n, d), F32),
                   jax.ShapeDtypeStruct((n, d), F32),
                   jax.ShapeDtypeStruct((n, LANES), F32),
                   jax.ShapeDtypeStruct((8, n), F32),
                   jax.ShapeDtypeStruct((8, LANES), F32)],
        scratch_shapes=[pltpu.VMEM((tm, w), BF16), pltpu.VMEM((8, LANES), F32)],
        compiler_params=_cparams(("arbitrary",)),
        name="merge",
    )(z, z, yb, z, z, z, z, x, *consts)


def _dispatch_kernel(d1_ref, d2_ref, u_ref, xs_in_ref, xs_ref, sem):
    del xs_in_ref
    tm = u_ref.shape[0]
    base = pl.program_id(0) * tm

    def copies(t):
        src = u_ref.at[pl.ds(t, 1), :]
        return (pltpu.make_async_copy(src, xs_ref.at[pl.ds(d1_ref[base + t], 1), :], sem),
                pltpu.make_async_copy(src, xs_ref.at[pl.ds(d2_ref[base + t], 1), :], sem))

    def start(t, carry):
        for cp in copies(t):
            cp.start()
        return carry

    def wait(t, carry):
        for cp in copies(t):
            cp.wait()
        return carry

    lax.fori_loop(0, tm, start, 0)
    lax.fori_loop(0, tm, wait, 0)


def _dispatch(dest1, dest2, u2, xs0, tm):
    n, d = u2.shape
    return pl.pallas_call(
        _dispatch_kernel,
        grid_spec=pltpu.PrefetchScalarGridSpec(
            num_scalar_prefetch=2,
            grid=(n // tm,),
            in_specs=[pl.BlockSpec((tm, d), lambda i, d1, d2: (i, 0)),
                      pl.BlockSpec(memory_space=pl.ANY)],
            out_specs=pl.BlockSpec(memory_space=pl.ANY),
            scratch_shapes=[pltpu.SemaphoreType.DMA(())]),
        out_shape=jax.ShapeDtypeStruct(xs0.shape, xs0.dtype),
        input_output_aliases={3: 0},
        compiler_params=_cparams(("arbitrary",)),
        name="dispatch",
    )(dest1, dest2, u2, xs0)


def _experts_kernel(be_ref, nb_ref, xs_ref, w1_ref, w3_ref, w2_ref, ys_ref, xb_scr, acc_scr):
    b = pl.program_id(0)
    f = pl.program_id(1)

    @pl.when(b < nb_ref[0])
    def _():
        @pl.when(f == 0)
        def _():
            xb_scr[...] = xs_ref[...].astype(BF16)

        xb = xb_scr[...]
        h1 = _dot(xb, w1_ref[...].astype(BF16))
        h3 = _dot(xb, w3_ref[...].astype(BF16))
        part = _dot((_silu(h1) * h3).astype(BF16), w2_ref[...].astype(BF16))

        @pl.when(f == 0)
        def _():
            acc_scr[...] = part

        @pl.when(f > 0)
        def _():
            acc_scr[...] += part

        @pl.when(f == pl.num_programs(1) - 1)
        def _():
            ys_ref[...] = acc_scr[...]


def _experts(block_e, nblk, xs, w1, w3, w2, max_blocks):
    d = xs.shape[1]
    de = w1.shape[2]
    nf = de // MOE_FCHUNK
    bm = MOE_BLOCK

    def live(b, nb):
        return b < nb[0]

    def x_map(b, f, be, nb):
        return (jnp.minimum(b, nb[0] - 1), 0)

    def w13_map(b, f, be, nb):
        return (be[jnp.minimum(b, nb[0] - 1)], 0, jnp.where(live(b, nb), f, nf - 1))

    def w2_map(b, f, be, nb):
        return (be[jnp.minimum(b, nb[0] - 1)], jnp.where(live(b, nb), f, nf - 1), 0)

    return pl.pallas_call(
        _experts_kernel,
        grid_spec=pltpu.PrefetchScalarGridSpec(
            num_scalar_prefetch=2,
            grid=(max_blocks, nf),
            in_specs=[pl.BlockSpec((bm, d), x_map),
                      pl.BlockSpec((None, d, MOE_FCHUNK), w13_map),
                      pl.BlockSpec((None, d, MOE_FCHUNK), w13_map),
                      pl.BlockSpec((None, MOE_FCHUNK, d), w2_map)],
            out_specs=pl.BlockSpec((bm, d), x_map),
            scratch_shapes=[pltpu.VMEM((bm, d), BF16), pltpu.VMEM((bm, d), F32)]),
        out_shape=jax.ShapeDtypeStruct(xs.shape, F32),
        input_output_aliases={2: 0},
        compiler_params=_cparams(("arbitrary", "arbitrary")),
        name="experts",
    )(block_e, nblk, xs, w1, w3, w2)


def _combine_kernel(d1_ref, d2_ref, x1_ref, route_ref, g2_ref, lg_ref, lb_ref, ys_ref,
                    o_ref, buf, sem):
    tm = x1_ref.shape[0]
    i = pl.program_id(0)
    nsteps = pl.num_programs(0)

    def copies(step, slot, t):
        tok = step * tm + t
        return (pltpu.make_async_copy(ys_ref.at[pl.ds(d1_ref[tok], 1), :],
                                      buf.at[slot, 0, pl.ds(t, 1), :], sem.at[slot]),
                pltpu.make_async_copy(ys_ref.at[pl.ds(d2_ref[tok], 1), :],
                                      buf.at[slot, 1, pl.ds(t, 1), :], sem.at[slot]))

    def issue(step, slot):
        def body(t, carry):
            for cp in copies(step, slot, t):
                cp.start()
            return carry
        lax.fori_loop(0, tm, body, 0)

    def drain(step, slot):
        def body(t, carry):
            for cp in copies(step, slot, t):
                cp.wait()
            return carry
        lax.fori_loop(0, tm, body, 0)

    slot = lax.rem(i, 2)

    @pl.when(i == 0)
    def _():
        issue(0, 0)

    @pl.when(i + 1 < nsteps)
    def _():
        issue(i + 1, 1 - slot)

    drain(i, slot)
    route = route_ref[...]
    f = route[:, 2:3] * buf[slot, 0] + route[:, 3:4] * buf[slot, 1]
    o_ref[...] = _ln(ALPHA * x1_ref[...] + g2_ref[...] * f) * lg_ref[...] + lb_ref[...]


def _combine(dest1, dest2, x1, route, g2, lg, lb, ys, tm):
    n, d = x1.shape
    vec = pl.BlockSpec((1, d), lambda i, d1, d2: (0, 0))
    return pl.pallas_call(
        _combine_kernel,
        grid_spec=pltpu.PrefetchScalarGridSpec(
            num_scalar_prefetch=2,
            grid=(n // tm,),
            in_specs=[pl.BlockSpec((tm, d), lambda i, d1, d2: (i, 0)),
                      pl.BlockSpec((tm, LANES), lambda i, d1, d2: (i, 0)),
                      vec, vec, vec,
                      pl.BlockSpec(memory_space=pl.ANY)],
            out_specs=pl.BlockSpec((tm, d), lambda i, d1, d2: (i, 0)),
            scratch_shapes=[pltpu.VMEM((2, 2, tm, d), F32), pltpu.SemaphoreType.DMA((2,))]),
        out_shape=jax.ShapeDtypeStruct((n, d), F32),
        compiler_params=_cparams(("arbitrary",)),
        name="combine",
    )(dest1, dest2, x1, route, g2, lg, lb, ys)


def _layer(x, ctx, c, c_ctx, w_ada, b_ada, w_in, b_in, w_s, b_s, sgu_g, sgu_b, conv_w, conv_b,
           w_q, w_k, mh_g, skip, p_a, p_b, w_o, ln1_g, ln1_b, w_rg, b_rg, w_re, b_re,
           w1, w3, w2, ln2_g, ln2_b):
    n, d = x.shape
    w = N_HEADS * HEAD_DIM
    nz = 9 * w
    H = N_HEADS
    row = lambda a: a.reshape(1, -1)

    cc = jnp.zeros((8, d), F32).at[0].set(c[0]).at[1].set(c_ctx)
    mod = _ada(cc, w_ada, row(b_ada))
    sh1, sc1, g1, sh2, sc2, g2 = [mod[0:1, i * d:(i + 1) * d] for i in range(6)]
    sh1c, sc1c = mod[1:2, 0:d], mod[1:2, d:2 * d]

    def gate_lanes(t):
        cols = [t[:, nz + i * H:nz + (i + 1) * H] for i in (0, 2, 1, 3)]
        return jnp.concatenate(cols + [jnp.zeros((t.shape[0], LANES - 4 * H), F32)], axis=1)

    wg2 = gate_lanes(w_in)
    bg2 = gate_lanes(row(b_in))
    wgh = wg2.astype(BF16)
    wgl = (wg2 - wgh.astype(F32)).astype(BF16)
    w_main = w_in.astype(BF16)
    b_main = row(b_in)

    z, gt = _in_proj(x, sc1, sh1, w_main, b_main, wgh, wgl, bg2, tm=min(n, 1024))
    zc, gct = _in_proj(ctx, sc1c, sh1c, w_main, b_main, wgh, wgl, bg2, tm=ctx.shape[0])

    cw, cb = conv_w, row(conv_b)
    wq, wk = w_q.astype(BF16), w_k.astype(BF16)
    xc, q, k = _conv_qk(z, 2, cw, cb, wq, wk, tm=min(n, 512))
    _, qc, kc = _conv_qk(zc, 2, cw, cb, wq, wk, tm=ctx.shape[0])

    zero = (jnp.zeros((H, HEAD_DIM, HEAD_DIM), F32), jnp.zeros((H, HEAD_DIM), F32),
            jnp.full((H, LANES), NEG, F32))
    st_f = _mlstm("state", False, qc, kc, zc, 3, gct, zero)
    st_b = _mlstm("state", True, qc, kc, zc, 3, gct, zero)
    hb = _mlstm("h", True, q, k, z, 3, gt, st_b)
    yb = _mlstm("out", False, q, k, z, 3, gt, st_f, extra=(hb, xc, 4, row(mh_g), row(skip)))

    bs_full = jnp.repeat(b_s.T, HEAD_DIM, axis=1)
    wr = jnp.zeros((d, LANES), F32).at[:, :N_GROUPS].set(w_rg).at[:, N_GROUPS:N_GROUPS + N_EXPERTS].set(w_re)
    br = jnp.zeros((1, LANES), F32).at[0, :N_GROUPS].set(b_rg).at[0, N_GROUPS:N_GROUPS + N_EXPERTS].set(b_re)
    wrh = wr.astype(BF16)
    wrl = (wr - wrh.astype(F32)).astype(BF16)
    x1, u2, route, route_t, cnt = _merge(z, yb, x, w_s.astype(BF16), bs_full, row(sgu_g), row(sgu_b),
                                         p_a.astype(BF16), p_b.astype(BF16), w_o.astype(BF16),
                                         g1, row(ln1_g), row(ln1_b), sc2, sh2, wrh, wrl, br, tm=min(n, 256))

    bm = MOE_BLOCK
    counts = cnt[0, :N_EXPERTS].astype(jnp.int32)
    nblk_e = (counts + bm - 1) // bm
    blk_end = jnp.cumsum(nblk_e)
    row_start = (blk_end - nblk_e) * bm
    eid = jnp.arange(N_EXPERTS, dtype=F32)[:, None]

    def dest_rows(e, rank):
        start = jnp.sum(jnp.where(e[None, :] == eid, row_start[:, None], 0), axis=0)
        return start + rank.astype(jnp.int32)

    dest1 = dest_rows(route_t[0], route_t[4])
    dest2 = dest_rows(route_t[1], route_t[5])
    max_blocks = (2 * n + N_EXPERTS * (bm - 1)) // bm
    block_e = jnp.minimum(jnp.searchsorted(blk_end, jnp.arange(max_blocks), side='right'),
                          N_EXPERTS - 1).astype(jnp.int32)
    nblk = blk_end[-1:].astype(jnp.int32)

    xs = _dispatch(dest1, dest2, u2, jnp.zeros((max_blocks * bm, d), F32), tm=min(n, 256))
    ys = _experts(block_e, nblk, xs, w1, w3, w2, max_blocks)
    return _combine(dest1, dest2, x1, route, g2, row(ln2_g), row(ln2_b), ys, tm=min(n, 256))


def kernel(x, c, ctx, c_ctx, w_ada, b_ada, w_in, b_in, w_s, b_s, sgu_g, sgu_b, conv_w, conv_b, w_q, w_k, mh_g, skip, p_a, p_b, w_o, ln1_g, ln1_b, w_rg, b_rg, w_re, b_re, w1, w3, w2, ln2_g, ln2_b):
    assert x.shape[0] == 1 and w_ada.shape[0] == DEPTH == 1
    out = _layer(x[0], ctx[0], c, c_ctx, w_ada[0], b_ada[0], w_in[0], b_in[0], w_s[0], b_s[0],
                 sgu_g[0], sgu_b[0], conv_w[0], conv_b[0], w_q[0], w_k[0], mh_g[0], skip[0],
                 p_a[0], p_b[0], w_o[0], ln1_g[0], ln1_b[0], w_rg[0], b_rg[0], w_re[0], b_re[0],
                 w1[0], w3[0], w2[0], ln2_g[0], ln2_b[0])
    return out[None]
```

```python
import functools

import jax
import jax.numpy as jnp
from jax import lax
from jax.experimental import pallas as pl
from jax.experimental.pallas import tpu as pltpu

F32 = jnp.float32
BF16 = jnp.bfloat16

CHUNK = 128
N_HEADS = 8
HEAD_DIM = 128
N_GROUPS = 4
EXP_PER_GROUP = 8
N_EXPERTS = N_GROUPS * EXP_PER_GROUP
LN_EPS = 1e-5
NEG = -1e30
DEPTH = 1
ALPHA = (2 * DEPTH) ** 0.25
LANES = 128
VMEM_LIMIT = 56 * 1024 * 1024

MOE_BLOCK = 256
MOE_FCHUNK = 256


def _cparams(sem):
    return pltpu.CompilerParams(dimension_semantics=sem, vmem_limit_bytes=VMEM_LIMIT)


def _ln(x):
    mu = jnp.mean(x, axis=-1, keepdims=True)
    xc = x - mu
    var = jnp.mean(xc * xc, axis=-1, keepdims=True)
    return xc * lax.rsqrt(var + LN_EPS)


def _sigmoid(x):
    return 1.0 / (1.0 + jnp.exp(-x))


def _silu(x):
    return x * _sigmoid(x)


def _gelu_tanh(x):
    return 0.5 * x * (1.0 + jnp.tanh(0.7978845608028654 * (x + 0.044715 * (x * x * x))))


def _log_sigmoid(x):
    return jnp.minimum(x, 0.0) - jnp.log(1.0 + jnp.exp(-jnp.abs(x)))


def _dot(a, b):
    return jnp.dot(a, b, preferred_element_type=F32)


def _dot_nt(a, b):
    return lax.dot_general(a, b, (((1,), (1,)), ((), ())), preferred_element_type=F32)


def _split2(a):
    hi = a.astype(BF16)
    lo = (a - hi.astype(F32)).astype(BF16)
    return hi, lo


def _split3(a):
    hi = a.astype(BF16)
    r = a - hi.astype(F32)
    mid = r.astype(BF16)
    lo = (r - mid.astype(F32)).astype(BF16)
    return hi, mid, lo


def _ada_kernel(c_ref, w_ref, b_ref, o_ref):
    s_hi, s_lo = _split2(_silu(c_ref[...]))
    w_hi, w_lo = _split2(w_ref[...])
    o_ref[...] = _dot(s_hi, w_hi) + _dot(s_lo, w_hi) + _dot(s_hi, w_lo) + b_ref[...]


def _ada(cc, w, b):
    d, n = w.shape
    tn = 1024
    return pl.pallas_call(
        _ada_kernel,
        grid=(n // tn,),
        in_specs=[pl.BlockSpec((8, d), lambda j: (0, 0)),
                  pl.BlockSpec((d, tn), lambda j: (0, j)),
                  pl.BlockSpec((1, tn), lambda j: (0, j))],
        out_specs=pl.BlockSpec((8, tn), lambda j: (0, j)),
        out_shape=jax.ShapeDtypeStruct((8, n), F32),
        compiler_params=_cparams(("arbitrary",)),
        name="ada",
    )(cc, w, b)


def _inproj_kernel(x_ref, sc_ref, sh_ref, w_ref, b_ref, wgh_ref, wgl_ref, bg_ref,
                   z_ref, g_ref, u_scr):
    @pl.when(pl.program_id(1) == 0)
    def _():
        u = _ln(x_ref[...]) * (1.0 + sc_ref[...]) + sh_ref[...]
        u_hi, u_lo = _split2(u)
        u_scr[...] = u_hi
        wgh = wgh_ref[...]
        g = _dot(u_hi, wgh) + _dot(u_lo, wgh) + _dot(u_hi, wgl_ref[...]) + bg_ref[...]
        g_ref[...] = g.T[:g_ref.shape[0], :]

    z_ref[...] = (_dot(u_scr[...], w_ref[...]) + b_ref[...]).astype(z_ref.dtype)


def _in_proj(x, sc, sh, w, b, wgh, wgl, bg, tm):
    n, d = x.shape
    tn = 1024
    nz = w.shape[1] // tn * tn
    ng = wgh.shape[1]
    return pl.pallas_call(
        _inproj_kernel,
        grid=(n // tm, nz // tn),
        in_specs=[pl.BlockSpec((tm, d), lambda i, j: (i, 0)),
                  pl.BlockSpec((1, d), lambda i, j: (0, 0)),
                  pl.BlockSpec((1, d), lambda i, j: (0, 0)),
                  pl.BlockSpec((d, tn), lambda i, j: (0, j)),
                  pl.BlockSpec((1, tn), lambda i, j: (0, j)),
                  pl.BlockSpec((d, ng), lambda i, j: (0, 0)),
                  pl.BlockSpec((d, ng), lambda i, j: (0, 0)),
                  pl.BlockSpec((1, ng), lambda i, j: (0, 0))],
        out_specs=[pl.BlockSpec((tm, tn), lambda i, j: (i, j)),
                   pl.BlockSpec((4 * N_HEADS, tm), lambda i, j: (0, i))],
        out_shape=[jax.ShapeDtypeStruct((n, nz), BF16),
                   jax.ShapeDtypeStruct((4 * N_HEADS, n), F32)],
        scratch_shapes=[pltpu.VMEM((tm, d), BF16)],
        compiler_params=_cparams(("arbitrary", "arbitrary")),
        name="in_proj",
    )(x, sc, sh, w, b, wgh, wgl, bg)


def _convqk_kernel(xm_ref, prev_ref, next_ref, cw_ref, cb_ref, wq_ref, wk_ref,
                   xc_ref, q_ref, k_ref):
    i = pl.program_id(0)
    last = pl.num_programs(0) - 1
    tm = xm_ref.shape[0]
    xm = xm_ref[...].astype(F32)
    prev_row = jnp.where(i == 0, 0.0, prev_ref[...].astype(F32)[-1:, :])
    next_row = jnp.where(i == last, 0.0, next_ref[...].astype(F32)[:1, :])
    row = lax.broadcasted_iota(jnp.int32, xm.shape, 0)
    x_prev = jnp.where(row == 0, prev_row, pltpu.roll(xm, 1, 0))
    x_next = jnp.where(row == tm - 1, next_row, pltpu.roll(xm, tm - 1, 0))
    cw = cw_ref[...]
    y = cw[0:1, :] * x_prev + cw[1:2, :] * xm + cw[2:3, :] * x_next + cb_ref[...]
    xc = _silu(y).astype(BF16)
    xc_ref[...] = xc
    for h in range(N_HEADS):
        sl = slice(h * HEAD_DIM, (h + 1) * HEAD_DIM)
        q_ref[:, sl] = _dot(xc[:, sl], wq_ref[h]).astype(BF16)
        k_ref[:, sl] = (_dot(xc[:, sl], wk_ref[h]) * (HEAD_DIM ** -0.5)).astype(BF16)


def _conv_qk(z, xm_blk, cw, cb, wq, wk, tm):
    n = z.shape[0]
    w = N_HEADS * HEAD_DIM
    halo = 16
    nb = n // halo
    per = tm // halo
    out = jax.ShapeDtypeStruct((n, w), BF16)
    return pl.pallas_call(
        _convqk_kernel,
        grid=(n // tm,),
        in_specs=[pl.BlockSpec((tm, w), lambda i: (i, xm_blk)),
                  pl.BlockSpec((halo, w), lambda i: (jnp.maximum(i * per - 1, 0), xm_blk)),
                  pl.BlockSpec((halo, w), lambda i: (jnp.minimum((i + 1) * per, nb - 1), xm_blk)),
                  pl.BlockSpec((3, w), lambda i: (0, 0)),
                  pl.BlockSpec((1, w), lambda i: (0, 0)),
                  pl.BlockSpec((N_HEADS, HEAD_DIM, HEAD_DIM), lambda i: (0, 0, 0)),
                  pl.BlockSpec((N_HEADS, HEAD_DIM, HEAD_DIM), lambda i: (0, 0, 0))],
        out_specs=[pl.BlockSpec((tm, w), lambda i: (i, 0))] * 3,
        out_shape=[out, out, out],
        compiler_params=_cparams(("arbitrary",)),
        name="conv_qk",
    )(z, z, z, cw, cb, wq, wk)


def _dot3_right(a, t_bf16):
    hi, mid, lo = _split3(a)
    return _dot(hi, t_bf16) + _dot(mid, t_bf16) + _dot(lo, t_bf16)


def _cummax_lanes(x, reverse):
    n = x.shape[1]
    lane = lax.broadcasted_iota(jnp.int32, x.shape, 1)
    s = 1
    while s < n:
        if reverse:
            x = jnp.maximum(x, jnp.where(lane < n - s, pltpu.roll(x, n - s, 1), NEG))
        else:
            x = jnp.maximum(x, jnp.where(lane >= s, pltpu.roll(x, s, 1), NEG))
        s *= 2
    return x


def _mlstm_kernel(reverse, mode, *refs):
    q_ref, k_ref, v_ref, gt_ref, c0_ref, n0_ref, m0_ref = refs[:7]
    rest = refs[7:]
    if mode == "state":
        c_out, n_out, m_out, c_scr, n_scr, m_scr = rest
    elif mode == "h":
        h_out, c_scr, n_scr, m_scr = rest
    else:
        hb_ref, xc_ref, ob_ref, mhg_ref, skip_ref, y_out, c_scr, n_scr, m_scr = rest

    @pl.when(pl.program_id(0) == 0)
    def _():
        c_scr[...] = c0_ref[...]
        n_scr[...] = n0_ref[...]
        m_scr[...] = m0_ref[...]

    L, H = CHUNK, N_HEADS
    ri = lax.broadcasted_iota(jnp.int32, (L, L), 0)
    ci = lax.broadcasted_iota(jnp.int32, (L, L), 1)
    seen_t = (ri >= ci) if reverse else (ri <= ci)
    row0 = H if reverse else 0
    last = 0 if reverse else L - 1

    gt = gt_ref[...]
    li = gt[row0:row0 + H, :]
    lf = _log_sigmoid(gt[2 * H + row0:3 * H + row0, :])
    b = _dot3_right(lf, seen_t.astype(BF16))
    r = li - b
    big_r = _cummax_lanes(r, reverse)
    m = m_scr[...]
    r_last = big_r[:, last:last + 1]
    big_m = jnp.maximum(big_r, m)
    a = jnp.exp(m - big_m)
    sc = jnp.exp(big_r - big_m)
    floor = jnp.exp(-(b + big_m))
    m_last = jnp.maximum(r_last, m)
    d1 = jnp.exp(m - m_last)
    d2 = jnp.exp(r_last - m_last)
    wk = jnp.exp(r - r_last)
    m_scr[...] = b[:, last:last + 1] + m_last

    ones8 = jnp.ones((H, L), F32)
    r_terms = jnp.concatenate([t.astype(F32) for t in _split3(r)] + [ones8], axis=0).astype(BF16)
    a_mat = _dot_nt((ri == ci).astype(BF16), r_terms).astype(BF16)
    nr_terms = [t.astype(F32) for t in _split3(-big_r)]
    sub = lax.broadcasted_iota(jnp.int32, (H, L), 0)
    ones16 = jnp.ones((16, L), BF16)

    for h in range(H):
        sl = slice(h * HEAD_DIM, (h + 1) * HEAD_DIM)
        row = slice(h, h + 1)
        qh = q_ref[:, sl]
        kh = k_ref[:, sl]
        vt = v_ref[:, sl].T
        sel = (sub == h).astype(F32)
        dyn = jnp.where(sub == 0, nr_terms[0][row], jnp.where(sub == 1, nr_terms[1][row],
                        jnp.where(sub == 2, nr_terms[2][row], 0.0)))
        b_mat = jnp.concatenate([sel, sel, sel, dyn], axis=0).astype(BF16)
        arg = _dot(a_mat, b_mat)
        st = _dot_nt(kh, qh) * jnp.exp(jnp.where(seen_t, arg, NEG))
        c_prev = c_scr[h]
        n_prev = n_scr[row, :]
        if mode != "state":
            n16 = jnp.broadcast_to(n_prev, (16, HEAD_DIM)).astype(BF16)
            intra = _dot(jnp.concatenate([vt, ones16], axis=0), st.astype(BF16))
            inter = _dot_nt(jnp.concatenate([c_prev.astype(BF16), n16], axis=0), qh)
            num = a[row] * inter[:L] + sc[row] * intra[:L]
            den = a[row] * inter[L:L + 1] + sc[row] * intra[L:L + 1]
            ht = num * (1.0 / jnp.maximum(jnp.abs(den), floor[row]))
            if mode == "h":
                h_out[h] = ht.astype(h_out.dtype)
            else:
                hs = ht + hb_ref[h].astype(F32)
                mu = jnp.mean(hs, axis=0, keepdims=True)
                hc = hs - mu
                var = jnp.mean(hc * hc, axis=0, keepdims=True)
                hn = (hc * lax.rsqrt(var + LN_EPS)).T
                y = _sigmoid(ob_ref[:, sl].astype(F32)) * (
                    hn * mhg_ref[:, sl] + skip_ref[:, sl] * xc_ref[:, sl].astype(F32))
                y_out[:, sl] = y.astype(y_out.dtype)
        wk16 = jnp.broadcast_to(wk[row], (16, L)).astype(BF16)
        vtw = (vt.astype(F32) * wk[row]).astype(BF16)
        upd = _dot(jnp.concatenate([vtw, wk16], axis=0), kh)
        c_scr[h] = d1[row] * c_prev + d2[row] * upd[:L]
        n_scr[row, :] = d1[row] * n_prev + d2[row] * upd[L:L + 1]

    if mode == "state":
        c_out[...] = c_scr[...]
        n_out[...] = n_scr[...]
        m_out[...] = m_scr[...]


def _mlstm(mode, reverse, q, k, z, v_blk, gt, state, extra=()):
    n = q.shape[0]
    nc = n // CHUNK
    w = N_HEADS * HEAD_DIM
    c0, n0, m0 = state
    pos = (lambda c: nc - 1 - c) if reverse else (lambda c: c)
    row = lambda blk: pl.BlockSpec((CHUNK, w), lambda c: (pos(c), blk))
    full = lambda a: pl.BlockSpec(a.shape, lambda c: (0,) * a.ndim)
    ht_spec = pl.BlockSpec((None, N_HEADS, HEAD_DIM, CHUNK), lambda c: (pos(c), 0, 0, 0))
    in_specs = [row(0), row(0), row(v_blk),
                pl.BlockSpec((4 * N_HEADS, CHUNK), lambda c: (0, pos(c))),
                full(c0), full(n0), full(m0)]
    args = [q, k, z, gt, c0, n0, m0]
    scratch = [pltpu.VMEM(c0.shape, F32), pltpu.VMEM(n0.shape, F32), pltpu.VMEM(m0.shape, F32)]
    if mode == "state":
        out_specs = [full(c0), full(n0), full(m0)]
        out_shape = [jax.ShapeDtypeStruct(a.shape, F32) for a in state]
    elif mode == "h":
        out_specs = ht_spec
        out_shape = jax.ShapeDtypeStruct((nc, N_HEADS, HEAD_DIM, CHUNK), BF16)
    else:
        out_specs = row(0)
        out_shape = jax.ShapeDtypeStruct((n, w), BF16)
        hb, xc, ob_blk, mhg, skip = extra
        in_specs += [ht_spec, row(0), row(ob_blk), full(mhg), full(skip)]
        args += [hb, xc, z, mhg, skip]
    return pl.pallas_call(
        functools.partial(_mlstm_kernel, reverse, mode),
        grid=(nc,),
        in_specs=in_specs, out_specs=out_specs, out_shape=out_shape,
        scratch_shapes=scratch,
        compiler_params=_cparams(("arbitrary",)),
        name="mlstm_%s_%s" % (mode, "bwd" if reverse else "fwd"),
    )(*args)


def _merge_kernel(ua_ref, va_ref, yb_ref, ga0_ref, ga1_ref, gb0_ref, gb1_ref, x_ref,
                  ws_ref, bs_ref, sg_ref, sb_ref, pa_ref, pb_ref, wo_ref,
                  g1_ref, l1g_ref, l1b_ref, sc2_ref, sh2_ref, wrh_ref, wrl_ref, br_ref,
                  x1_ref, u2_ref, route_ref, route_t_ref, cnt_ref, a_scr, run_scr):
    tm = x_ref.shape[0]

    @pl.when(pl.program_id(0) == 0)
    def _():
        run_scr[...] = jnp.zeros_like(run_scr)

    vn = (_ln(_gelu_tanh(va_ref[...].astype(F32))) * sg_ref[...] + sb_ref[...]).astype(BF16)
    for c in range(tm // CHUNK):
        rows = slice(c * CHUNK, (c + 1) * CHUNK)
        for g in range(N_HEADS):
            cols = slice(g * HEAD_DIM, (g + 1) * HEAD_DIM)
            mixed = _dot(ws_ref[g], vn[rows, cols]) + bs_ref[:, cols]
            a_scr[rows, cols] = (_gelu_tanh(ua_ref[rows, cols].astype(F32)) * mixed).astype(BF16)

    pa = _dot(a_scr[...], pa_ref[...])
    pb = _dot(yb_ref[...], pb_ref[...])
    ga = jnp.concatenate([ga0_ref[...], ga1_ref[...]], axis=1).astype(F32)
    gb = jnp.concatenate([gb0_ref[...], gb1_ref[...]], axis=1).astype(F32)
    mrg = (_sigmoid(ga) * pa + _sigmoid(gb) * pb).astype(BF16)
    y = _dot(mrg, wo_ref[...])
    x1 = _ln(ALPHA * x_ref[...] + g1_ref[...] * y) * l1g_ref[...] + l1b_ref[...]
    x1_ref[...] = x1
    u2 = _ln(x1) * (1.0 + sc2_ref[...]) + sh2_ref[...]
    u2_ref[...] = u2

    u_hi, u_lo = _split2(u2)
    wrh = wrh_ref[...]
    logit = _dot(u_hi, wrh) + _dot(u_lo, wrh) + _dot(u_hi, wrl_ref[...]) + br_ref[...]
    lane = lax.broadcasted_iota(jnp.int32, logit.shape, 1)
    lane_f = lane.astype(F32)
    is_g = lane < N_GROUPS
    gmax = jnp.max(jnp.where(is_g, logit, NEG), axis=-1, keepdims=True)
    g_sel = jnp.min(jnp.where(is_g & (logit == gmax), lane_f, 1e9), axis=-1, keepdims=True)
    p_g = 1.0 / jnp.sum(jnp.where(is_g, jnp.exp(logit - gmax), 0.0), axis=-1, keepdims=True)
    lo = N_GROUPS + EXP_PER_GROUP * g_sel
    in_grp = (lane_f >= lo) & (lane_f < lo + EXP_PER_GROUP)
    el = jnp.where(in_grp, logit, NEG)
    e1max = jnp.max(el, axis=-1, keepdims=True)
    l1 = jnp.min(jnp.where(in_grp & (el == e1max), lane_f, 1e9), axis=-1, keepdims=True)
    el2 = jnp.where(lane_f == l1, NEG, el)
    e2max = jnp.max(el2, axis=-1, keepdims=True)
    l2 = jnp.min(jnp.where(in_grp & (el2 == e2max), lane_f, 1e9), axis=-1, keepdims=True)
    zsum = jnp.sum(jnp.where(in_grp, jnp.exp(el - e1max), 0.0), axis=-1, keepdims=True)
    p1 = 1.0 / zsum
    p2 = jnp.exp(e2max - e1max) / zsum
    w1 = p_g * p1 / (p1 + p2)
    w2 = p_g * p2 / (p1 + p2)
    e1 = l1 - N_GROUPS
    e2 = l2 - N_GROUPS

    oh1 = (lane_f == e1).astype(BF16)
    oh2 = (lane_f == e2).astype(BF16)
    ri = lax.broadcasted_iota(jnp.int32, (tm, tm), 0)
    ci = lax.broadcasted_iota(jnp.int32, (tm, tm), 1)
    strict = (ci < ri).astype(BF16)
    run = run_scr[0:1, :]
    cnt1 = jnp.sum(oh1.astype(F32), axis=0, keepdims=True)
    cnt2 = jnp.sum(oh2.astype(F32), axis=0, keepdims=True)
    pre1 = _dot(strict, oh1) + run
    pre2 = _dot(strict, oh2) + run + cnt1
    rank1 = jnp.sum(oh1.astype(F32) * pre1, axis=-1, keepdims=True)
    rank2 = jnp.sum(oh2.astype(F32) * pre2, axis=-1, keepdims=True)
    new_run = run + cnt1 + cnt2
    run_scr[...] = jnp.broadcast_to(new_run, run_scr.shape)
    cnt_ref[...] = jnp.broadcast_to(new_run, cnt_ref.shape)
    route = jnp.where(lane == 0, e1, 0.0)
    route = jnp.where(lane == 1, e2, route)
    route = jnp.where(lane == 2, w1, route)
    route = jnp.where(lane == 3, w2, route)
    route = jnp.where(lane == 4, rank1, route)
    route = jnp.where(lane == 5, rank2, route)
    route_ref[...] = route
    route_t_ref[...] = route.T[:route_t_ref.shape[0], :]


def _merge(z, yb, x, ws, bs, sg, sb, pa, pb, wo, g1, l1g, l1b, sc2, sh2, wrh, wrl, br, tm):
    n, d = x.shape
    w = N_HEADS * HEAD_DIM
    zc = lambda blk: pl.BlockSpec((tm, w), lambda i: (i, blk))
    full = lambda a: pl.BlockSpec(a.shape, lambda i: (0,) * a.ndim)
    consts = [ws, bs, sg, sb, pa, pb, wo, g1, l1g, l1b, sc2, sh2, wrh, wrl, br]
    return pl.pallas_call(
        _merge_kernel,
        grid=(n // tm,),
        in_specs=[zc(0), zc(1), pl.BlockSpec((tm, w), lambda i: (i, 0)),
                  zc(5), zc(6), zc(7), zc(8),
                  pl.BlockSpec((tm, d), lambda i: (i, 0))] + [full(a) for a in consts],
        out_specs=[pl.BlockSpec((tm, d), lambda i: (i, 0)),
                   pl.BlockSpec((tm, d), lambda i: (i, 0)),
                   pl.BlockSpec((tm, LANES), lambda i: (i, 0)),
                   pl.BlockSpec((8, tm), lambda i: (0, i)),
                   pl.BlockSpec((8, LANES), lambda i: (0, 0))],
        out_shape=[jax.ShapeDtypeStruct((n, d), F32),
                   jax.ShapeDtypeStruct((n, d), F32),
                   jax.ShapeDtypeStruct((n, LANES), F32),
                   jax.ShapeDtypeStruct((8, n), F32),
                   jax.ShapeDtypeStruct((8, LANES), F32)],
        scratch_shapes=[pltpu.VMEM((tm, w), BF16), pltpu.VMEM((8, LANES), F32)],
        compiler_params=_cparams(("arbitrary",)),
        name="merge",
    )(z, z, yb, z, z, z, z, x, *consts)


def _moe_kernel(be_ref, slot_ref, kidx_ref, nk_ref, nxt_ref, nb_ref, src_ref, dst_ref,
                u_hbm, w1_hbm, w3_hbm, w2_hbm, y_hbm,
                xbuf, obuf, w1b, w3b, w2b, st1, st3, st2, gsem, ssem, wsem, zsem):
    b = pl.program_id(0)
    nb = nb_ref[0]
    bm = xbuf.shape[1]
    nf = w1b.shape[1]
    fch = w1b.shape[3]
    pad_rows = y_hbm.shape[0] - 2 * bm

    def gather_start(blk, par):
        base = blk * bm
        for t in range(bm):
            pltpu.make_async_copy(u_hbm.at[pl.ds(src_ref[base + t], 1), :],
                                  xbuf.at[par, pl.ds(t, 1), :], gsem.at[par]).start()

    def gather_wait(par):
        pltpu.make_async_copy(u_hbm.at[pl.ds(0, bm), :], xbuf.at[par], gsem.at[par]).wait()

    def scatter_start(blk, par):
        base = blk * bm
        for t in range(bm):
            pltpu.make_async_copy(obuf.at[par, pl.ds(t, 1), :],
                                  y_hbm.at[pl.ds(dst_ref[base + t], 1), :], ssem.at[par]).start()

    def scatter_wait(par):
        pltpu.make_async_copy(obuf.at[par], y_hbm.at[pl.ds(0, bm), :], ssem.at[par]).wait()

    def w_copies(e, f):
        col = pl.multiple_of(f * fch, fch)
        return (pltpu.make_async_copy(w1_hbm.at[e, :, pl.ds(col, fch)], st1, wsem.at[0]),
                pltpu.make_async_copy(w3_hbm.at[e, :, pl.ds(col, fch)], st3, wsem.at[1]),
                pltpu.make_async_copy(w2_hbm.at[e, pl.ds(col, fch), :], st2, wsem.at[2]))

    def w_start(e, f):
        for cp in w_copies(e, f):
            cp.start()

    def w_finish(e, f, s):
        for cp in w_copies(e, f):
            cp.wait()
        w1b[s, f] = st1[...].astype(BF16)
        w3b[s, f] = st3[...].astype(BF16)
        w2b[s, f] = st2[...].astype(BF16)

    @pl.when(b < nb)
    def _():
        par = lax.rem(b, 2)
        e = be_ref[b]
        s = slot_ref[b]
        k = kidx_ref[b]
        nk = nk_ref[b]
        e_next = nxt_ref[b]

        @pl.when(b == 0)
        def _():
            obuf[0] = jnp.zeros(obuf.shape[1:], obuf.dtype)
            zero = [pltpu.make_async_copy(obuf.at[0], y_hbm.at[pl.ds(pad_rows + i * bm, bm), :], zsem)
                    for i in range(2)]
            for cp in zero:
                cp.start()
            gather_start(0, 0)
            for f in range(nf):
                w_start(e, f)
                w_finish(e, f, s)
            for cp in zero:
                cp.wait()

        gather_wait(par)

        @pl.when(b >= 2)
        def _():
            scatter_wait(par)

        g_lo = (k * nf) // nk
        n_groups = jnp.where(e_next < N_EXPERTS, ((k + 1) * nf) // nk - g_lo, 0)

        @pl.when(n_groups > 0)
        def _():
            w_start(e_next, g_lo)

        gather_start(jnp.minimum(b + 1, nb - 1), 1 - par)
        x = xbuf[par].astype(BF16)
        acc = None
        for f in range(nf):
            h1 = _dot(x, w1b[s, f])
            h3 = _dot(x, w3b[s, f])
            part = _dot((_silu(h1) * h3).astype(BF16), w2b[s, f])
            acc = part if acc is None else acc + part
        obuf[par] = acc
        scatter_start(b, par)

        @pl.when(n_groups > 0)
        def _():
            w_finish(e_next, g_lo, 1 - s)

        def more(f, carry):
            w_start(e_next, f)
            w_finish(e_next, f, 1 - s)
            return carry

        lax.fori_loop(g_lo + 1, g_lo + n_groups, more, 0)

        @pl.when(b == nb - 1)
        def _():
            gather_wait(1 - par)
            scatter_wait(par)

            @pl.when(nb >= 2)
            def _():
                scatter_wait(1 - par)


def _moe(tables, src, dst, u2, w1, w3, w2, max_blocks, n_out_rows):
    n, d = u2.shape
    de = w1.shape[2]
    bm = MOE_BLOCK
    nf = de // MOE_FCHUNK
    any_spec = pl.BlockSpec(memory_space=pl.ANY)
    return pl.pallas_call(
        _moe_kernel,
        grid_spec=pltpu.PrefetchScalarGridSpec(
            num_scalar_prefetch=8,
            grid=(max_blocks,),
            in_specs=[any_spec] * 4,
            out_specs=any_spec,
            scratch_shapes=[pltpu.VMEM((2, bm, d), F32), pltpu.VMEM((2, bm, d), F32),
                            pltpu.VMEM((2, nf, d, MOE_FCHUNK), BF16),
                            pltpu.VMEM((2, nf, d, MOE_FCHUNK), BF16),
                            pltpu.VMEM((2, nf, MOE_FCHUNK, d), BF16),
                            pltpu.VMEM((d, MOE_FCHUNK), F32), pltpu.VMEM((d, MOE_FCHUNK), F32),
                            pltpu.VMEM((MOE_FCHUNK, d), F32),
                            pltpu.SemaphoreType.DMA((2,)), pltpu.SemaphoreType.DMA((2,)),
                            pltpu.SemaphoreType.DMA((3,)), pltpu.SemaphoreType.DMA(())]),
        out_shape=jax.ShapeDtypeStruct((n_out_rows, d), F32),
        compiler_params=_cparams(("arbitrary",)),
        name="moe",
    )(*tables, src, dst, u2, w1, w3, w2)


def _combine_kernel(x1_ref, route_ref, y0_ref, y1_ref, g2_ref, lg_ref, lb_ref, o_ref):
    route = route_ref[...]
    f = route[:, 2:3] * y0_ref[...] + route[:, 3:4] * y1_ref[...]
    o_ref[...] = _ln(ALPHA * x1_ref[...] + g2_ref[...] * f) * lg_ref[...] + lb_ref[...]


def _combine(x1, route, y2, g2, lg, lb, tm):
    n, d = x1.shape
    vec = pl.BlockSpec((1, d), lambda i: (0, 0))
    nt = n // tm
    return pl.pallas_call(
        _combine_kernel,
        grid=(nt,),
        in_specs=[pl.BlockSpec((tm, d), lambda i: (i, 0)),
                  pl.BlockSpec((tm, LANES), lambda i: (i, 0)),
                  pl.BlockSpec((tm, d), lambda i: (i, 0)),
                  pl.BlockSpec((tm, d), lambda i: (i + nt, 0)),
                  vec, vec, vec],
        out_specs=pl.BlockSpec((tm, d), lambda i: (i, 0)),
        out_shape=jax.ShapeDtypeStruct((n, d), F32),
        compiler_params=_cparams(("arbitrary",)),
        name="combine",
    )(x1, route, y2, y2, g2, lg, lb)


def _moe_tables(nblk_e, blk_end, dest1, dest2, n, max_blocks):
    bm = MOE_BLOCK
    i32 = jnp.int32
    blk = jnp.arange(max_blocks)
    be = jnp.minimum(jnp.searchsorted(blk_end, blk, side='right'), N_EXPERTS - 1)
    kidx = blk - (blk_end - nblk_e)[be]
    nk = jnp.maximum(nblk_e[be], 1)
    live = nblk_e > 0
    slot = ((jnp.cumsum(live) - 1)[be]) % 2
    first_live_from = jnp.flip(lax.cummin(jnp.flip(jnp.where(live, jnp.arange(N_EXPERTS), N_EXPERTS))))
    nxt = jnp.concatenate([first_live_from[1:], jnp.full((1,), N_EXPERTS)])[be]
    nb = blk_end[-1:]
    tables = [t.astype(i32) for t in (be, slot, kidx, nk, nxt, nb)]
    rows = max_blocks * bm
    tok = jnp.arange(n, dtype=i32)
    src = jnp.zeros((rows,), i32).at[dest1].set(tok).at[dest2].set(tok)
    dst = jnp.full((rows,), -1, i32).at[dest1].set(tok).at[dest2].set(n + tok)
    r = jnp.arange(rows, dtype=i32)
    dst = jnp.where(dst < 0, 2 * n + ((r // bm) % 2) * bm + r % bm, dst)
    return tables, src, dst


def _layer(x, ctx, c, c_ctx, w_ada, b_ada, w_in, b_in, w_s, b_s, sgu_g, sgu_b, conv_w, conv_b,
           w_q, w_k, mh_g, skip, p_a, p_b, w_o, ln1_g, ln1_b, w_rg, b_rg, w_re, b_re,
           w1, w3, w2, ln2_g, ln2_b):
    n, d = x.shape
    w = N_HEADS * HEAD_DIM
    nz = 9 * w
    H = N_HEADS
    row = lambda a: a.reshape(1, -1)

    cc = jnp.zeros((8, d), F32).at[0].set(c[0]).at[1].set(c_ctx)
    mod = _ada(cc, w_ada, row(b_ada))
    sh1, sc1, g1, sh2, sc2, g2 = [mod[0:1, i * d:(i + 1) * d] for i in range(6)]
    sh1c, sc1c = mod[1:2, 0:d], mod[1:2, d:2 * d]

    def gate_lanes(t):
        cols = [t[:, nz + i * H:nz + (i + 1) * H] for i in (0, 2, 1, 3)]
        return jnp.concatenate(cols + [jnp.zeros((t.shape[0], LANES - 4 * H), F32)], axis=1)

    wg2 = gate_lanes(w_in)
    bg2 = gate_lanes(row(b_in))
    wgh = wg2.astype(BF16)
    wgl = (wg2 - wgh.astype(F32)).astype(BF16)
    w_main = w_in.astype(BF16)
    b_main = row(b_in)

    z, gt = _in_proj(x, sc1, sh1, w_main, b_main, wgh, wgl, bg2, tm=min(n, 1024))
    zc, gct = _in_proj(ctx, sc1c, sh1c, w_main, b_main, wgh, wgl, bg2, tm=ctx.shape[0])

    cw, cb = conv_w, row(conv_b)
    wq, wk = w_q.astype(BF16), w_k.astype(BF16)
    xc, q, k = _conv_qk(z, 2, cw, cb, wq, wk, tm=min(n, 512))
    _, qc, kc = _conv_qk(zc, 2, cw, cb, wq, wk, tm=ctx.shape[0])

    zero = (jnp.zeros((H, HEAD_DIM, HEAD_DIM), F32), jnp.zeros((H, HEAD_DIM), F32),
            jnp.full((H, LANES), NEG, F32))
    st_f = _mlstm("state", False, qc, kc, zc, 3, gct, zero)
    st_b = _mlstm("state", True, qc, kc, zc, 3, gct, zero)
    hb = _mlstm("h", True, q, k, z, 3, gt, st_b)
    yb = _mlstm("out", False, q, k, z, 3, gt, st_f, extra=(hb, xc, 4, row(mh_g), row(skip)))

    bs_full = jnp.repeat(b_s.T, HEAD_DIM, axis=1)
    wr = jnp.zeros((d, LANES), F32).at[:, :N_GROUPS].set(w_rg).at[:, N_GROUPS:N_GROUPS + N_EXPERTS].set(w_re)
    br = jnp.zeros((1, LANES), F32).at[0, :N_GROUPS].set(b_rg).at[0, N_GROUPS:N_GROUPS + N_EXPERTS].set(b_re)
    wrh = wr.astype(BF16)
    wrl = (wr - wrh.astype(F32)).astype(BF16)
    x1, u2, route, route_t, cnt = _merge(z, yb, x, w_s.astype(BF16), bs_full, row(sgu_g), row(sgu_b),
                                         p_a.astype(BF16), p_b.astype(BF16), w_o.astype(BF16),
                                         g1, row(ln1_g), row(ln1_b), sc2, sh2, wrh, wrl, br, tm=min(n, 256))

    bm = MOE_BLOCK
    counts = cnt[0, :N_EXPERTS].astype(jnp.int32)
    nblk_e = (counts + bm - 1) // bm
    blk_end = jnp.cumsum(nblk_e)
    row_start = (blk_end - nblk_e) * bm
    eid = jnp.arange(N_EXPERTS, dtype=F32)[:, None]

    def dest_rows(e, rank):
        start = jnp.sum(jnp.where(e[None, :] == eid, row_start[:, None], 0), axis=0)
        return start + rank.astype(jnp.int32)

    dest1 = dest_rows(route_t[0], route_t[4])
    dest2 = dest_rows(route_t[1], route_t[5])
    max_blocks = (2 * n + N_EXPERTS * (bm - 1)) // bm
    tables, src, dst = _moe_tables(nblk_e, blk_end, dest1, dest2, n, max_blocks)
    y2 = _moe(tables, src, dst, u2, w1, w3, w2, max_blocks, 2 * n + 2 * bm)
    return _combine(x1, route, y2, g2, row(ln2_g), row(ln2_b), tm=min(n, 512))


def kernel(x, c, ctx, c_ctx, w_ada, b_ada, w_in, b_in, w_s, b_s, sgu_g, sgu_b, conv_w, conv_b, w_q, w_k, mh_g, skip, p_a, p_b, w_o, ln1_g, ln1_b, w_rg, b_rg, w_re, b_re, w1, w3, w2, ln2_g, ln2_b):
    assert x.shape[0] == 1 and w_ada.shape[0] == DEPTH == 1
    out = _layer(x[0], ctx[0], c, c_ctx, w_ada[0], b_ada[0], w_in[0], b_in[0], w_s[0], b_s[0],
                 sgu_g[0], sgu_b[0], conv_w[0], conv_b[0], w_q[0], w_k[0], mh_g[0], skip[0],
                 p_a[0], p_b[0], w_o[0], ln1_g[0], ln1_b[0], w_rg[0], b_rg[0], w_re[0], b_re[0],
                 w1[0], w3[0], w2[0], ln2_g[0], ln2_b[0])
    return out[None]
```

```python
import functools

import jax
import jax.numpy as jnp
from jax import lax
from jax.experimental import pallas as pl
from jax.experimental.pallas import tpu as pltpu

F32 = jnp.float32
BF16 = jnp.bfloat16

CHUNK = 128
N_HEADS = 8
HEAD_DIM = 128
N_GROUPS = 4
EXP_PER_GROUP = 8
N_EXPERTS = N_GROUPS * EXP_PER_GROUP
LN_EPS = 1e-5
NEG = -1e30
DEPTH = 1
ALPHA = (2 * DEPTH) ** 0.25
LANES = 128
VMEM_LIMIT = 56 * 1024 * 1024

MOE_BLOCK = 256
MOE_FCHUNK = 256


def _cparams(sem):
    return pltpu.CompilerParams(dimension_semantics=sem, vmem_limit_bytes=VMEM_LIMIT)


def _ln(x):
    mu = jnp.mean(x, axis=-1, keepdims=True)
    xc = x - mu
    var = jnp.mean(xc * xc, axis=-1, keepdims=True)
    return xc * lax.rsqrt(var + LN_EPS)


def _sigmoid(x):
    return 1.0 / (1.0 + jnp.exp(-x))


def _silu(x):
    return x * _sigmoid(x)


def _gelu_tanh(x):
    return 0.5 * x * (1.0 + jnp.tanh(0.7978845608028654 * (x + 0.044715 * (x * x * x))))


def _log_sigmoid(x):
    return jnp.minimum(x, 0.0) - jnp.log(1.0 + jnp.exp(-jnp.abs(x)))


def _dot(a, b):
    return jnp.dot(a, b, preferred_element_type=F32)


def _dot_nt(a, b):
    return lax.dot_general(a, b, (((1,), (1,)), ((), ())), preferred_element_type=F32)


def _split2(a):
    hi = a.astype(BF16)
    lo = (a - hi.astype(F32)).astype(BF16)
    return hi, lo


def _split3(a):
    hi = a.astype(BF16)
    r = a - hi.astype(F32)
    mid = r.astype(BF16)
    lo = (r - mid.astype(F32)).astype(BF16)
    return hi, mid, lo


SLAB = 16


def _slab_store(ref, lead, x):
    rows, d = x.shape
    per = d // SLAB // LANES
    for c in range(d // LANES):
        ref[lead + (pl.ds(c // per, rows, stride=SLAB), slice((c % per) * LANES, (c % per + 1) * LANES))] = (
            x[:, c * LANES:(c + 1) * LANES])


def _slab_load(ref, lead, rows, d):
    per = d // SLAB // LANES
    return jnp.concatenate(
        [ref[lead + (pl.ds(c // per, rows, stride=SLAB), slice((c % per) * LANES, (c % per + 1) * LANES))]
         for c in range(d // LANES)], axis=1)


def _ada_kernel(c_ref, w_ref, b_ref, o_ref):
    s_hi, s_lo = _split2(_silu(c_ref[...]))
    w_hi, w_lo = _split2(w_ref[...])
    o_ref[...] = _dot(s_hi, w_hi) + _dot(s_lo, w_hi) + _dot(s_hi, w_lo) + b_ref[...]


def _ada(cc, w, b):
    d, n = w.shape
    tn = 1024
    return pl.pallas_call(
        _ada_kernel,
        grid=(n // tn,),
        in_specs=[pl.BlockSpec((8, d), lambda j: (0, 0)),
                  pl.BlockSpec((d, tn), lambda j: (0, j)),
                  pl.BlockSpec((1, tn), lambda j: (0, j))],
        out_specs=pl.BlockSpec((8, tn), lambda j: (0, j)),
        out_shape=jax.ShapeDtypeStruct((8, n), F32),
        compiler_params=_cparams(("arbitrary",)),
        name="ada",
    )(cc, w, b)


def _inproj_kernel(x_ref, sc_ref, sh_ref, w_ref, b_ref, wgh_ref, wgl_ref, bg_ref,
                   z_ref, g_ref, u_scr):
    @pl.when(pl.program_id(1) == 0)
    def _():
        u = _ln(x_ref[...]) * (1.0 + sc_ref[...]) + sh_ref[...]
        u_hi, u_lo = _split2(u)
        u_scr[...] = u_hi
        wgh = wgh_ref[...]
        g = _dot(u_hi, wgh) + _dot(u_lo, wgh) + _dot(u_hi, wgl_ref[...]) + bg_ref[...]
        g_ref[...] = g.T[:g_ref.shape[0], :]

    z_ref[...] = (_dot(u_scr[...], w_ref[...]) + b_ref[...]).astype(z_ref.dtype)


def _in_proj(x, sc, sh, w, b, wgh, wgl, bg, tm):
    n, d = x.shape
    tn = 1024
    nz = w.shape[1] // tn * tn
    ng = wgh.shape[1]
    return pl.pallas_call(
        _inproj_kernel,
        grid=(n // tm, nz // tn),
        in_specs=[pl.BlockSpec((tm, d), lambda i, j: (i, 0)),
                  pl.BlockSpec((1, d), lambda i, j: (0, 0)),
                  pl.BlockSpec((1, d), lambda i, j: (0, 0)),
                  pl.BlockSpec((d, tn), lambda i, j: (0, j)),
                  pl.BlockSpec((1, tn), lambda i, j: (0, j)),
                  pl.BlockSpec((d, ng), lambda i, j: (0, 0)),
                  pl.BlockSpec((d, ng), lambda i, j: (0, 0)),
                  pl.BlockSpec((1, ng), lambda i, j: (0, 0))],
        out_specs=[pl.BlockSpec((tm, tn), lambda i, j: (i, j)),
                   pl.BlockSpec((4 * N_HEADS, tm), lambda i, j: (0, i))],
        out_shape=[jax.ShapeDtypeStruct((n, nz), BF16),
                   jax.ShapeDtypeStruct((4 * N_HEADS, n), F32)],
        scratch_shapes=[pltpu.VMEM((tm, d), BF16)],
        compiler_params=_cparams(("arbitrary", "arbitrary")),
        name="in_proj",
    )(x, sc, sh, w, b, wgh, wgl, bg)


def _convqk_kernel(xm_ref, prev_ref, next_ref, cw_ref, cb_ref, wq_ref, wk_ref,
                   xc_ref, q_ref, k_ref):
    i = pl.program_id(0)
    last = pl.num_programs(0) - 1
    tm = xm_ref.shape[0]
    xm = xm_ref[...].astype(F32)
    prev_row = jnp.where(i == 0, 0.0, prev_ref[...].astype(F32)[-1:, :])
    next_row = jnp.where(i == last, 0.0, next_ref[...].astype(F32)[:1, :])
    row = lax.broadcasted_iota(jnp.int32, xm.shape, 0)
    x_prev = jnp.where(row == 0, prev_row, pltpu.roll(xm, 1, 0))
    x_next = jnp.where(row == tm - 1, next_row, pltpu.roll(xm, tm - 1, 0))
    cw = cw_ref[...]
    y = cw[0:1, :] * x_prev + cw[1:2, :] * xm + cw[2:3, :] * x_next + cb_ref[...]
    xc = _silu(y).astype(BF16)
    xc_ref[...] = xc
    for h in range(N_HEADS):
        sl = slice(h * HEAD_DIM, (h + 1) * HEAD_DIM)
        q_ref[:, sl] = _dot(xc[:, sl], wq_ref[h]).astype(BF16)
        k_ref[:, sl] = (_dot(xc[:, sl], wk_ref[h]) * (HEAD_DIM ** -0.5)).astype(BF16)


def _conv_qk(z, xm_blk, cw, cb, wq, wk, tm):
    n = z.shape[0]
    w = N_HEADS * HEAD_DIM
    halo = 16
    nb = n // halo
    per = tm // halo
    out = jax.ShapeDtypeStruct((n, w), BF16)
    return pl.pallas_call(
        _convqk_kernel,
        grid=(n // tm,),
        in_specs=[pl.BlockSpec((tm, w), lambda i: (i, xm_blk)),
                  pl.BlockSpec((halo, w), lambda i: (jnp.maximum(i * per - 1, 0), xm_blk)),
                  pl.BlockSpec((halo, w), lambda i: (jnp.minimum((i + 1) * per, nb - 1), xm_blk)),
                  pl.BlockSpec((3, w), lambda i: (0, 0)),
                  pl.BlockSpec((1, w), lambda i: (0, 0)),
                  pl.BlockSpec((N_HEADS, HEAD_DIM, HEAD_DIM), lambda i: (0, 0, 0)),
                  pl.BlockSpec((N_HEADS, HEAD_DIM, HEAD_DIM), lambda i: (0, 0, 0))],
        out_specs=[pl.BlockSpec((tm, w), lambda i: (i, 0))] * 3,
        out_shape=[out, out, out],
        compiler_params=_cparams(("arbitrary",)),
        name="conv_qk",
    )(z, z, z, cw, cb, wq, wk)


def _dot3_right(a, t_bf16):
    hi, mid, lo = _split3(a)
    return _dot(hi, t_bf16) + _dot(mid, t_bf16) + _dot(lo, t_bf16)


def _cummax_lanes(x, reverse):
    n = x.shape[1]
    lane = lax.broadcasted_iota(jnp.int32, x.shape, 1)
    s = 1
    while s < n:
        if reverse:
            x = jnp.maximum(x, jnp.where(lane < n - s, pltpu.roll(x, n - s, 1), NEG))
        else:
            x = jnp.maximum(x, jnp.where(lane >= s, pltpu.roll(x, s, 1), NEG))
        s *= 2
    return x


def _mlstm_kernel(reverse, mode, *refs):
    q_ref, k_ref, v_ref, gt_ref, c0_ref, n0_ref, m0_ref = refs[:7]
    rest = refs[7:]
    if mode == "state":
        c_out, n_out, m_out, c_scr, n_scr, m_scr = rest
    elif mode == "h":
        h_out, c_scr, n_scr, m_scr = rest
    else:
        hb_ref, xc_ref, ob_ref, mhg_ref, skip_ref, y_out, c_scr, n_scr, m_scr = rest

    @pl.when(pl.program_id(0) == 0)
    def _():
        c_scr[...] = c0_ref[...]
        n_scr[...] = n0_ref[...]
        m_scr[...] = m0_ref[...]

    L, H = CHUNK, N_HEADS
    ri = lax.broadcasted_iota(jnp.int32, (L, L), 0)
    ci = lax.broadcasted_iota(jnp.int32, (L, L), 1)
    seen_t = (ri >= ci) if reverse else (ri <= ci)
    row0 = H if reverse else 0
    last = 0 if reverse else L - 1

    gt = gt_ref[...]
    li = gt[row0:row0 + H, :]
    lf = _log_sigmoid(gt[2 * H + row0:3 * H + row0, :])
    b = _dot3_right(lf, seen_t.astype(BF16))
    r = li - b
    big_r = _cummax_lanes(r, reverse)
    m = m_scr[...]
    r_last = big_r[:, last:last + 1]
    big_m = jnp.maximum(big_r, m)
    a = jnp.exp(m - big_m)
    sc = jnp.exp(big_r - big_m)
    floor = jnp.exp(-(b + big_m))
    m_last = jnp.maximum(r_last, m)
    d1 = jnp.exp(m - m_last)
    d2 = jnp.exp(r_last - m_last)
    wk = jnp.exp(r - r_last)
    m_scr[...] = b[:, last:last + 1] + m_last

    ones8 = jnp.ones((H, L), F32)
    r_terms = jnp.concatenate([t.astype(F32) for t in _split3(r)] + [ones8], axis=0).astype(BF16)
    a_mat = _dot_nt((ri == ci).astype(BF16), r_terms).astype(BF16)
    nr_terms = [t.astype(F32) for t in _split3(-big_r)]
    sub = lax.broadcasted_iota(jnp.int32, (H, L), 0)
    ones16 = jnp.ones((16, L), BF16)

    for h in range(H):
        sl = slice(h * HEAD_DIM, (h + 1) * HEAD_DIM)
        row = slice(h, h + 1)
        qh = q_ref[:, sl]
        kh = k_ref[:, sl]
        vt = v_ref[:, sl].T
        sel = (sub == h).astype(F32)
        dyn = jnp.where(sub == 0, nr_terms[0][row], jnp.where(sub == 1, nr_terms[1][row],
                        jnp.where(sub == 2, nr_terms[2][row], 0.0)))
        b_mat = jnp.concatenate([sel, sel, sel, dyn], axis=0).astype(BF16)
        arg = _dot(a_mat, b_mat)
        st = _dot_nt(kh, qh) * jnp.exp(jnp.where(seen_t, arg, NEG))
        c_prev = c_scr[h]
        n_prev = n_scr[row, :]
        if mode != "state":
            n16 = jnp.broadcast_to(n_prev, (16, HEAD_DIM)).astype(BF16)
            intra = _dot(jnp.concatenate([vt, ones16], axis=0), st.astype(BF16))
            inter = _dot_nt(jnp.concatenate([c_prev.astype(BF16), n16], axis=0), qh)
            num = a[row] * inter[:L] + sc[row] * intra[:L]
            den = a[row] * inter[L:L + 1] + sc[row] * intra[L:L + 1]
            ht = num * (1.0 / jnp.maximum(jnp.abs(den), floor[row]))
            if mode == "h":
                h_out[h] = ht.astype(h_out.dtype)
            else:
                hs = ht + hb_ref[h].astype(F32)
                mu = jnp.mean(hs, axis=0, keepdims=True)
                hc = hs - mu
                var = jnp.mean(hc * hc, axis=0, keepdims=True)
                hn = (hc * lax.rsqrt(var + LN_EPS)).T
                y = _sigmoid(ob_ref[:, sl].astype(F32)) * (
                    hn * mhg_ref[:, sl] + skip_ref[:, sl] * xc_ref[:, sl].astype(F32))
                y_out[:, sl] = y.astype(y_out.dtype)
        wk16 = jnp.broadcast_to(wk[row], (16, L)).astype(BF16)
        vtw = (vt.astype(F32) * wk[row]).astype(BF16)
        upd = _dot(jnp.concatenate([vtw, wk16], axis=0), kh)
        c_scr[h] = d1[row] * c_prev + d2[row] * upd[:L]
        n_scr[row, :] = d1[row] * n_prev + d2[row] * upd[L:L + 1]

    if mode == "state":
        c_out[...] = c_scr[...]
        n_out[...] = n_scr[...]
        m_out[...] = m_scr[...]


def _mlstm(mode, reverse, q, k, z, v_blk, gt, state, extra=()):
    n = q.shape[0]
    nc = n // CHUNK
    w = N_HEADS * HEAD_DIM
    c0, n0, m0 = state
    pos = (lambda c: nc - 1 - c) if reverse else (lambda c: c)
    row = lambda blk: pl.BlockSpec((CHUNK, w), lambda c: (pos(c), blk))
    full = lambda a: pl.BlockSpec(a.shape, lambda c: (0,) * a.ndim)
    ht_spec = pl.BlockSpec((None, N_HEADS, HEAD_DIM, CHUNK), lambda c: (pos(c), 0, 0, 0))
    in_specs = [row(0), row(0), row(v_blk),
                pl.BlockSpec((4 * N_HEADS, CHUNK), lambda c: (0, pos(c))),
                full(c0), full(n0), full(m0)]
    args = [q, k, z, gt, c0, n0, m0]
    scratch = [pltpu.VMEM(c0.shape, F32), pltpu.VMEM(n0.shape, F32), pltpu.VMEM(m0.shape, F32)]
    if mode == "state":
        out_specs = [full(c0), full(n0), full(m0)]
        out_shape = [jax.ShapeDtypeStruct(a.shape, F32) for a in state]
    elif mode == "h":
        out_specs = ht_spec
        out_shape = jax.ShapeDtypeStruct((nc, N_HEADS, HEAD_DIM, CHUNK), BF16)
    else:
        out_specs = row(0)
        out_shape = jax.ShapeDtypeStruct((n, w), BF16)
        hb, xc, ob_blk, mhg, skip = extra
        in_specs += [ht_spec, row(0), row(ob_blk), full(mhg), full(skip)]
        args += [hb, xc, z, mhg, skip]
    return pl.pallas_call(
        functools.partial(_mlstm_kernel, reverse, mode),
        grid=(nc,),
        in_specs=in_specs, out_specs=out_specs, out_shape=out_shape,
        scratch_shapes=scratch,
        compiler_params=_cparams(("arbitrary",)),
        name="mlstm_%s_%s" % (mode, "bwd" if reverse else "fwd"),
    )(*args)


def _merge_kernel(ua_ref, va_ref, yb_ref, ga0_ref, ga1_ref, gb0_ref, gb1_ref, x_ref,
                  ws_ref, bs_ref, sg_ref, sb_ref, pa_ref, pb_ref, wo_ref,
                  g1_ref, l1g_ref, l1b_ref, sc2_ref, sh2_ref, wrh_ref, wrl_ref, br_ref,
                  x1_ref, u2_ref, route_ref, route_t_ref, cnt_ref, a_scr, run_scr):
    tm = x_ref.shape[0]

    @pl.when(pl.program_id(0) == 0)
    def _():
        run_scr[...] = jnp.zeros_like(run_scr)

    vn = (_ln(_gelu_tanh(va_ref[...].astype(F32))) * sg_ref[...] + sb_ref[...]).astype(BF16)
    for c in range(tm // CHUNK):
        rows = slice(c * CHUNK, (c + 1) * CHUNK)
        for g in range(N_HEADS):
            cols = slice(g * HEAD_DIM, (g + 1) * HEAD_DIM)
            mixed = _dot(ws_ref[g], vn[rows, cols]) + bs_ref[:, cols]
            a_scr[rows, cols] = (_gelu_tanh(ua_ref[rows, cols].astype(F32)) * mixed).astype(BF16)

    pa = _dot(a_scr[...], pa_ref[...])
    pb = _dot(yb_ref[...], pb_ref[...])
    ga = jnp.concatenate([ga0_ref[...], ga1_ref[...]], axis=1).astype(F32)
    gb = jnp.concatenate([gb0_ref[...], gb1_ref[...]], axis=1).astype(F32)
    mrg = (_sigmoid(ga) * pa + _sigmoid(gb) * pb).astype(BF16)
    y = _dot(mrg, wo_ref[...])
    x1 = _ln(ALPHA * x_ref[...] + g1_ref[...] * y) * l1g_ref[...] + l1b_ref[...]
    x1_ref[...] = x1
    u2 = _ln(x1) * (1.0 + sc2_ref[...]) + sh2_ref[...]
    _slab_store(u2_ref, (), u2)

    u_hi, u_lo = _split2(u2)
    wrh = wrh_ref[...]
    logit = _dot(u_hi, wrh) + _dot(u_lo, wrh) + _dot(u_hi, wrl_ref[...]) + br_ref[...]
    lane = lax.broadcasted_iota(jnp.int32, logit.shape, 1)
    lane_f = lane.astype(F32)
    is_g = lane < N_GROUPS
    gmax = jnp.max(jnp.where(is_g, logit, NEG), axis=-1, keepdims=True)
    g_sel = jnp.min(jnp.where(is_g & (logit == gmax), lane_f, 1e9), axis=-1, keepdims=True)
    p_g = 1.0 / jnp.sum(jnp.where(is_g, jnp.exp(logit - gmax), 0.0), axis=-1, keepdims=True)
    lo = N_GROUPS + EXP_PER_GROUP * g_sel
    in_grp = (lane_f >= lo) & (lane_f < lo + EXP_PER_GROUP)
    el = jnp.where(in_grp, logit, NEG)
    e1max = jnp.max(el, axis=-1, keepdims=True)
    l1 = jnp.min(jnp.where(in_grp & (el == e1max), lane_f, 1e9), axis=-1, keepdims=True)
    el2 = jnp.where(lane_f == l1, NEG, el)
    e2max = jnp.max(el2, axis=-1, keepdims=True)
    l2 = jnp.min(jnp.where(in_grp & (el2 == e2max), lane_f, 1e9), axis=-1, keepdims=True)
    zsum = jnp.sum(jnp.where(in_grp, jnp.exp(el - e1max), 0.0), axis=-1, keepdims=True)
    p1 = 1.0 / zsum
    p2 = jnp.exp(e2max - e1max) / zsum
    w1 = p_g * p1 / (p1 + p2)
    w2 = p_g * p2 / (p1 + p2)
    e1 = l1 - N_GROUPS
    e2 = l2 - N_GROUPS

    oh1 = (lane_f == e1).astype(BF16)
    oh2 = (lane_f == e2).astype(BF16)
    ri = lax.broadcasted_iota(jnp.int32, (tm, tm), 0)
    ci = lax.broadcasted_iota(jnp.int32, (tm, tm), 1)
    strict = (ci < ri).astype(BF16)
    run = run_scr[0:1, :]
    cnt1 = jnp.sum(oh1.astype(F32), axis=0, keepdims=True)
    cnt2 = jnp.sum(oh2.astype(F32), axis=0, keepdims=True)
    pre1 = _dot(strict, oh1) + run
    pre2 = _dot(strict, oh2) + run + cnt1
    rank1 = jnp.sum(oh1.astype(F32) * pre1, axis=-1, keepdims=True)
    rank2 = jnp.sum(oh2.astype(F32) * pre2, axis=-1, keepdims=True)
    new_run = run + cnt1 + cnt2
    run_scr[...] = jnp.broadcast_to(new_run, run_scr.shape)
    cnt_ref[...] = jnp.broadcast_to(new_run, cnt_ref.shape)
    route = jnp.where(lane == 0, e1, 0.0)
    route = jnp.where(lane == 1, e2, route)
    route = jnp.where(lane == 2, w1, route)
    route = jnp.where(lane == 3, w2, route)
    route = jnp.where(lane == 4, rank1, route)
    route = jnp.where(lane == 5, rank2, route)
    route_ref[...] = route
    route_t_ref[...] = route.T[:route_t_ref.shape[0], :]


def _merge(z, yb, x, ws, bs, sg, sb, pa, pb, wo, g1, l1g, l1b, sc2, sh2, wrh, wrl, br, tm):
    n, d = x.shape
    w = N_HEADS * HEAD_DIM
    zc = lambda blk: pl.BlockSpec((tm, w), lambda i: (i, blk))
    full = lambda a: pl.BlockSpec(a.shape, lambda i: (0,) * a.ndim)
    consts = [ws, bs, sg, sb, pa, pb, wo, g1, l1g, l1b, sc2, sh2, wrh, wrl, br]
    return pl.pallas_call(
        _merge_kernel,
        grid=(n // tm,),
        in_specs=[zc(0), zc(1), pl.BlockSpec((tm, w), lambda i: (i, 0)),
                  zc(5), zc(6), zc(7), zc(8),
                  pl.BlockSpec((tm, d), lambda i: (i, 0))] + [full(a) for a in consts],
        out_specs=[pl.BlockSpec((tm, d), lambda i: (i, 0)),
                   pl.BlockSpec((tm * SLAB, d // SLAB), lambda i: (i, 0)),
                   pl.BlockSpec((tm, LANES), lambda i: (i, 0)),
                   pl.BlockSpec((8, tm), lambda i: (0, i)),
                   pl.BlockSpec((8, LANES), lambda i: (0, 0))],
        out_shape=[jax.ShapeDtypeStruct((n, d), F32),
                   jax.ShapeDtypeStruct((n * SLAB, d // SLAB), F32),
                   jax.ShapeDtypeStruct((n, LANES), F32),
                   jax.ShapeDtypeStruct((8, n), F32),
                   jax.ShapeDtypeStruct((8, LANES), F32)],
        scratch_shapes=[pltpu.VMEM((tm, w), BF16), pltpu.VMEM((8, LANES), F32)],
        compiler_params=_cparams(("arbitrary",)),
        name="merge",
    )(z, z, yb, z, z, z, z, x, *consts)


def _moe_kernel(be_ref, slot_ref, kidx_ref, nk_ref, nxt_ref, ubase_ref, nvalid_ref, nb_ref, stok_ref, sdst_ref,
                u_hbm, w1_hbm, w3_hbm, w2_hbm, y_hbm,
                xbuf, obuf, w1b, w3b, w2b, st1, st3, st2, gsem, ssem, wsem, zsem):
    b = pl.program_id(0)
    nb = nb_ref[0]
    bm = xbuf.shape[1] // SLAB
    d = xbuf.shape[2] * SLAB
    nf = w1b.shape[1]
    fch = w1b.shape[3]
    pad_rows = y_hbm.shape[0] - 2 * bm * SLAB

    def gather_start(blk, par):
        base = ubase_ref[blk]
        for t in range(bm):
            row = pl.multiple_of(stok_ref[base + t], SLAB)
            pltpu.make_async_copy(u_hbm.at[pl.ds(row, SLAB), :],
                                  xbuf.at[par, pl.ds(t * SLAB, SLAB), :], gsem.at[par]).start()

    def gather_wait(par):
        pltpu.make_async_copy(u_hbm.at[pl.ds(0, bm * SLAB), :], xbuf.at[par], gsem.at[par]).wait()

    def scatter_start(blk, par):
        base = ubase_ref[blk]
        nvalid = nvalid_ref[blk]
        pad = pad_rows + par * (bm * SLAB)
        for t in range(bm):
            row = pl.multiple_of(jnp.where(t < nvalid, sdst_ref[base + t], pad + t * SLAB), SLAB)
            pltpu.make_async_copy(obuf.at[par, pl.ds(t * SLAB, SLAB), :],
                                  y_hbm.at[pl.ds(row, SLAB), :], ssem.at[par]).start()

    def scatter_wait(par):
        pltpu.make_async_copy(obuf.at[par], y_hbm.at[pl.ds(0, bm * SLAB), :], ssem.at[par]).wait()

    def w_copies(e, f):
        col = pl.multiple_of(f * fch, fch)
        return (pltpu.make_async_copy(w1_hbm.at[e, :, pl.ds(col, fch)], st1, wsem.at[0]),
                pltpu.make_async_copy(w3_hbm.at[e, :, pl.ds(col, fch)], st3, wsem.at[1]),
                pltpu.make_async_copy(w2_hbm.at[e, pl.ds(col, fch), :], st2, wsem.at[2]))

    def w_start(e, f):
        for cp in w_copies(e, f):
            cp.start()

    def w_finish(e, f, s):
        for cp in w_copies(e, f):
            cp.wait()
        w1b[s, f] = st1[...].astype(BF16)
        w3b[s, f] = st3[...].astype(BF16)
        w2b[s, f] = st2[...].astype(BF16)

    @pl.when(b < nb)
    def _():
        par = lax.rem(b, 2)
        e = be_ref[b]
        s = slot_ref[b]
        k = kidx_ref[b]
        nk = nk_ref[b]
        e_next = nxt_ref[b]

        @pl.when(b == 0)
        def _():
            obuf[0] = jnp.zeros(obuf.shape[1:], obuf.dtype)
            zero = [pltpu.make_async_copy(obuf.at[0], y_hbm.at[pl.ds(pad_rows + i * bm * SLAB, bm * SLAB), :], zsem)
                    for i in range(2)]
            for cp in zero:
                cp.start()
            gather_start(0, 0)
            for f in range(nf):
                w_start(e, f)
                w_finish(e, f, s)
            for cp in zero:
                cp.wait()

        gather_wait(par)

        @pl.when(b >= 2)
        def _():
            scatter_wait(par)

        g_lo = (k * nf) // nk
        n_groups = jnp.where(e_next < N_EXPERTS, ((k + 1) * nf) // nk - g_lo, 0)

        @pl.when(n_groups > 0)
        def _():
            w_start(e_next, g_lo)

        gather_start(jnp.minimum(b + 1, nb - 1), 1 - par)
        x = _slab_load(xbuf, (par,), bm, d).astype(BF16)
        acc = None
        for f in range(nf):
            h1 = _dot(x, w1b[s, f])
            h3 = _dot(x, w3b[s, f])
            part = _dot((_silu(h1) * h3).astype(BF16), w2b[s, f])
            acc = part if acc is None else acc + part
        _slab_store(obuf, (par,), acc)
        scatter_start(b, par)

        @pl.when(n_groups > 0)
        def _():
            w_finish(e_next, g_lo, 1 - s)

        def more(f, carry):
            w_start(e_next, f)
            w_finish(e_next, f, 1 - s)
            return carry

        lax.fori_loop(g_lo + 1, g_lo + n_groups, more, 0)

        @pl.when(b == nb - 1)
        def _():
            gather_wait(1 - par)
            scatter_wait(par)

            @pl.when(nb >= 2)
            def _():
                scatter_wait(1 - par)


def _moe(tables, u2, w1, w3, w2, max_blocks, n_out_rows):
    d = w1.shape[1]
    de = w1.shape[2]
    bm = MOE_BLOCK
    nf = de // MOE_FCHUNK
    any_spec = pl.BlockSpec(memory_space=pl.ANY)
    return pl.pallas_call(
        _moe_kernel,
        grid_spec=pltpu.PrefetchScalarGridSpec(
            num_scalar_prefetch=len(tables),
            grid=(max_blocks,),
            in_specs=[any_spec] * 4,
            out_specs=any_spec,
            scratch_shapes=[pltpu.VMEM((2, bm * SLAB, d // SLAB), F32), pltpu.VMEM((2, bm * SLAB, d // SLAB), F32),
                            pltpu.VMEM((2, nf, d, MOE_FCHUNK), BF16),
                            pltpu.VMEM((2, nf, d, MOE_FCHUNK), BF16),
                            pltpu.VMEM((2, nf, MOE_FCHUNK, d), BF16),
                            pltpu.VMEM((d, MOE_FCHUNK), F32), pltpu.VMEM((d, MOE_FCHUNK), F32),
                            pltpu.VMEM((MOE_FCHUNK, d), F32),
                            pltpu.SemaphoreType.DMA((2,)), pltpu.SemaphoreType.DMA((2,)),
                            pltpu.SemaphoreType.DMA((3,)), pltpu.SemaphoreType.DMA(())]),
        out_shape=jax.ShapeDtypeStruct((n_out_rows * SLAB, d // SLAB), F32),
        compiler_params=_cparams(("arbitrary",)),
        name="moe",
    )(*tables, u2, w1, w3, w2)


def _combine_kernel(x1_ref, route_ref, y0_ref, y1_ref, g2_ref, lg_ref, lb_ref, o_ref):
    route = route_ref[...]
    tm, d = x1_ref.shape
    f = route[:, 2:3] * _slab_load(y0_ref, (), tm, d) + route[:, 3:4] * _slab_load(y1_ref, (), tm, d)
    o_ref[...] = _ln(ALPHA * x1_ref[...] + g2_ref[...] * f) * lg_ref[...] + lb_ref[...]


def _combine(x1, route, y2, g2, lg, lb, tm):
    n, d = x1.shape
    vec = pl.BlockSpec((1, d), lambda i: (0, 0))
    nt = n // tm
    return pl.pallas_call(
        _combine_kernel,
        grid=(nt,),
        in_specs=[pl.BlockSpec((tm, d), lambda i: (i, 0)),
                  pl.BlockSpec((tm, LANES), lambda i: (i, 0)),
                  pl.BlockSpec((tm * SLAB, d // SLAB), lambda i: (i, 0)),
                  pl.BlockSpec((tm * SLAB, d // SLAB), lambda i: (i + nt, 0)),
                  vec, vec, vec],
        out_specs=pl.BlockSpec((tm, d), lambda i: (i, 0)),
        out_shape=jax.ShapeDtypeStruct((n, d), F32),
        compiler_params=_cparams(("arbitrary",)),
        name="combine",
    )(x1, route, y2, y2, g2, lg, lb)


def _moe_tables(counts, nblk_e, blk_end, dest1, dest2, n, max_blocks):
    bm = MOE_BLOCK
    i32 = jnp.int32
    blk = jnp.arange(max_blocks)
    be = jnp.minimum(jnp.searchsorted(blk_end, blk, side='right'), N_EXPERTS - 1)
    kidx = blk - (blk_end - nblk_e)[be]
    nk = jnp.maximum(nblk_e[be], 1)
    live = nblk_e > 0
    slot = ((jnp.cumsum(live) - 1)[be]) % 2
    first_live_from = jnp.flip(lax.cummin(jnp.flip(jnp.where(live, jnp.arange(N_EXPERTS), N_EXPERTS))))
    nxt = jnp.concatenate([first_live_from[1:], jnp.full((1,), N_EXPERTS)])[be]
    ubase = (jnp.cumsum(counts) - counts)[be] + kidx * bm
    nvalid = jnp.clip(counts[be] - kidx * bm, 0, bm)
    nb = blk_end[-1:]
    j = jnp.arange(2 * n, dtype=i32)
    _, stok, sdst = lax.sort((jnp.concatenate([dest1, dest2]), (j % n) * SLAB, j * SLAB), num_keys=1)
    tail = jnp.zeros((bm,), i32)
    stok = jnp.concatenate([stok, tail])
    sdst = jnp.concatenate([sdst, tail])
    return [t.astype(i32) for t in (be, slot, kidx, nk, nxt, ubase, nvalid, nb, stok, sdst)]


def _layer(x, ctx, c, c_ctx, w_ada, b_ada, w_in, b_in, w_s, b_s, sgu_g, sgu_b, conv_w, conv_b,
           w_q, w_k, mh_g, skip, p_a, p_b, w_o, ln1_g, ln1_b, w_rg, b_rg, w_re, b_re,
           w1, w3, w2, ln2_g, ln2_b):
    n, d = x.shape
    w = N_HEADS * HEAD_DIM
    nz = 9 * w
    H = N_HEADS
    row = lambda a: a.reshape(1, -1)

    cc = jnp.zeros((8, d), F32).at[0].set(c[0]).at[1].set(c_ctx)
    mod = _ada(cc, w_ada, row(b_ada))
    sh1, sc1, g1, sh2, sc2, g2 = [mod[0:1, i * d:(i + 1) * d] for i in range(6)]
    sh1c, sc1c = mod[1:2, 0:d], mod[1:2, d:2 * d]

    def gate_lanes(t):
        cols = [t[:, nz + i * H:nz + (i + 1) * H] for i in (0, 2, 1, 3)]
        return jnp.concatenate(cols + [jnp.zeros((t.shape[0], LANES - 4 * H), F32)], axis=1)

    wg2 = gate_lanes(w_in)
    bg2 = gate_lanes(row(b_in))
    wgh = wg2.astype(BF16)
    wgl = (wg2 - wgh.astype(F32)).astype(BF16)
    w_main = w_in.astype(BF16)
    b_main = row(b_in)

    z, gt = _in_proj(x, sc1, sh1, w_main, b_main, wgh, wgl, bg2, tm=min(n, 1024))
    zc, gct = _in_proj(ctx, sc1c, sh1c, w_main, b_main, wgh, wgl, bg2, tm=ctx.shape[0])

    cw, cb = conv_w, row(conv_b)
    wq, wk = w_q.astype(BF16), w_k.astype(BF16)
    xc, q, k = _conv_qk(z, 2, cw, cb, wq, wk, tm=min(n, 512))
    _, qc, kc = _conv_qk(zc, 2, cw, cb, wq, wk, tm=ctx.shape[0])

    zero = (jnp.zeros((H, HEAD_DIM, HEAD_DIM), F32), jnp.zeros((H, HEAD_DIM), F32),
            jnp.full((H, LANES), NEG, F32))
    st_f = _mlstm("state", False, qc, kc, zc, 3, gct, zero)
    st_b = _mlstm("state", True, qc, kc, zc, 3, gct, zero)
    hb = _mlstm("h", True, q, k, z, 3, gt, st_b)
    yb = _mlstm("out", False, q, k, z, 3, gt, st_f, extra=(hb, xc, 4, row(mh_g), row(skip)))

    bs_full = jnp.repeat(b_s.T, HEAD_DIM, axis=1)
    wr = jnp.zeros((d, LANES), F32).at[:, :N_GROUPS].set(w_rg).at[:, N_GROUPS:N_GROUPS + N_EXPERTS].set(w_re)
    br = jnp.zeros((1, LANES), F32).at[0, :N_GROUPS].set(b_rg).at[0, N_GROUPS:N_GROUPS + N_EXPERTS].set(b_re)
    wrh = wr.astype(BF16)
    wrl = (wr - wrh.astype(F32)).astype(BF16)
    x1, u2, route, route_t, cnt = _merge(z, yb, x, w_s.astype(BF16), bs_full, row(sgu_g), row(sgu_b),
                                         p_a.astype(BF16), p_b.astype(BF16), w_o.astype(BF16),
                                         g1, row(ln1_g), row(ln1_b), sc2, sh2, wrh, wrl, br, tm=min(n, 256))

    bm = MOE_BLOCK
    counts = cnt[0, :N_EXPERTS].astype(jnp.int32)
    nblk_e = (counts + bm - 1) // bm
    blk_end = jnp.cumsum(nblk_e)
    row_start = (blk_end - nblk_e) * bm
    eid = jnp.arange(N_EXPERTS, dtype=F32)[:, None]

    def dest_rows(e, rank):
        start = jnp.sum(jnp.where(e[None, :] == eid, row_start[:, None], 0), axis=0)
        return start + rank.astype(jnp.int32)

    dest1 = dest_rows(route_t[0], route_t[4])
    dest2 = dest_rows(route_t[1], route_t[5])
    max_blocks = (2 * n + N_EXPERTS * (bm - 1)) // bm
    tables = _moe_tables(counts, nblk_e, blk_end, dest1, dest2, n, max_blocks)
    y2 = _moe(tables, u2, w1, w3, w2, max_blocks, 2 * n + 2 * bm)
    return _combine(x1, route, y2, g2, row(ln2_g), row(ln2_b), tm=min(n, 512))


def kernel(x, c, ctx, c_ctx, w_ada, b_ada, w_in, b_in, w_s, b_s, sgu_g, sgu_b, conv_w, conv_b, w_q, w_k, mh_g, skip, p_a, p_b, w_o, ln1_g, ln1_b, w_rg, b_rg, w_re, b_re, w1, w3, w2, ln2_g, ln2_b):
    assert x.shape[0] == 1 and w_ada.shape[0] == DEPTH == 1
    out = _layer(x[0], ctx[0], c, c_ctx, w_ada[0], b_ada[0], w_in[0], b_in[0], w_s[0], b_s[0],
                 sgu_g[0], sgu_b[0], conv_w[0], conv_b[0], w_q[0], w_k[0], mh_g[0], skip[0],
                 p_a[0], p_b[0], w_o[0], ln1_g[0], ln1_b[0], w_rg[0], b_rg[0], w_re[0], b_re[0],
                 w1[0], w3[0], w2[0], ln2_g[0], ln2_b[0])
    return out[None]
```

```python
import functools

import jax
import jax.numpy as jnp
from jax import lax
from jax.experimental import pallas as pl
from jax.experimental.pallas import tpu as pltpu

F32 = jnp.float32
BF16 = jnp.bfloat16

CHUNK = 128
N_HEADS = 8
HEAD_DIM = 128
N_GROUPS = 4
EXP_PER_GROUP = 8
N_EXPERTS = N_GROUPS * EXP_PER_GROUP
LN_EPS = 1e-5
NEG = -1e30
DEPTH = 1
ALPHA = (2 * DEPTH) ** 0.25
LANES = 128
VMEM_LIMIT = 56 * 1024 * 1024

MOE_BLOCK = 256
MOE_WGROUPS = 4


def _cparams(sem):
    return pltpu.CompilerParams(dimension_semantics=sem, vmem_limit_bytes=VMEM_LIMIT)


def _ln(x):
    mu = jnp.mean(x, axis=-1, keepdims=True)
    xc = x - mu
    var = jnp.mean(xc * xc, axis=-1, keepdims=True)
    return xc * lax.rsqrt(var + LN_EPS)


def _sigmoid(x):
    return 1.0 / (1.0 + jnp.exp(-x))


def _silu(x):
    return x * _sigmoid(x)


def _gelu_tanh(x):
    return 0.5 * x * (1.0 + jnp.tanh(0.7978845608028654 * (x + 0.044715 * (x * x * x))))


def _log_sigmoid(x):
    return jnp.minimum(x, 0.0) - jnp.log(1.0 + jnp.exp(-jnp.abs(x)))


def _dot(a, b):
    return jnp.dot(a, b, preferred_element_type=F32)


def _dot_nt(a, b):
    return lax.dot_general(a, b, (((1,), (1,)), ((), ())), preferred_element_type=F32)


def _split2(a):
    hi = a.astype(BF16)
    lo = (a - hi.astype(F32)).astype(BF16)
    return hi, lo


def _split3(a):
    hi = a.astype(BF16)
    r = a - hi.astype(F32)
    mid = r.astype(BF16)
    lo = (r - mid.astype(F32)).astype(BF16)
    return hi, mid, lo


SLAB = 16


def _slab_store(ref, lead, x):
    rows, d = x.shape
    per = d // SLAB // LANES
    for c in range(d // LANES):
        ref[lead + (pl.ds(c // per, rows, stride=SLAB), slice((c % per) * LANES, (c % per + 1) * LANES))] = (
            x[:, c * LANES:(c + 1) * LANES])


def _slab_load(ref, lead, rows, d):
    per = d // SLAB // LANES
    return jnp.concatenate(
        [ref[lead + (pl.ds(c // per, rows, stride=SLAB), slice((c % per) * LANES, (c % per + 1) * LANES))]
         for c in range(d // LANES)], axis=1)


def _ada_kernel(c_ref, w_ref, b_ref, o_ref):
    s_hi, s_lo = _split2(_silu(c_ref[...]))
    w_hi, w_lo = _split2(w_ref[...])
    o_ref[...] = _dot(s_hi, w_hi) + _dot(s_lo, w_hi) + _dot(s_hi, w_lo) + b_ref[...]


def _ada(cc, w, b):
    d, n = w.shape
    tn = 1024
    return pl.pallas_call(
        _ada_kernel,
        grid=(n // tn,),
        in_specs=[pl.BlockSpec((8, d), lambda j: (0, 0)),
                  pl.BlockSpec((d, tn), lambda j: (0, j)),
                  pl.BlockSpec((1, tn), lambda j: (0, j))],
        out_specs=pl.BlockSpec((8, tn), lambda j: (0, j)),
        out_shape=jax.ShapeDtypeStruct((8, n), F32),
        compiler_params=_cparams(("arbitrary",)),
        name="ada",
    )(cc, w, b)


def _inproj_kernel(x_ref, sc_ref, sh_ref, w_ref, b_ref, wgh_ref, wgl_ref, bg_ref,
                   z_ref, g_ref, u_scr):
    @pl.when(pl.program_id(1) == 0)
    def _():
        u = _ln(x_ref[...]) * (1.0 + sc_ref[...]) + sh_ref[...]
        u_hi, u_lo = _split2(u)
        u_scr[...] = u_hi
        wgh = wgh_ref[...]
        g = _dot(u_hi, wgh) + _dot(u_lo, wgh) + _dot(u_hi, wgl_ref[...]) + bg_ref[...]
        g_ref[...] = g.T[:g_ref.shape[0], :]

    z_ref[...] = (_dot(u_scr[...], w_ref[...]) + b_ref[...]).astype(z_ref.dtype)


def _in_proj(x, sc, sh, w, b, wgh, wgl, bg, tm):
    n, d = x.shape
    tn = 1024
    nz = w.shape[1] // tn * tn
    ng = wgh.shape[1]
    return pl.pallas_call(
        _inproj_kernel,
        grid=(n // tm, nz // tn),
        in_specs=[pl.BlockSpec((tm, d), lambda i, j: (i, 0)),
                  pl.BlockSpec((1, d), lambda i, j: (0, 0)),
                  pl.BlockSpec((1, d), lambda i, j: (0, 0)),
                  pl.BlockSpec((d, tn), lambda i, j: (0, j)),
                  pl.BlockSpec((1, tn), lambda i, j: (0, j)),
                  pl.BlockSpec((d, ng), lambda i, j: (0, 0)),
                  pl.BlockSpec((d, ng), lambda i, j: (0, 0)),
                  pl.BlockSpec((1, ng), lambda i, j: (0, 0))],
        out_specs=[pl.BlockSpec((tm, tn), lambda i, j: (i, j)),
                   pl.BlockSpec((4 * N_HEADS, tm), lambda i, j: (0, i))],
        out_shape=[jax.ShapeDtypeStruct((n, nz), BF16),
                   jax.ShapeDtypeStruct((4 * N_HEADS, n), F32)],
        scratch_shapes=[pltpu.VMEM((tm, d), BF16)],
        compiler_params=_cparams(("arbitrary", "arbitrary")),
        name="in_proj",
    )(x, sc, sh, w, b, wgh, wgl, bg)


def _convqk_kernel(xm_ref, prev_ref, next_ref, cw_ref, cb_ref, wq_ref, wk_ref,
                   xc_ref, q_ref, k_ref):
    i = pl.program_id(0)
    last = pl.num_programs(0) - 1
    tm = xm_ref.shape[0]
    xm = xm_ref[...].astype(F32)
    prev_row = jnp.where(i == 0, 0.0, prev_ref[...].astype(F32)[-1:, :])
    next_row = jnp.where(i == last, 0.0, next_ref[...].astype(F32)[:1, :])
    row = lax.broadcasted_iota(jnp.int32, xm.shape, 0)
    x_prev = jnp.where(row == 0, prev_row, pltpu.roll(xm, 1, 0))
    x_next = jnp.where(row == tm - 1, next_row, pltpu.roll(xm, tm - 1, 0))
    cw = cw_ref[...]
    y = cw[0:1, :] * x_prev + cw[1:2, :] * xm + cw[2:3, :] * x_next + cb_ref[...]
    xc = _silu(y).astype(BF16)
    xc_ref[...] = xc
    for h in range(N_HEADS):
        sl = slice(h * HEAD_DIM, (h + 1) * HEAD_DIM)
        q_ref[:, sl] = _dot(xc[:, sl], wq_ref[h]).astype(BF16)
        k_ref[:, sl] = (_dot(xc[:, sl], wk_ref[h]) * (HEAD_DIM ** -0.5)).astype(BF16)


def _conv_qk(z, xm_blk, cw, cb, wq, wk, tm):
    n = z.shape[0]
    w = N_HEADS * HEAD_DIM
    halo = 16
    nb = n // halo
    per = tm // halo
    out = jax.ShapeDtypeStruct((n, w), BF16)
    return pl.pallas_call(
        _convqk_kernel,
        grid=(n // tm,),
        in_specs=[pl.BlockSpec((tm, w), lambda i: (i, xm_blk)),
                  pl.BlockSpec((halo, w), lambda i: (jnp.maximum(i * per - 1, 0), xm_blk)),
                  pl.BlockSpec((halo, w), lambda i: (jnp.minimum((i + 1) * per, nb - 1), xm_blk)),
                  pl.BlockSpec((3, w), lambda i: (0, 0)),
                  pl.BlockSpec((1, w), lambda i: (0, 0)),
                  pl.BlockSpec((N_HEADS, HEAD_DIM, HEAD_DIM), lambda i: (0, 0, 0)),
                  pl.BlockSpec((N_HEADS, HEAD_DIM, HEAD_DIM), lambda i: (0, 0, 0))],
        out_specs=[pl.BlockSpec((tm, w), lambda i: (i, 0))] * 3,
        out_shape=[out, out, out],
        compiler_params=_cparams(("arbitrary",)),
        name="conv_qk",
    )(z, z, z, cw, cb, wq, wk)


def _dot3_right(a, t_bf16):
    hi, mid, lo = _split3(a)
    return _dot(hi, t_bf16) + _dot(mid, t_bf16) + _dot(lo, t_bf16)


def _cummax_lanes(x, reverse):
    n = x.shape[1]
    lane = lax.broadcasted_iota(jnp.int32, x.shape, 1)
    s = 1
    while s < n:
        if reverse:
            x = jnp.maximum(x, jnp.where(lane < n - s, pltpu.roll(x, n - s, 1), NEG))
        else:
            x = jnp.maximum(x, jnp.where(lane >= s, pltpu.roll(x, s, 1), NEG))
        s *= 2
    return x


def _mlstm_kernel(reverse, mode, *refs):
    q_ref, k_ref, v_ref, gt_ref, c0_ref, n0_ref, m0_ref = refs[:7]
    rest = refs[7:]
    if mode == "state":
        c_out, n_out, m_out, c_scr, n_scr, m_scr = rest
    elif mode == "h":
        h_out, c_scr, n_scr, m_scr = rest
    else:
        hb_ref, xc_ref, ob_ref, mhg_ref, skip_ref, y_out, c_scr, n_scr, m_scr = rest

    @pl.when(pl.program_id(0) == 0)
    def _():
        c_scr[...] = c0_ref[...]
        n_scr[...] = n0_ref[...]
        m_scr[...] = m0_ref[...]

    L, H = CHUNK, N_HEADS
    ri = lax.broadcasted_iota(jnp.int32, (L, L), 0)
    ci = lax.broadcasted_iota(jnp.int32, (L, L), 1)
    seen_t = (ri >= ci) if reverse else (ri <= ci)
    row0 = H if reverse else 0
    last = 0 if reverse else L - 1

    gt = gt_ref[...]
    li = gt[row0:row0 + H, :]
    lf = _log_sigmoid(gt[2 * H + row0:3 * H + row0, :])
    b = _dot3_right(lf, seen_t.astype(BF16))
    r = li - b
    big_r = _cummax_lanes(r, reverse)
    m = m_scr[...]
    r_last = big_r[:, last:last + 1]
    big_m = jnp.maximum(big_r, m)
    a = jnp.exp(m - big_m)
    sc = jnp.exp(big_r - big_m)
    floor = jnp.exp(-(b + big_m))
    m_last = jnp.maximum(r_last, m)
    d1 = jnp.exp(m - m_last)
    d2 = jnp.exp(r_last - m_last)
    wk = jnp.exp(r - r_last)
    m_scr[...] = b[:, last:last + 1] + m_last

    ones8 = jnp.ones((H, L), F32)
    r_terms = jnp.concatenate([t.astype(F32) for t in _split3(r)] + [ones8], axis=0).astype(BF16)
    a_mat = _dot_nt((ri == ci).astype(BF16), r_terms).astype(BF16)
    nr_terms = [t.astype(F32) for t in _split3(-big_r)]
    sub = lax.broadcasted_iota(jnp.int32, (H, L), 0)
    ones16 = jnp.ones((16, L), BF16)

    for h in range(H):
        sl = slice(h * HEAD_DIM, (h + 1) * HEAD_DIM)
        row = slice(h, h + 1)
        qh = q_ref[:, sl]
        kh = k_ref[:, sl]
        vt = v_ref[:, sl].T
        sel = (sub == h).astype(F32)
        dyn = jnp.where(sub == 0, nr_terms[0][row], jnp.where(sub == 1, nr_terms[1][row],
                        jnp.where(sub == 2, nr_terms[2][row], 0.0)))
        b_mat = jnp.concatenate([sel, sel, sel, dyn], axis=0).astype(BF16)
        arg = _dot(a_mat, b_mat)
        st = _dot_nt(kh, qh) * jnp.exp(jnp.where(seen_t, arg, NEG))
        c_prev = c_scr[h]
        n_prev = n_scr[row, :]
        if mode != "state":
            n16 = jnp.broadcast_to(n_prev, (16, HEAD_DIM)).astype(BF16)
            intra = _dot(jnp.concatenate([vt, ones16], axis=0), st.astype(BF16))
            inter = _dot_nt(jnp.concatenate([c_prev.astype(BF16), n16], axis=0), qh)
            num = a[row] * inter[:L] + sc[row] * intra[:L]
            den = a[row] * inter[L:L + 1] + sc[row] * intra[L:L + 1]
            ht = num * (1.0 / jnp.maximum(jnp.abs(den), floor[row]))
            if mode == "h":
                h_out[h] = ht.astype(h_out.dtype)
            else:
                hs = ht + hb_ref[h].astype(F32)
                mu = jnp.mean(hs, axis=0, keepdims=True)
                hc = hs - mu
                var = jnp.mean(hc * hc, axis=0, keepdims=True)
                hn = (hc * lax.rsqrt(var + LN_EPS)).T
                y = _sigmoid(ob_ref[:, sl].astype(F32)) * (
                    hn * mhg_ref[:, sl] + skip_ref[:, sl] * xc_ref[:, sl].astype(F32))
                y_out[:, sl] = y.astype(y_out.dtype)
        wk16 = jnp.broadcast_to(wk[row], (16, L)).astype(BF16)
        vtw = (vt.astype(F32) * wk[row]).astype(BF16)
        upd = _dot(jnp.concatenate([vtw, wk16], axis=0), kh)
        c_scr[h] = d1[row] * c_prev + d2[row] * upd[:L]
        n_scr[row, :] = d1[row] * n_prev + d2[row] * upd[L:L + 1]

    if mode == "state":
        c_out[...] = c_scr[...]
        n_out[...] = n_scr[...]
        m_out[...] = m_scr[...]


def _mlstm(mode, reverse, q, k, z, v_blk, gt, state, extra=()):
    n = q.shape[0]
    nc = n // CHUNK
    w = N_HEADS * HEAD_DIM
    c0, n0, m0 = state
    pos = (lambda c: nc - 1 - c) if reverse else (lambda c: c)
    row = lambda blk: pl.BlockSpec((CHUNK, w), lambda c: (pos(c), blk))
    full = lambda a: pl.BlockSpec(a.shape, lambda c: (0,) * a.ndim)
    ht_spec = pl.BlockSpec((None, N_HEADS, HEAD_DIM, CHUNK), lambda c: (pos(c), 0, 0, 0))
    in_specs = [row(0), row(0), row(v_blk),
                pl.BlockSpec((4 * N_HEADS, CHUNK), lambda c: (0, pos(c))),
                full(c0), full(n0), full(m0)]
    args = [q, k, z, gt, c0, n0, m0]
    scratch = [pltpu.VMEM(c0.shape, F32), pltpu.VMEM(n0.shape, F32), pltpu.VMEM(m0.shape, F32)]
    if mode == "state":
        out_specs = [full(c0), full(n0), full(m0)]
        out_shape = [jax.ShapeDtypeStruct(a.shape, F32) for a in state]
    elif mode == "h":
        out_specs = ht_spec
        out_shape = jax.ShapeDtypeStruct((nc, N_HEADS, HEAD_DIM, CHUNK), BF16)
    else:
        out_specs = row(0)
        out_shape = jax.ShapeDtypeStruct((n, w), BF16)
        hb, xc, ob_blk, mhg, skip = extra
        in_specs += [ht_spec, row(0), row(ob_blk), full(mhg), full(skip)]
        args += [hb, xc, z, mhg, skip]
    return pl.pallas_call(
        functools.partial(_mlstm_kernel, reverse, mode),
        grid=(nc,),
        in_specs=in_specs, out_specs=out_specs, out_shape=out_shape,
        scratch_shapes=scratch,
        compiler_params=_cparams(("arbitrary",)),
        name="mlstm_%s_%s" % (mode, "bwd" if reverse else "fwd"),
    )(*args)


def _merge_kernel(ua_ref, va_ref, yb_ref, ga0_ref, ga1_ref, gb0_ref, gb1_ref, x_ref,
                  ws_ref, bs_ref, sg_ref, sb_ref, pa_ref, pb_ref, wo_ref,
                  g1_ref, l1g_ref, l1b_ref, sc2_ref, sh2_ref, wrh_ref, wrl_ref, br_ref,
                  x1_ref, u2_ref, route_ref, route_t_ref, cnt_ref, a_scr, run_scr):
    tm = x_ref.shape[0]

    @pl.when(pl.program_id(0) == 0)
    def _():
        run_scr[...] = jnp.zeros_like(run_scr)

    vn = (_ln(_gelu_tanh(va_ref[...].astype(F32))) * sg_ref[...] + sb_ref[...]).astype(BF16)
    for c in range(tm // CHUNK):
        rows = slice(c * CHUNK, (c + 1) * CHUNK)
        for g in range(N_HEADS):
            cols = slice(g * HEAD_DIM, (g + 1) * HEAD_DIM)
            mixed = _dot(ws_ref[g], vn[rows, cols]) + bs_ref[:, cols]
            a_scr[rows, cols] = (_gelu_tanh(ua_ref[rows, cols].astype(F32)) * mixed).astype(BF16)

    pa = _dot(a_scr[...], pa_ref[...])
    pb = _dot(yb_ref[...], pb_ref[...])
    ga = jnp.concatenate([ga0_ref[...], ga1_ref[...]], axis=1).astype(F32)
    gb = jnp.concatenate([gb0_ref[...], gb1_ref[...]], axis=1).astype(F32)
    mrg = (_sigmoid(ga) * pa + _sigmoid(gb) * pb).astype(BF16)
    y = _dot(mrg, wo_ref[...])
    x1 = _ln(ALPHA * x_ref[...] + g1_ref[...] * y) * l1g_ref[...] + l1b_ref[...]
    x1_ref[...] = x1
    u2 = _ln(x1) * (1.0 + sc2_ref[...]) + sh2_ref[...]
    _slab_store(u2_ref, (), u2)

    u_hi, u_lo = _split2(u2)
    wrh = wrh_ref[...]
    logit = _dot(u_hi, wrh) + _dot(u_lo, wrh) + _dot(u_hi, wrl_ref[...]) + br_ref[...]
    lane = lax.broadcasted_iota(jnp.int32, logit.shape, 1)
    lane_f = lane.astype(F32)
    is_g = lane < N_GROUPS
    gmax = jnp.max(jnp.where(is_g, logit, NEG), axis=-1, keepdims=True)
    g_sel = jnp.min(jnp.where(is_g & (logit == gmax), lane_f, 1e9), axis=-1, keepdims=True)
    p_g = 1.0 / jnp.sum(jnp.where(is_g, jnp.exp(logit - gmax), 0.0), axis=-1, keepdims=True)
    lo = N_GROUPS + EXP_PER_GROUP * g_sel
    in_grp = (lane_f >= lo) & (lane_f < lo + EXP_PER_GROUP)
    el = jnp.where(in_grp, logit, NEG)
    e1max = jnp.max(el, axis=-1, keepdims=True)
    l1 = jnp.min(jnp.where(in_grp & (el == e1max), lane_f, 1e9), axis=-1, keepdims=True)
    el2 = jnp.where(lane_f == l1, NEG, el)
    e2max = jnp.max(el2, axis=-1, keepdims=True)
    l2 = jnp.min(jnp.where(in_grp & (el2 == e2max), lane_f, 1e9), axis=-1, keepdims=True)
    zsum = jnp.sum(jnp.where(in_grp, jnp.exp(el - e1max), 0.0), axis=-1, keepdims=True)
    p1 = 1.0 / zsum
    p2 = jnp.exp(e2max - e1max) / zsum
    w1 = p_g * p1 / (p1 + p2)
    w2 = p_g * p2 / (p1 + p2)
    e1 = l1 - N_GROUPS
    e2 = l2 - N_GROUPS

    oh1 = (lane_f == e1).astype(BF16)
    oh2 = (lane_f == e2).astype(BF16)
    ri = lax.broadcasted_iota(jnp.int32, (tm, tm), 0)
    ci = lax.broadcasted_iota(jnp.int32, (tm, tm), 1)
    strict = (ci < ri).astype(BF16)
    run = run_scr[0:1, :]
    cnt1 = jnp.sum(oh1.astype(F32), axis=0, keepdims=True)
    cnt2 = jnp.sum(oh2.astype(F32), axis=0, keepdims=True)
    pre1 = _dot(strict, oh1) + run
    pre2 = _dot(strict, oh2) + run + cnt1
    rank1 = jnp.sum(oh1.astype(F32) * pre1, axis=-1, keepdims=True)
    rank2 = jnp.sum(oh2.astype(F32) * pre2, axis=-1, keepdims=True)
    new_run = run + cnt1 + cnt2
    run_scr[...] = jnp.broadcast_to(new_run, run_scr.shape)
    cnt_ref[...] = jnp.broadcast_to(new_run, cnt_ref.shape)
    route = jnp.where(lane == 0, e1, 0.0)
    route = jnp.where(lane == 1, e2, route)
    route = jnp.where(lane == 2, w1, route)
    route = jnp.where(lane == 3, w2, route)
    route = jnp.where(lane == 4, rank1, route)
    route = jnp.where(lane == 5, rank2, route)
    route_ref[...] = route
    route_t_ref[...] = route.T[:route_t_ref.shape[0], :]


def _merge(z, yb, x, ws, bs, sg, sb, pa, pb, wo, g1, l1g, l1b, sc2, sh2, wrh, wrl, br, tm):
    n, d = x.shape
    w = N_HEADS * HEAD_DIM
    zc = lambda blk: pl.BlockSpec((tm, w), lambda i: (i, blk))
    full = lambda a: pl.BlockSpec(a.shape, lambda i: (0,) * a.ndim)
    consts = [ws, bs, sg, sb, pa, pb, wo, g1, l1g, l1b, sc2, sh2, wrh, wrl, br]
    return pl.pallas_call(
        _merge_kernel,
        grid=(n // tm,),
        in_specs=[zc(0), zc(1), pl.BlockSpec((tm, w), lambda i: (i, 0)),
                  zc(5), zc(6), zc(7), zc(8),
                  pl.BlockSpec((tm, d), lambda i: (i, 0))] + [full(a) for a in consts],
        out_specs=[pl.BlockSpec((tm, d), lambda i: (i, 0)),
                   pl.BlockSpec((tm * SLAB, d // SLAB), lambda i: (i, 0)),
                   pl.BlockSpec((tm, LANES), lambda i: (i, 0)),
                   pl.BlockSpec((8, tm), lambda i: (0, i)),
                   pl.BlockSpec((8, LANES), lambda i: (0, 0))],
        out_shape=[jax.ShapeDtypeStruct((n, d), F32),
                   jax.ShapeDtypeStruct((n * SLAB, d // SLAB), F32),
                   jax.ShapeDtypeStruct((n, LANES), F32),
                   jax.ShapeDtypeStruct((8, n), F32),
                   jax.ShapeDtypeStruct((8, LANES), F32)],
        scratch_shapes=[pltpu.VMEM((tm, w), BF16), pltpu.VMEM((8, LANES), F32)],
        compiler_params=_cparams(("arbitrary",)),
        name="merge",
    )(z, z, yb, z, z, z, z, x, *consts)


def _moe_kernel(be_ref, slot_ref, kidx_ref, nk_ref, nxt_ref, ubase_ref, nvalid_ref, nb_ref, stok_ref, sdst_ref,
                u_hbm, w1_hbm, w3_hbm, w2_hbm, y_hbm,
                xbuf, obuf, w1b, w3b, w2b, st1, st3, st2, gsem, ssem, wsem, zsem):
    b = pl.program_id(0)
    nb = nb_ref[0]
    bm = xbuf.shape[1] // SLAB
    d = xbuf.shape[2] * SLAB
    kch = st1.shape[0]
    fch = st2.shape[0]
    nf = w2b.shape[1] // fch
    pad_rows = y_hbm.shape[0] - 2 * bm * SLAB

    def gather_start(blk, par):
        base = ubase_ref[blk]
        for t in range(bm):
            row = pl.multiple_of(stok_ref[base + t], SLAB)
            pltpu.make_async_copy(u_hbm.at[pl.ds(row, SLAB), :],
                                  xbuf.at[par, pl.ds(t * SLAB, SLAB), :], gsem.at[par]).start()

    def gather_wait(par):
        pltpu.make_async_copy(u_hbm.at[pl.ds(0, bm * SLAB), :], xbuf.at[par], gsem.at[par]).wait()

    def scatter_start(blk, par):
        base = ubase_ref[blk]
        nvalid = nvalid_ref[blk]
        pad = pad_rows + par * (bm * SLAB)
        for t in range(bm):
            row = pl.multiple_of(jnp.where(t < nvalid, sdst_ref[base + t], pad + t * SLAB), SLAB)
            pltpu.make_async_copy(obuf.at[par, pl.ds(t * SLAB, SLAB), :],
                                  y_hbm.at[pl.ds(row, SLAB), :], ssem.at[par]).start(priority=t % 2)

    def scatter_wait(par):
        pltpu.make_async_copy(obuf.at[par], y_hbm.at[pl.ds(0, bm * SLAB), :], ssem.at[par]).wait()

    def w_rows(f):
        hint = (lambda v, m: v) if isinstance(f, int) else pl.multiple_of
        return (pl.ds(hint(f * kch, kch), kch), pl.ds(hint(f * fch, fch), fch))

    def w_copies(e, f):
        r13, r2 = w_rows(f)
        return (pltpu.make_async_copy(w1_hbm.at[e, r13, :], st1, wsem.at[0]),
                pltpu.make_async_copy(w3_hbm.at[e, r13, :], st3, wsem.at[1]),
                pltpu.make_async_copy(w2_hbm.at[e, r2, :], st2, wsem.at[2]))

    def w_start(e, f):
        for cp in w_copies(e, f):
            cp.start(priority=1)

    def w_finish(e, f, s):
        r13, r2 = w_rows(f)
        for cp in w_copies(e, f):
            cp.wait()
        w1b[s, r13, :] = st1[...].astype(BF16)
        w3b[s, r13, :] = st3[...].astype(BF16)
        w2b[s, r2, :] = st2[...].astype(BF16)

    @pl.when(b < nb)
    def _():
        par = lax.rem(b, 2)
        e = be_ref[b]
        s = slot_ref[b]
        k = kidx_ref[b]
        nk = nk_ref[b]
        e_next = nxt_ref[b]

        @pl.when(b == 0)
        def _():
            obuf[0] = jnp.zeros(obuf.shape[1:], obuf.dtype)
            zero = [pltpu.make_async_copy(obuf.at[0], y_hbm.at[pl.ds(pad_rows + i * bm * SLAB, bm * SLAB), :], zsem)
                    for i in range(2)]
            for cp in zero:
                cp.start()
            gather_start(0, 0)
            for f in range(nf):
                w_start(e, f)
                w_finish(e, f, s)
            for cp in zero:
                cp.wait()

        gather_wait(par)

        @pl.when(b >= 2)
        def _():
            scatter_wait(par)

        g_lo = (k * nf) // nk
        n_groups = jnp.where(e_next < N_EXPERTS, ((k + 1) * nf) // nk - g_lo, 0)

        @pl.when(n_groups > 0)
        def _():
            w_start(e_next, g_lo)

        gather_start(jnp.minimum(b + 1, nb - 1), 1 - par)
        x = _slab_load(xbuf, (par,), bm, d).astype(BF16)
        hidden = (_silu(_dot(x, w1b[s])) * _dot(x, w3b[s])).astype(BF16)
        _slab_store(obuf, (par,), _dot(hidden, w2b[s]))
        scatter_start(b, par)

        @pl.when(n_groups > 0)
        def _():
            w_finish(e_next, g_lo, 1 - s)

        def more(f, carry):
            w_start(e_next, f)
            w_finish(e_next, f, 1 - s)
            return carry

        lax.fori_loop(g_lo + 1, g_lo + n_groups, more, 0)

        @pl.when(b == nb - 1)
        def _():
            gather_wait(1 - par)
            scatter_wait(par)

            @pl.when(nb >= 2)
            def _():
                scatter_wait(1 - par)


def _moe(tables, u2, w1, w3, w2, max_blocks, n_out_rows):
    d = w1.shape[1]
    de = w1.shape[2]
    bm = MOE_BLOCK
    nf = MOE_WGROUPS
    any_spec = pl.BlockSpec(memory_space=pl.ANY)
    return pl.pallas_call(
        _moe_kernel,
        grid_spec=pltpu.PrefetchScalarGridSpec(
            num_scalar_prefetch=len(tables),
            grid=(max_blocks,),
            in_specs=[any_spec] * 4,
            out_specs=any_spec,
            scratch_shapes=[pltpu.VMEM((2, bm * SLAB, d // SLAB), F32), pltpu.VMEM((2, bm * SLAB, d // SLAB), F32),
                            pltpu.VMEM((2, d, de), BF16), pltpu.VMEM((2, d, de), BF16),
                            pltpu.VMEM((2, de, d), BF16),
                            pltpu.VMEM((d // nf, de), F32), pltpu.VMEM((d // nf, de), F32),
                            pltpu.VMEM((de // nf, d), F32),
                            pltpu.SemaphoreType.DMA((2,)), pltpu.SemaphoreType.DMA((2,)),
                            pltpu.SemaphoreType.DMA((3,)), pltpu.SemaphoreType.DMA(())]),
        out_shape=jax.ShapeDtypeStruct((n_out_rows * SLAB, d // SLAB), F32),
        compiler_params=_cparams(("arbitrary",)),
        name="moe",
    )(*tables, u2, w1, w3, w2)


def _combine_kernel(x1_ref, route_ref, y0_ref, y1_ref, g2_ref, lg_ref, lb_ref, o_ref):
    route = route_ref[...]
    tm, d = x1_ref.shape
    f = route[:, 2:3] * _slab_load(y0_ref, (), tm, d) + route[:, 3:4] * _slab_load(y1_ref, (), tm, d)
    o_ref[...] = _ln(ALPHA * x1_ref[...] + g2_ref[...] * f) * lg_ref[...] + lb_ref[...]


def _combine(x1, route, y2, g2, lg, lb, tm):
    n, d = x1.shape
    vec = pl.BlockSpec((1, d), lambda i: (0, 0))
    nt = n // tm
    return pl.pallas_call(
        _combine_kernel,
        grid=(nt,),
        in_specs=[pl.BlockSpec((tm, d), lambda i: (i, 0)),
                  pl.BlockSpec((tm, LANES), lambda i: (i, 0)),
                  pl.BlockSpec((tm * SLAB, d // SLAB), lambda i: (i, 0)),
                  pl.BlockSpec((tm * SLAB, d // SLAB), lambda i: (i + nt, 0)),
                  vec, vec, vec],
        out_specs=pl.BlockSpec((tm, d), lambda i: (i, 0)),
        out_shape=jax.ShapeDtypeStruct((n, d), F32),
        compiler_params=_cparams(("arbitrary",)),
        name="combine",
    )(x1, route, y2, y2, g2, lg, lb)


def _moe_tables(counts, nblk_e, blk_end, dest1, dest2, n, max_blocks):
    bm = MOE_BLOCK
    i32 = jnp.int32
    blk = jnp.arange(max_blocks)
    be = jnp.minimum(jnp.searchsorted(blk_end, blk, side='right'), N_EXPERTS - 1)
    kidx = blk - (blk_end - nblk_e)[be]
    nk = jnp.maximum(nblk_e[be], 1)
    live = nblk_e > 0
    slot = ((jnp.cumsum(live) - 1)[be]) % 2
    first_live_from = jnp.flip(lax.cummin(jnp.flip(jnp.where(live, jnp.arange(N_EXPERTS), N_EXPERTS))))
    nxt = jnp.concatenate([first_live_from[1:], jnp.full((1,), N_EXPERTS)])[be]
    ubase = (jnp.cumsum(counts) - counts)[be] + kidx * bm
    nvalid = jnp.clip(counts[be] - kidx * bm, 0, bm)
    nb = blk_end[-1:]
    j = jnp.arange(2 * n, dtype=i32)
    _, stok, sdst = lax.sort((jnp.concatenate([dest1, dest2]), (j % n) * SLAB, j * SLAB), num_keys=1)
    tail = jnp.zeros((bm,), i32)
    stok = jnp.concatenate([stok, tail])
    sdst = jnp.concatenate([sdst, tail])
    return [t.astype(i32) for t in (be, slot, kidx, nk, nxt, ubase, nvalid, nb, stok, sdst)]


def _layer(x, ctx, c, c_ctx, w_ada, b_ada, w_in, b_in, w_s, b_s, sgu_g, sgu_b, conv_w, conv_b,
           w_q, w_k, mh_g, skip, p_a, p_b, w_o, ln1_g, ln1_b, w_rg, b_rg, w_re, b_re,
           w1, w3, w2, ln2_g, ln2_b):
    n, d = x.shape
    w = N_HEADS * HEAD_DIM
    nz = 9 * w
    H = N_HEADS
    row = lambda a: a.reshape(1, -1)

    cc = jnp.zeros((8, d), F32).at[0].set(c[0]).at[1].set(c_ctx)
    mod = _ada(cc, w_ada, row(b_ada))
    sh1, sc1, g1, sh2, sc2, g2 = [mod[0:1, i * d:(i + 1) * d] for i in range(6)]
    sh1c, sc1c = mod[1:2, 0:d], mod[1:2, d:2 * d]

    def gate_lanes(t):
        cols = [t[:, nz + i * H:nz + (i + 1) * H] for i in (0, 2, 1, 3)]
        return jnp.concatenate(cols + [jnp.zeros((t.shape[0], LANES - 4 * H), F32)], axis=1)

    wg2 = gate_lanes(w_in)
    bg2 = gate_lanes(row(b_in))
    wgh = wg2.astype(BF16)
    wgl = (wg2 - wgh.astype(F32)).astype(BF16)
    w_main = w_in.astype(BF16)
    b_main = row(b_in)

    z, gt = _in_proj(x, sc1, sh1, w_main, b_main, wgh, wgl, bg2, tm=min(n, 1024))
    zc, gct = _in_proj(ctx, sc1c, sh1c, w_main, b_main, wgh, wgl, bg2, tm=ctx.shape[0])

    cw, cb = conv_w, row(conv_b)
    wq, wk = w_q.astype(BF16), w_k.astype(BF16)
    xc, q, k = _conv_qk(z, 2, cw, cb, wq, wk, tm=min(n, 512))
    _, qc, kc = _conv_qk(zc, 2, cw, cb, wq, wk, tm=ctx.shape[0])

    zero = (jnp.zeros((H, HEAD_DIM, HEAD_DIM), F32), jnp.zeros((H, HEAD_DIM), F32),
            jnp.full((H, LANES), NEG, F32))
    st_f = _mlstm("state", False, qc, kc, zc, 3, gct, zero)
    st_b = _mlstm("state", True, qc, kc, zc, 3, gct, zero)
    hb = _mlstm("h", True, q, k, z, 3, gt, st_b)
    yb = _mlstm("out", False, q, k, z, 3, gt, st_f, extra=(hb, xc, 4, row(mh_g), row(skip)))

    bs_full = jnp.repeat(b_s.T, HEAD_DIM, axis=1)
    wr = jnp.zeros((d, LANES), F32).at[:, :N_GROUPS].set(w_rg).at[:, N_GROUPS:N_GROUPS + N_EXPERTS].set(w_re)
    br = jnp.zeros((1, LANES), F32).at[0, :N_GROUPS].set(b_rg).at[0, N_GROUPS:N_GROUPS + N_EXPERTS].set(b_re)
    wrh = wr.astype(BF16)
    wrl = (wr - wrh.astype(F32)).astype(BF16)
    x1, u2, route, route_t, cnt = _merge(z, yb, x, w_s.astype(BF16), bs_full, row(sgu_g), row(sgu_b),
                                         p_a.astype(BF16), p_b.astype(BF16), w_o.astype(BF16),
                                         g1, row(ln1_g), row(ln1_b), sc2, sh2, wrh, wrl, br, tm=min(n, 256))

    bm = MOE_BLOCK
    counts = cnt[0, :N_EXPERTS].astype(jnp.int32)
    nblk_e = (counts + bm - 1) // bm
    blk_end = jnp.cumsum(nblk_e)
    row_start = (blk_end - nblk_e) * bm
    eid = jnp.arange(N_EXPERTS, dtype=F32)[:, None]

    def dest_rows(e, rank):
        start = jnp.sum(jnp.where(e[None, :] == eid, row_start[:, None], 0), axis=0)
        return start + rank.astype(jnp.int32)

    dest1 = dest_rows(route_t[0], route_t[4])
    dest2 = dest_rows(route_t[1], route_t[5])
    max_blocks = (2 * n + N_EXPERTS * (bm - 1)) // bm
    tables = _moe_tables(counts, nblk_e, blk_end, dest1, dest2, n, max_blocks)
    y2 = _moe(tables, u2, w1, w3, w2, max_blocks, 2 * n + 2 * bm)
    return _combine(x1, route, y2, g2, row(ln2_g), row(ln2_b), tm=min(n, 512))


def kernel(x, c, ctx, c_ctx, w_ada, b_ada, w_in, b_in, w_s, b_s, sgu_g, sgu_b, conv_w, conv_b, w_q, w_k, mh_g, skip, p_a, p_b, w_o, ln1_g, ln1_b, w_rg, b_rg, w_re, b_re, w1, w3, w2, ln2_g, ln2_b):
    assert x.shape[0] == 1 and w_ada.shape[0] == DEPTH == 1
    out = _layer(x[0], ctx[0], c, c_ctx, w_ada[0], b_ada[0], w_in[0], b_in[0], w_s[0], b_s[0],
                 sgu_g[0], sgu_b[0], conv_w[0], conv_b[0], w_q[0], w_k[0], mh_g[0], skip[0],
                 p_a[0], p_b[0], w_o[0], ln1_g[0], ln1_b[0], w_rg[0], b_rg[0], w_re[0], b_re[0],
                 w1[0], w3[0], w2[0], ln2_g[0], ln2_b[0])
    return out[None]
```

```python
import functools

import jax
import jax.numpy as jnp
from jax import lax
from jax.experimental import pallas as pl
from jax.experimental.pallas import tpu as pltpu

F32 = jnp.float32
BF16 = jnp.bfloat16

CHUNK = 128
N_HEADS = 8
HEAD_DIM = 128
N_GROUPS = 4
EXP_PER_GROUP = 8
N_EXPERTS = N_GROUPS * EXP_PER_GROUP
LN_EPS = 1e-5
NEG = -1e30
DEPTH = 1
ALPHA = (2 * DEPTH) ** 0.25
LANES = 128
VMEM_LIMIT = 56 * 1024 * 1024

MERGE_SUB = 128
MOE_BLOCK = 256
MOE_WGROUPS = 4


def _cparams(sem):
    return pltpu.CompilerParams(dimension_semantics=sem, vmem_limit_bytes=VMEM_LIMIT)


def _ln(x):
    mu = jnp.mean(x, axis=-1, keepdims=True)
    xc = x - mu
    var = jnp.mean(xc * xc, axis=-1, keepdims=True)
    return xc * lax.rsqrt(var + LN_EPS)


def _sigmoid(x):
    return 1.0 / (1.0 + jnp.exp(-x))


def _silu(x):
    return x * _sigmoid(x)


def _gelu_tanh(x):
    return 0.5 * x * (1.0 + jnp.tanh(0.7978845608028654 * (x + 0.044715 * (x * x * x))))


def _log_sigmoid(x):
    return jnp.minimum(x, 0.0) - jnp.log(1.0 + jnp.exp(-jnp.abs(x)))


def _dot(a, b):
    return jnp.dot(a, b, preferred_element_type=F32)


def _dot_nt(a, b):
    return lax.dot_general(a, b, (((1,), (1,)), ((), ())), preferred_element_type=F32)


def _split2(a):
    hi = a.astype(BF16)
    lo = (a - hi.astype(F32)).astype(BF16)
    return hi, lo


def _split3(a):
    hi = a.astype(BF16)
    r = a - hi.astype(F32)
    mid = r.astype(BF16)
    lo = (r - mid.astype(F32)).astype(BF16)
    return hi, mid, lo


SLAB = 16


def _slab_store(ref, lead, x, r0=0):
    rows, d = x.shape
    per = d // SLAB // LANES
    for c in range(d // LANES):
        ref[lead + (pl.ds(r0 * SLAB + c // per, rows, stride=SLAB),
                    slice((c % per) * LANES, (c % per + 1) * LANES))] = x[:, c * LANES:(c + 1) * LANES]


def _interleave(gens):
    live = list(gens)
    while live:
        for g in list(live):
            try:
                next(g)
            except StopIteration:
                live.remove(g)


def _slab_load(ref, lead, rows, d):
    per = d // SLAB // LANES
    return jnp.concatenate(
        [ref[lead + (pl.ds(c // per, rows, stride=SLAB), slice((c % per) * LANES, (c % per + 1) * LANES))]
         for c in range(d // LANES)], axis=1)


def _ada_kernel(c_ref, w_ref, b_ref, o_ref):
    s_hi, s_lo = _split2(_silu(c_ref[...]))
    w_hi, w_lo = _split2(w_ref[...])
    o_ref[...] = _dot(s_hi, w_hi) + _dot(s_lo, w_hi) + _dot(s_hi, w_lo) + b_ref[...]


def _ada(cc, w, b):
    d, n = w.shape
    tn = 1024
    return pl.pallas_call(
        _ada_kernel,
        grid=(n // tn,),
        in_specs=[pl.BlockSpec((8, d), lambda j: (0, 0)),
                  pl.BlockSpec((d, tn), lambda j: (0, j)),
                  pl.BlockSpec((1, tn), lambda j: (0, j))],
        out_specs=pl.BlockSpec((8, tn), lambda j: (0, j)),
        out_shape=jax.ShapeDtypeStruct((8, n), F32),
        compiler_params=_cparams(("arbitrary",)),
        name="ada",
    )(cc, w, b)


def _inproj_kernel(x_ref, sc_ref, sh_ref, w_ref, b_ref, wgh_ref, wgl_ref, bg_ref,
                   z_ref, g_ref, u_scr):
    @pl.when(pl.program_id(1) == 0)
    def _():
        u = _ln(x_ref[...]) * (1.0 + sc_ref[...]) + sh_ref[...]
        u_hi, u_lo = _split2(u)
        u_scr[...] = u_hi
        wgh = wgh_ref[...]
        g = _dot(u_hi, wgh) + _dot(u_lo, wgh) + _dot(u_hi, wgl_ref[...]) + bg_ref[...]
        g_ref[...] = g.T[:g_ref.shape[0], :]

    z_ref[...] = (_dot(u_scr[...], w_ref[...]) + b_ref[...]).astype(z_ref.dtype)


def _in_proj(x, sc, sh, w, b, wgh, wgl, bg, tm):
    n, d = x.shape
    tn = 1024
    nz = w.shape[1] // tn * tn
    ng = wgh.shape[1]
    return pl.pallas_call(
        _inproj_kernel,
        grid=(n // tm, nz // tn),
        in_specs=[pl.BlockSpec((tm, d), lambda i, j: (i, 0)),
                  pl.BlockSpec((1, d), lambda i, j: (0, 0)),
                  pl.BlockSpec((1, d), lambda i, j: (0, 0)),
                  pl.BlockSpec((d, tn), lambda i, j: (0, j)),
                  pl.BlockSpec((1, tn), lambda i, j: (0, j)),
                  pl.BlockSpec((d, ng), lambda i, j: (0, 0)),
                  pl.BlockSpec((d, ng), lambda i, j: (0, 0)),
                  pl.BlockSpec((1, ng), lambda i, j: (0, 0))],
        out_specs=[pl.BlockSpec((tm, tn), lambda i, j: (i, j)),
                   pl.BlockSpec((4 * N_HEADS, tm), lambda i, j: (0, i))],
        out_shape=[jax.ShapeDtypeStruct((n, nz), BF16),
                   jax.ShapeDtypeStruct((4 * N_HEADS, n), F32)],
        scratch_shapes=[pltpu.VMEM((tm, d), BF16)],
        compiler_params=_cparams(("arbitrary", "arbitrary")),
        name="in_proj",
    )(x, sc, sh, w, b, wgh, wgl, bg)


def _convqk_kernel(xm_ref, prev_ref, next_ref, cw_ref, cb_ref, wq_ref, wk_ref,
                   xc_ref, q_ref, k_ref):
    i = pl.program_id(0)
    last = pl.num_programs(0) - 1
    tm = xm_ref.shape[0]
    xm = xm_ref[...].astype(F32)
    prev_row = jnp.where(i == 0, 0.0, prev_ref[...].astype(F32)[-1:, :])
    next_row = jnp.where(i == last, 0.0, next_ref[...].astype(F32)[:1, :])
    row = lax.broadcasted_iota(jnp.int32, xm.shape, 0)
    x_prev = jnp.where(row == 0, prev_row, pltpu.roll(xm, 1, 0))
    x_next = jnp.where(row == tm - 1, next_row, pltpu.roll(xm, tm - 1, 0))
    cw = cw_ref[...]
    y = cw[0:1, :] * x_prev + cw[1:2, :] * xm + cw[2:3, :] * x_next + cb_ref[...]
    xc = _silu(y).astype(BF16)
    xc_ref[...] = xc
    for h in range(N_HEADS):
        sl = slice(h * HEAD_DIM, (h + 1) * HEAD_DIM)
        q_ref[:, sl] = _dot(xc[:, sl], wq_ref[h]).astype(BF16)
        k_ref[:, sl] = (_dot(xc[:, sl], wk_ref[h]) * (HEAD_DIM ** -0.5)).astype(BF16)


def _conv_qk(z, xm_blk, cw, cb, wq, wk, tm):
    n = z.shape[0]
    w = N_HEADS * HEAD_DIM
    halo = 16
    nb = n // halo
    per = tm // halo
    out = jax.ShapeDtypeStruct((n, w), BF16)
    return pl.pallas_call(
        _convqk_kernel,
        grid=(n // tm,),
        in_specs=[pl.BlockSpec((tm, w), lambda i: (i, xm_blk)),
                  pl.BlockSpec((halo, w), lambda i: (jnp.maximum(i * per - 1, 0), xm_blk)),
                  pl.BlockSpec((halo, w), lambda i: (jnp.minimum((i + 1) * per, nb - 1), xm_blk)),
                  pl.BlockSpec((3, w), lambda i: (0, 0)),
                  pl.BlockSpec((1, w), lambda i: (0, 0)),
                  pl.BlockSpec((N_HEADS, HEAD_DIM, HEAD_DIM), lambda i: (0, 0, 0)),
                  pl.BlockSpec((N_HEADS, HEAD_DIM, HEAD_DIM), lambda i: (0, 0, 0))],
        out_specs=[pl.BlockSpec((tm, w), lambda i: (i, 0))] * 3,
        out_shape=[out, out, out],
        compiler_params=_cparams(("arbitrary",)),
        name="conv_qk",
    )(z, z, z, cw, cb, wq, wk)


def _dot3_right(a, t_bf16):
    hi, mid, lo = _split3(a)
    return _dot(hi, t_bf16) + _dot(mid, t_bf16) + _dot(lo, t_bf16)


def _gate_prep_kernel(gt_ref, rows_ref, amat_ref):
    L, H = CHUNK, N_HEADS
    ri = lax.broadcasted_iota(jnp.int32, (L, L), 0)
    ci = lax.broadcasted_iota(jnp.int32, (L, L), 1)
    eye = (ri == ci).astype(BF16)
    ones8 = jnp.ones((H, L), F32)
    pad = jnp.zeros((LANES - 4 * H, L), F32)
    lane = lax.broadcasted_iota(jnp.int32, (H, L), 1)

    def chunk(d, c):
        reverse = bool(d)
        lanes = slice(c * L, (c + 1) * L)
        tri = ((ri >= ci) if reverse else (ri <= ci)).astype(BF16)
        li = gt_ref[2 * d * H:(2 * d + 1) * H, lanes]
        lf = _log_sigmoid(gt_ref[(2 * d + 1) * H:(2 * d + 2) * H, lanes])
        b = _dot3_right(lf, tri)
        yield
        r = li - b
        terms = jnp.concatenate([t.astype(F32) for t in _split3(r)] + [ones8, pad], axis=0).astype(BF16)
        amat_ref[d, lanes, :] = _dot_nt(eye, terms).astype(BF16)
        big_r = r
        s = 1
        while s < L:
            yield
            if reverse:
                big_r = jnp.maximum(big_r, jnp.where(lane < L - s, pltpu.roll(big_r, L - s, 1), NEG))
            else:
                big_r = jnp.maximum(big_r, jnp.where(lane >= s, pltpu.roll(big_r, s, 1), NEG))
            s *= 2
        rows_ref[d, :, lanes] = jnp.concatenate([b, r, big_r], axis=0)

    _interleave([chunk(d, c) for d in (0, 1) for c in range(gt_ref.shape[1] // L)])


def _gate_prep(gt, tg):
    n = gt.shape[1]
    return pl.pallas_call(
        _gate_prep_kernel,
        grid=(n // tg,),
        in_specs=[pl.BlockSpec((4 * N_HEADS, tg), lambda i: (0, i))],
        out_specs=[pl.BlockSpec((2, 3 * N_HEADS, tg), lambda i: (0, 0, i)),
                   pl.BlockSpec((2, tg, LANES), lambda i: (0, i, 0))],
        out_shape=[jax.ShapeDtypeStruct((2, 3 * N_HEADS, n), F32),
                   jax.ShapeDtypeStruct((2, n, LANES), BF16)],
        compiler_params=_cparams(("arbitrary",)),
        name="gate_prep",
    )(gt)


def _mlstm_kernel(reverse, mode, *refs):
    q_ref, k_ref, v_ref, rows_ref, amat_ref, c0_ref, n0_ref, m0_ref = refs[:8]
    rest = refs[8:]
    if mode == "state":
        c_out, n_out, m_out, c_scr, n_scr, m_scr = rest
    elif mode == "h":
        h_out, c_scr, n_scr, m_scr = rest
    else:
        hb_ref, xc_ref, ob_ref, mhg_ref, skip_ref, y_out, c_scr, n_scr, m_scr = rest

    @pl.when(pl.program_id(0) == 0)
    def _():
        c_scr[...] = c0_ref[...]
        n_scr[...] = n0_ref[...]
        m_scr[...] = m0_ref[...]

    L, H = CHUNK, N_HEADS
    ri = lax.broadcasted_iota(jnp.int32, (L, L), 0)
    ci = lax.broadcasted_iota(jnp.int32, (L, L), 1)
    seen_t = (ri >= ci) if reverse else (ri <= ci)
    last = 0 if reverse else L - 1

    b = rows_ref[0:H, :]
    r = rows_ref[H:2 * H, :]
    big_r = rows_ref[2 * H:3 * H, :]
    m = m_scr[...]
    r_last = big_r[:, last:last + 1]
    big_m = jnp.maximum(big_r, m)
    a = jnp.exp(m - big_m)
    sc = jnp.exp(big_r - big_m)
    floor = jnp.exp(-(b + big_m))
    m_last = jnp.maximum(r_last, m)
    d1 = jnp.exp(m - m_last)
    d2 = jnp.exp(r_last - m_last)
    wk = jnp.exp(r - r_last)
    m_scr[...] = b[:, last:last + 1] + m_last

    a_mat = amat_ref[:, 0:4 * H]
    nr_terms = [t.astype(F32) for t in _split3(-big_r)]
    sub = lax.broadcasted_iota(jnp.int32, (H, L), 0)
    ones16 = jnp.ones((16, L), BF16)

    def head(h):
        sl = slice(h * HEAD_DIM, (h + 1) * HEAD_DIM)
        row = slice(h, h + 1)
        qh = q_ref[:, sl]
        kh = k_ref[:, sl]
        vt = v_ref[:, sl].T
        sel = (sub == h).astype(F32)
        dyn = jnp.where(sub == 0, nr_terms[0][row], jnp.where(sub == 1, nr_terms[1][row],
                        jnp.where(sub == 2, nr_terms[2][row], 0.0)))
        b_mat = jnp.concatenate([sel, sel, sel, dyn], axis=0).astype(BF16)
        arg = _dot(a_mat, b_mat)
        kq = _dot_nt(kh, qh)
        yield
        st = (kq * jnp.exp(jnp.where(seen_t, arg, NEG))).astype(BF16)
        c_prev = c_scr[h]
        n_prev = n_scr[row, :]
        wk16 = jnp.broadcast_to(wk[row], (16, L)).astype(BF16)
        vtw = (vt.astype(F32) * wk[row]).astype(BF16)
        upd = _dot(jnp.concatenate([vtw, wk16], axis=0), kh)
        if mode != "state":
            n16 = jnp.broadcast_to(n_prev, (16, HEAD_DIM)).astype(BF16)
            intra = _dot(jnp.concatenate([vt, ones16], axis=0), st)
            inter = _dot_nt(jnp.concatenate([c_prev.astype(BF16), n16], axis=0), qh)
        yield
        c_scr[h] = d1[row] * c_prev + d2[row] * upd[:L]
        n_scr[row, :] = d1[row] * n_prev + d2[row] * upd[L:L + 1]
        if mode != "state":
            num = a[row] * inter[:L] + sc[row] * intra[:L]
            den = a[row] * inter[L:L + 1] + sc[row] * intra[L:L + 1]
            ht = num * (1.0 / jnp.maximum(jnp.abs(den), floor[row]))
            if mode == "h":
                h_out[h] = ht.astype(h_out.dtype)
            else:
                hs = ht + hb_ref[h].astype(F32)
                mu = jnp.mean(hs, axis=0, keepdims=True)
                hc = hs - mu
                var = jnp.mean(hc * hc, axis=0, keepdims=True)
                hn = (hc * lax.rsqrt(var + LN_EPS)).T
                y = _sigmoid(ob_ref[:, sl].astype(F32)) * (
                    hn * mhg_ref[:, sl] + skip_ref[:, sl] * xc_ref[:, sl].astype(F32))
                y_out[:, sl] = y.astype(y_out.dtype)
        yield

    _interleave([head(h) for h in range(H)])

    if mode == "state":
        c_out[...] = c_scr[...]
        n_out[...] = n_scr[...]
        m_out[...] = m_scr[...]


def _mlstm(mode, reverse, q, k, z, v_blk, rows, amat, state, extra=()):
    n = q.shape[0]
    nc = n // CHUNK
    w = N_HEADS * HEAD_DIM
    c0, n0, m0 = state
    d = int(reverse)
    pos = (lambda c: nc - 1 - c) if reverse else (lambda c: c)
    row = lambda blk: pl.BlockSpec((CHUNK, w), lambda c: (pos(c), blk))
    full = lambda a: pl.BlockSpec(a.shape, lambda c: (0,) * a.ndim)
    ht_spec = pl.BlockSpec((None, N_HEADS, HEAD_DIM, CHUNK), lambda c: (pos(c), 0, 0, 0))
    in_specs = [row(0), row(0), row(v_blk),
                pl.BlockSpec((None, 3 * N_HEADS, CHUNK), lambda c: (d, 0, pos(c))),
                pl.BlockSpec((None, CHUNK, LANES), lambda c: (d, pos(c), 0)),
                full(c0), full(n0), full(m0)]
    args = [q, k, z, rows, amat, c0, n0, m0]
    scratch = [pltpu.VMEM(c0.shape, F32), pltpu.VMEM(n0.shape, F32), pltpu.VMEM(m0.shape, F32)]
    if mode == "state":
        out_specs = [full(c0), full(n0), full(m0)]
        out_shape = [jax.ShapeDtypeStruct(a.shape, F32) for a in state]
    elif mode == "h":
        out_specs = ht_spec
        out_shape = jax.ShapeDtypeStruct((nc, N_HEADS, HEAD_DIM, CHUNK), BF16)
    else:
        out_specs = row(0)
        out_shape = jax.ShapeDtypeStruct((n, w), BF16)
        hb, xc, ob_blk, mhg, skip = extra
        in_specs += [ht_spec, row(0), row(ob_blk), full(mhg), full(skip)]
        args += [hb, xc, z, mhg, skip]
    return pl.pallas_call(
        functools.partial(_mlstm_kernel, reverse, mode),
        grid=(nc,),
        in_specs=in_specs, out_specs=out_specs, out_shape=out_shape,
        scratch_shapes=scratch,
        compiler_params=_cparams(("arbitrary",)),
        name="mlstm_%s_%s" % (mode, "bwd" if reverse else "fwd"),
    )(*args)


def _merge_kernel(ua_ref, va_ref, yb_ref, ga0_ref, ga1_ref, gb0_ref, gb1_ref, x_ref,
                  ws_ref, bs_ref, sg_ref, sb_ref, pa_ref, pb_ref, wo_ref,
                  g1_ref, l1g_ref, l1b_ref, sc2_ref, sh2_ref, wr_ref, br_ref,
                  x1_ref, u2_ref, route_ref, route_t_ref, cnt_ref, a_scr, run_scr):
    tm = x_ref.shape[0]
    sub = MERGE_SUB

    @pl.when(pl.program_id(0) == 0)
    def _():
        run_scr[...] = jnp.zeros_like(run_scr)

    run = [run_scr[0:1, :]]

    def rows_of(r0):
        rs = slice(r0, r0 + sub)
        vn = (_ln(_gelu_tanh(va_ref[rs, :].astype(F32))) * sg_ref[...] + sb_ref[...]).astype(BF16)
        for c in range(sub // CHUNK):
            rows = slice(c * CHUNK, (c + 1) * CHUNK)
            dst = slice(r0 + c * CHUNK, r0 + (c + 1) * CHUNK)
            for g in range(N_HEADS):
                cols = slice(g * HEAD_DIM, (g + 1) * HEAD_DIM)
                mixed = _dot(ws_ref[g], vn[rows, cols]) + bs_ref[:, cols]
                a_scr[dst, cols] = (_gelu_tanh(ua_ref[dst, cols].astype(F32)) * mixed).astype(BF16)
        yield
        pa = _dot(a_scr[rs, :], pa_ref[...])
        pb = _dot(yb_ref[rs, :], pb_ref[...])
        yield
        ga = jnp.concatenate([ga0_ref[rs, :], ga1_ref[rs, :]], axis=1).astype(F32)
        gb = jnp.concatenate([gb0_ref[rs, :], gb1_ref[rs, :]], axis=1).astype(F32)
        mrg = (_sigmoid(ga) * pa + _sigmoid(gb) * pb).astype(BF16)
        yield
        y = _dot(mrg, wo_ref[...])
        yield
        x1 = _ln(ALPHA * x_ref[rs, :] + g1_ref[...] * y) * l1g_ref[...] + l1b_ref[...]
        x1_ref[rs, :] = x1
        u2 = _ln(x1) * (1.0 + sc2_ref[...]) + sh2_ref[...]
        _slab_store(u2_ref, (), u2, r0)
        yield
        logit = _dot(u2.astype(BF16), wr_ref[...]) + br_ref[...]
        lane = lax.broadcasted_iota(jnp.int32, logit.shape, 1)
        lane_f = lane.astype(F32)
        is_g = lane < N_GROUPS
        gmax = jnp.max(jnp.where(is_g, logit, NEG), axis=-1, keepdims=True)
        g_sel = jnp.min(jnp.where(is_g & (logit == gmax), lane_f, 1e9), axis=-1, keepdims=True)
        p_g = 1.0 / jnp.sum(jnp.where(is_g, jnp.exp(logit - gmax), 0.0), axis=-1, keepdims=True)
        lo = N_GROUPS + EXP_PER_GROUP * g_sel
        in_grp = (lane_f >= lo) & (lane_f < lo + EXP_PER_GROUP)
        el = jnp.where(in_grp, logit, NEG)
        e1max = jnp.max(el, axis=-1, keepdims=True)
        l1 = jnp.min(jnp.where(in_grp & (el == e1max), lane_f, 1e9), axis=-1, keepdims=True)
        el2 = jnp.where(lane_f == l1, NEG, el)
        e2max = jnp.max(el2, axis=-1, keepdims=True)
        l2 = jnp.min(jnp.where(in_grp & (el2 == e2max), lane_f, 1e9), axis=-1, keepdims=True)
        zsum = jnp.sum(jnp.where(in_grp, jnp.exp(el - e1max), 0.0), axis=-1, keepdims=True)
        p1 = 1.0 / zsum
        p2 = jnp.exp(e2max - e1max) / zsum
        w1 = p_g * p1 / (p1 + p2)
        w2 = p_g * p2 / (p1 + p2)
        e1 = l1 - N_GROUPS
        e2 = l2 - N_GROUPS
        oh1 = (lane_f == e1).astype(BF16)
        oh2 = (lane_f == e2).astype(BF16)
        ri = lax.broadcasted_iota(jnp.int32, (sub, sub), 0)
        ci = lax.broadcasted_iota(jnp.int32, (sub, sub), 1)
        strict = (ci < ri).astype(BF16)
        cnt1 = jnp.sum(oh1.astype(F32), axis=0, keepdims=True)
        cnt2 = jnp.sum(oh2.astype(F32), axis=0, keepdims=True)
        pre1 = _dot(strict, oh1) + run[0]
        pre2 = _dot(strict, oh2) + run[0] + cnt1
        rank1 = jnp.sum(oh1.astype(F32) * pre1, axis=-1, keepdims=True)
        rank2 = jnp.sum(oh2.astype(F32) * pre2, axis=-1, keepdims=True)
        run[0] = run[0] + cnt1 + cnt2
        route = jnp.where(lane == 0, e1, 0.0)
        route = jnp.where(lane == 1, e2, route)
        route = jnp.where(lane == 2, w1, route)
        route = jnp.where(lane == 3, w2, route)
        route = jnp.where(lane == 4, rank1, route)
        route = jnp.where(lane == 5, rank2, route)
        route_ref[rs, :] = route
        route_t_ref[:, rs] = route.T[:route_t_ref.shape[0], :]
        yield

    _interleave([rows_of(r0) for r0 in range(0, tm, sub)])
    run_scr[...] = jnp.broadcast_to(run[0], run_scr.shape)
    cnt_ref[...] = jnp.broadcast_to(run[0], cnt_ref.shape)


def _merge(z, yb, x, ws, bs, sg, sb, pa, pb, wo, g1, l1g, l1b, sc2, sh2, wr, br, tm):
    n, d = x.shape
    w = N_HEADS * HEAD_DIM
    zc = lambda blk: pl.BlockSpec((tm, w), lambda i: (i, blk))
    full = lambda a: pl.BlockSpec(a.shape, lambda i: (0,) * a.ndim)
    consts = [ws, bs, sg, sb, pa, pb, wo, g1, l1g, l1b, sc2, sh2, wr, br]
    return pl.pallas_call(
        _merge_kernel,
        grid=(n // tm,),
        in_specs=[zc(0), zc(1), pl.BlockSpec((tm, w), lambda i: (i, 0)),
                  zc(5), zc(6), zc(7), zc(8),
                  pl.BlockSpec((tm, d), lambda i: (i, 0))] + [full(a) for a in consts],
        out_specs=[pl.BlockSpec((tm, d), lambda i: (i, 0)),
                   pl.BlockSpec((tm * SLAB, d // SLAB), lambda i: (i, 0)),
                   pl.BlockSpec((tm, LANES), lambda i: (i, 0)),
                   pl.BlockSpec((8, tm), lambda i: (0, i)),
                   pl.BlockSpec((8, LANES), lambda i: (0, 0))],
        out_shape=[jax.ShapeDtypeStruct((n, d), F32),
                   jax.ShapeDtypeStruct((n * SLAB, d // SLAB), F32),
                   jax.ShapeDtypeStruct((n, LANES), F32),
                   jax.ShapeDtypeStruct((8, n), F32),
                   jax.ShapeDtypeStruct((8, LANES), F32)],
        scratch_shapes=[pltpu.VMEM((tm, w), BF16), pltpu.VMEM((8, LANES), F32)],
        compiler_params=_cparams(("arbitrary",)),
        name="merge",
    )(z, z, yb, z, z, z, z, x, *consts)


def _moe_kernel(be_ref, slot_ref, kidx_ref, nk_ref, nxt_ref, ubase_ref, nvalid_ref, nb_ref, stok_ref, sdst_ref,
                u_hbm, w1_hbm, w3_hbm, w2_hbm, y_hbm,
                xbuf, obuf, w1b, w3b, w2b, st1, st3, st2, gsem, ssem, wsem, zsem):
    b = pl.program_id(0)
    nb = nb_ref[0]
    bm = xbuf.shape[1] // SLAB
    d = xbuf.shape[2] * SLAB
    kch = st1.shape[0]
    fch = st2.shape[0]
    nf = w2b.shape[1] // fch
    pad_rows = y_hbm.shape[0] - 2 * bm * SLAB

    def gather_start(blk, par):
        base = ubase_ref[blk]
        for t in range(bm):
            row = pl.multiple_of(stok_ref[base + t], SLAB)
            pltpu.make_async_copy(u_hbm.at[pl.ds(row, SLAB), :],
                                  xbuf.at[par, pl.ds(t * SLAB, SLAB), :], gsem.at[par]).start()

    def gather_wait(par):
        pltpu.make_async_copy(u_hbm.at[pl.ds(0, bm * SLAB), :], xbuf.at[par], gsem.at[par]).wait()

    def scatter_start(blk, par):
        base = ubase_ref[blk]
        nvalid = nvalid_ref[blk]
        pad = pad_rows + par * (bm * SLAB)
        for t in range(bm):
            row = pl.multiple_of(jnp.where(t < nvalid, sdst_ref[base + t], pad + t * SLAB), SLAB)
            pltpu.make_async_copy(obuf.at[par, pl.ds(t * SLAB, SLAB), :],
                                  y_hbm.at[pl.ds(row, SLAB), :], ssem.at[par]).start(priority=t % 2)

    def scatter_wait(par):
        pltpu.make_async_copy(obuf.at[par], y_hbm.at[pl.ds(0, bm * SLAB), :], ssem.at[par]).wait()

    def w_rows(f):
        hint = (lambda v, m: v) if isinstance(f, int) else pl.multiple_of
        return (pl.ds(hint(f * kch, kch), kch), pl.ds(hint(f * fch, fch), fch))

    def w_copies(e, f):
        r13, r2 = w_rows(f)
        return (pltpu.make_async_copy(w1_hbm.at[e, r13, :], st1, wsem.at[0]),
                pltpu.make_async_copy(w3_hbm.at[e, r13, :], st3, wsem.at[1]),
                pltpu.make_async_copy(w2_hbm.at[e, r2, :], st2, wsem.at[2]))

    def w_start(e, f):
        for cp in w_copies(e, f):
            cp.start(priority=1)

    def w_finish(e, f, s):
        r13, r2 = w_rows(f)
        for cp in w_copies(e, f):
            cp.wait()
        w1b[s, r13, :] = st1[...].astype(BF16)
        w3b[s, r13, :] = st3[...].astype(BF16)
        w2b[s, r2, :] = st2[...].astype(BF16)

    @pl.when(b < nb)
    def _():
        par = lax.rem(b, 2)
        e = be_ref[b]
        s = slot_ref[b]
        k = kidx_ref[b]
        nk = nk_ref[b]
        e_next = nxt_ref[b]

        @pl.when(b == 0)
        def _():
            obuf[0] = jnp.zeros(obuf.shape[1:], obuf.dtype)
            zero = [pltpu.make_async_copy(obuf.at[0], y_hbm.at[pl.ds(pad_rows + i * bm * SLAB, bm * SLAB), :], zsem)
                    for i in range(2)]
            for cp in zero:
                cp.start()
            gather_start(0, 0)
            for f in range(nf):
                w_start(e, f)
                w_finish(e, f, s)
            for cp in zero:
                cp.wait()

        gather_wait(par)

        @pl.when(b >= 2)
        def _():
            scatter_wait(par)

        g_lo = (k * nf) // nk
        n_groups = jnp.where(e_next < N_EXPERTS, ((k + 1) * nf) // nk - g_lo, 0)

        @pl.when(n_groups > 0)
        def _():
            w_start(e_next, g_lo)

        gather_start(jnp.minimum(b + 1, nb - 1), 1 - par)
        x = _slab_load(xbuf, (par,), bm, d).astype(BF16)
        hidden = (_silu(_dot(x, w1b[s])) * _dot(x, w3b[s])).astype(BF16)
        _slab_store(obuf, (par,), _dot(hidden, w2b[s]))
        scatter_start(b, par)

        @pl.when(n_groups > 0)
        def _():
            w_finish(e_next, g_lo, 1 - s)

        def more(f, carry):
            w_start(e_next, f)
            w_finish(e_next, f, 1 - s)
            return carry

        lax.fori_loop(g_lo + 1, g_lo + n_groups, more, 0)

        @pl.when(b == nb - 1)
        def _():
            gather_wait(1 - par)
            scatter_wait(par)

            @pl.when(nb >= 2)
            def _():
                scatter_wait(1 - par)


def _moe(tables, u2, w1, w3, w2, max_blocks, n_out_rows):
    d = w1.shape[1]
    de = w1.shape[2]
    bm = MOE_BLOCK
    nf = MOE_WGROUPS
    any_spec = pl.BlockSpec(memory_space=pl.ANY)
    return pl.pallas_call(
        _moe_kernel,
        grid_spec=pltpu.PrefetchScalarGridSpec(
            num_scalar_prefetch=len(tables),
            grid=(max_blocks,),
            in_specs=[any_spec] * 4,
            out_specs=any_spec,
            scratch_shapes=[pltpu.VMEM((2, bm * SLAB, d // SLAB), F32), pltpu.VMEM((2, bm * SLAB, d // SLAB), F32),
                            pltpu.VMEM((2, d, de), BF16), pltpu.VMEM((2, d, de), BF16),
                            pltpu.VMEM((2, de, d), BF16),
                            pltpu.VMEM((d // nf, de), F32), pltpu.VMEM((d // nf, de), F32),
                            pltpu.VMEM((de // nf, d), F32),
                            pltpu.SemaphoreType.DMA((2,)), pltpu.SemaphoreType.DMA((2,)),
                            pltpu.SemaphoreType.DMA((3,)), pltpu.SemaphoreType.DMA(())]),
        out_shape=jax.ShapeDtypeStruct((n_out_rows * SLAB, d // SLAB), F32),
        compiler_params=_cparams(("arbitrary",)),
        name="moe",
    )(*tables, u2, w1, w3, w2)


def _combine_kernel(x1_ref, route_ref, y0_ref, y1_ref, g2_ref, lg_ref, lb_ref, o_ref):
    route = route_ref[...]
    tm, d = x1_ref.shape
    f = route[:, 2:3] * _slab_load(y0_ref, (), tm, d) + route[:, 3:4] * _slab_load(y1_ref, (), tm, d)
    o_ref[...] = _ln(ALPHA * x1_ref[...] + g2_ref[...] * f) * lg_ref[...] + lb_ref[...]


def _combine(x1, route, y2, g2, lg, lb, tm):
    n, d = x1.shape
    vec = pl.BlockSpec((1, d), lambda i: (0, 0))
    nt = n // tm
    return pl.pallas_call(
        _combine_kernel,
        grid=(nt,),
        in_specs=[pl.BlockSpec((tm, d), lambda i: (i, 0)),
                  pl.BlockSpec((tm, LANES), lambda i: (i, 0)),
                  pl.BlockSpec((tm * SLAB, d // SLAB), lambda i: (i, 0)),
                  pl.BlockSpec((tm * SLAB, d // SLAB), lambda i: (i + nt, 0)),
                  vec, vec, vec],
        out_specs=pl.BlockSpec((tm, d), lambda i: (i, 0)),
        out_shape=jax.ShapeDtypeStruct((n, d), F32),
        compiler_params=_cparams(("arbitrary",)),
        name="combine",
    )(x1, route, y2, y2, g2, lg, lb)


def _moe_tables(counts, nblk_e, blk_end, dest1, dest2, n, max_blocks):
    bm = MOE_BLOCK
    i32 = jnp.int32
    blk = jnp.arange(max_blocks)
    be = jnp.minimum(jnp.searchsorted(blk_end, blk, side='right'), N_EXPERTS - 1)
    kidx = blk - (blk_end - nblk_e)[be]
    nk = jnp.maximum(nblk_e[be], 1)
    live = nblk_e > 0
    slot = ((jnp.cumsum(live) - 1)[be]) % 2
    first_live_from = jnp.flip(lax.cummin(jnp.flip(jnp.where(live, jnp.arange(N_EXPERTS), N_EXPERTS))))
    nxt = jnp.concatenate([first_live_from[1:], jnp.full((1,), N_EXPERTS)])[be]
    ubase = (jnp.cumsum(counts) - counts)[be] + kidx * bm
    nvalid = jnp.clip(counts[be] - kidx * bm, 0, bm)
    nb = blk_end[-1:]
    j = jnp.arange(2 * n, dtype=i32)
    _, stok, sdst = lax.sort((jnp.concatenate([dest1, dest2]), (j % n) * SLAB, j * SLAB), num_keys=1)
    tail = jnp.zeros((bm,), i32)
    stok = jnp.concatenate([stok, tail])
    sdst = jnp.concatenate([sdst, tail])
    return [t.astype(i32) for t in (be, slot, kidx, nk, nxt, ubase, nvalid, nb, stok, sdst)]


def _layer(x, ctx, c, c_ctx, w_ada, b_ada, w_in, b_in, w_s, b_s, sgu_g, sgu_b, conv_w, conv_b,
           w_q, w_k, mh_g, skip, p_a, p_b, w_o, ln1_g, ln1_b, w_rg, b_rg, w_re, b_re,
           w1, w3, w2, ln2_g, ln2_b):
    n, d = x.shape
    w = N_HEADS * HEAD_DIM
    nz = 9 * w
    H = N_HEADS
    row = lambda a: a.reshape(1, -1)

    cc = jnp.zeros((8, d), F32).at[0].set(c[0]).at[1].set(c_ctx)
    mod = _ada(cc, w_ada, row(b_ada))
    sh1, sc1, g1, sh2, sc2, g2 = [mod[0:1, i * d:(i + 1) * d] for i in range(6)]
    sh1c, sc1c = mod[1:2, 0:d], mod[1:2, d:2 * d]

    def gate_lanes(t):
        return jnp.pad(t[:, nz:], ((0, 0), (0, LANES - 4 * H)))

    wg2 = gate_lanes(w_in)
    bg2 = gate_lanes(row(b_in))
    wgh = wg2.astype(BF16)
    wgl = (wg2 - wgh.astype(F32)).astype(BF16)
    w_main = w_in.astype(BF16)
    b_main = row(b_in)

    z, gt = _in_proj(x, sc1, sh1, w_main, b_main, wgh, wgl, bg2, tm=min(n, 1024))
    zc, gct = _in_proj(ctx, sc1c, sh1c, w_main, b_main, wgh, wgl, bg2, tm=ctx.shape[0])

    cw, cb = conv_w, row(conv_b)
    wq, wk = w_q.astype(BF16), w_k.astype(BF16)
    xc, q, k = _conv_qk(z, 2, cw, cb, wq, wk, tm=min(n, 512))
    _, qc, kc = _conv_qk(zc, 2, cw, cb, wq, wk, tm=ctx.shape[0])

    zero = (jnp.zeros((H, HEAD_DIM, HEAD_DIM), F32), jnp.zeros((H, HEAD_DIM), F32),
            jnp.full((H, LANES), NEG, F32))
    rows_c, amat_c = _gate_prep(gct, tg=ctx.shape[0])
    rows_x, amat_x = _gate_prep(gt, tg=min(n, 1024))
    st_f = _mlstm("state", False, qc, kc, zc, 3, rows_c, amat_c, zero)
    st_b = _mlstm("state", True, qc, kc, zc, 3, rows_c, amat_c, zero)
    hb = _mlstm("h", True, q, k, z, 3, rows_x, amat_x, st_b)
    yb = _mlstm("out", False, q, k, z, 3, rows_x, amat_x, st_f, extra=(hb, xc, 4, row(mh_g), row(skip)))

    bs_full = jnp.repeat(b_s.T, HEAD_DIM, axis=1)
    wr = jnp.zeros((d, LANES), F32).at[:, :N_GROUPS].set(w_rg).at[:, N_GROUPS:N_GROUPS + N_EXPERTS].set(w_re)
    br = jnp.zeros((1, LANES), F32).at[0, :N_GROUPS].set(b_rg).at[0, N_GROUPS:N_GROUPS + N_EXPERTS].set(b_re)
    x1, u2, route, route_t, cnt = _merge(z, yb, x, w_s.astype(BF16), bs_full, row(sgu_g), row(sgu_b),
                                         p_a.astype(BF16), p_b.astype(BF16), w_o.astype(BF16),
                                         g1, row(ln1_g), row(ln1_b), sc2, sh2, wr.astype(BF16), br, tm=min(n, 256))

    bm = MOE_BLOCK
    counts = cnt[0, :N_EXPERTS].astype(jnp.int32)
    nblk_e = (counts + bm - 1) // bm
    blk_end = jnp.cumsum(nblk_e)
    row_start = (blk_end - nblk_e) * bm
    eid = jnp.arange(N_EXPERTS, dtype=F32)[:, None]

    def dest_rows(e, rank):
        start = jnp.sum(jnp.where(e[None, :] == eid, row_start[:, None], 0), axis=0)
        return start + rank.astype(jnp.int32)

    dest1 = dest_rows(route_t[0], route_t[4])
    dest2 = dest_rows(route_t[1], route_t[5])
    max_blocks = (2 * n + N_EXPERTS * (bm - 1)) // bm
    tables = _moe_tables(counts, nblk_e, blk_end, dest1, dest2, n, max_blocks)
    y2 = _moe(tables, u2, w1, w3, w2, max_blocks, 2 * n + 2 * bm)
    return _combine(x1, route, y2, g2, row(ln2_g), row(ln2_b), tm=min(n, 512))


def kernel(x, c, ctx, c_ctx, w_ada, b_ada, w_in, b_in, w_s, b_s, sgu_g, sgu_b, conv_w, conv_b, w_q, w_k, mh_g, skip, p_a, p_b, w_o, ln1_g, ln1_b, w_rg, b_rg, w_re, b_re, w1, w3, w2, ln2_g, ln2_b):
    assert x.shape[0] == 1 and w_ada.shape[0] == DEPTH == 1
    out = _layer(x[0], ctx[0], c, c_ctx, w_ada[0], b_ada[0], w_in[0], b_in[0], w_s[0], b_s[0],
                 sgu_g[0], sgu_b[0], conv_w[0], conv_b[0], w_q[0], w_k[0], mh_g[0], skip[0],
                 p_a[0], p_b[0], w_o[0], ln1_g[0], ln1_b[0], w_rg[0], b_rg[0], w_re[0], b_re[0],
                 w1[0], w3[0], w2[0], ln2_g[0], ln2_b[0])
    return out[None]
```

```python
import functools

import jax
import jax.numpy as jnp
from jax import lax
from jax.experimental import pallas as pl
from jax.experimental.pallas import tpu as pltpu

F32 = jnp.float32
BF16 = jnp.bfloat16

CHUNK = 128
N_HEADS = 8
HEAD_DIM = 128
N_GROUPS = 4
EXP_PER_GROUP = 8
N_EXPERTS = N_GROUPS * EXP_PER_GROUP
LN_EPS = 1e-5
NEG = -1e30
DEPTH = 1
ALPHA = (2 * DEPTH) ** 0.25
LANES = 128
VMEM_LIMIT = 56 * 1024 * 1024

MERGE_SUB = 128
MOE_BLOCK = 256
MOE_WGROUPS = 4


def _cparams(sem):
    return pltpu.CompilerParams(dimension_semantics=sem, vmem_limit_bytes=VMEM_LIMIT)


def _ln(x):
    mu = jnp.mean(x, axis=-1, keepdims=True)
    xc = x - mu
    var = jnp.mean(xc * xc, axis=-1, keepdims=True)
    return xc * lax.rsqrt(var + LN_EPS)


def _sigmoid(x):
    return 1.0 / (1.0 + jnp.exp(-x))


def _silu(x):
    return x * _sigmoid(x)


def _gelu_tanh(x):
    return 0.5 * x * (1.0 + jnp.tanh(0.7978845608028654 * (x + 0.044715 * (x * x * x))))


def _log_sigmoid(x):
    return jnp.minimum(x, 0.0) - jnp.log(1.0 + jnp.exp(-jnp.abs(x)))


def _dot(a, b):
    return jnp.dot(a, b, preferred_element_type=F32)


def _dot_nt(a, b):
    return lax.dot_general(a, b, (((1,), (1,)), ((), ())), preferred_element_type=F32)


def _split2(a):
    hi = a.astype(BF16)
    lo = (a - hi.astype(F32)).astype(BF16)
    return hi, lo


def _split3(a):
    hi = a.astype(BF16)
    r = a - hi.astype(F32)
    mid = r.astype(BF16)
    lo = (r - mid.astype(F32)).astype(BF16)
    return hi, mid, lo


SLAB = 16


def _slab_store(ref, lead, x, r0=0):
    rows, d = x.shape
    per = d // SLAB // LANES
    for c in range(d // LANES):
        ref[lead + (pl.ds(r0 * SLAB + c // per, rows, stride=SLAB),
                    slice((c % per) * LANES, (c % per + 1) * LANES))] = x[:, c * LANES:(c + 1) * LANES]


def _interleave(gens):
    live = list(gens)
    while live:
        for g in list(live):
            try:
                next(g)
            except StopIteration:
                live.remove(g)


def _slab_load(ref, lead, rows, d):
    per = d // SLAB // LANES
    return jnp.concatenate(
        [ref[lead + (pl.ds(c // per, rows, stride=SLAB), slice((c % per) * LANES, (c % per + 1) * LANES))]
         for c in range(d // LANES)], axis=1)


def _ada_kernel(c_ref, w_ref, b_ref, o_ref):
    s_hi, s_lo = _split2(_silu(c_ref[...]))
    w_hi, w_lo = _split2(w_ref[...])
    o_ref[...] = _dot(s_hi, w_hi) + _dot(s_lo, w_hi) + _dot(s_hi, w_lo) + b_ref[...]


def _ada(cc, w, b):
    d, n = w.shape
    tn = 1024
    return pl.pallas_call(
        _ada_kernel,
        grid=(n // tn,),
        in_specs=[pl.BlockSpec((8, d), lambda j: (0, 0)),
                  pl.BlockSpec((d, tn), lambda j: (0, j)),
                  pl.BlockSpec((1, tn), lambda j: (0, j))],
        out_specs=pl.BlockSpec((8, tn), lambda j: (0, j)),
        out_shape=jax.ShapeDtypeStruct((8, n), F32),
        compiler_params=_cparams(("arbitrary",)),
        name="ada",
    )(cc, w, b)


def _gate_weights_kernel(w_hbm, o_ref, buf, sem):
    ng = buf.shape[1]
    cp = pltpu.make_async_copy(w_hbm.at[:, pl.ds(w_hbm.shape[1] - ng, ng)], buf, sem)
    cp.start()
    cp.wait()
    wg = jnp.concatenate([buf[...], jnp.zeros((buf.shape[0], LANES - ng), F32)], axis=1)
    hi, lo = _split2(wg)
    o_ref[...] = jnp.concatenate([hi, lo], axis=1)


def _gate_weights(w, ng):
    d = w.shape[0]
    return pl.pallas_call(
        _gate_weights_kernel,
        in_specs=[pl.BlockSpec(memory_space=pl.ANY)],
        out_specs=pl.BlockSpec((d, 2 * LANES), lambda: (0, 0)),
        out_shape=jax.ShapeDtypeStruct((d, 2 * LANES), BF16),
        scratch_shapes=[pltpu.VMEM((d, ng), F32), pltpu.SemaphoreType.DMA(())],
        compiler_params=pltpu.CompilerParams(vmem_limit_bytes=VMEM_LIMIT),
        name="gate_weights",
    )(w)


def _inproj_kernel(x_ref, sc_ref, sh_ref, w_ref, b_ref, wg_ref, bg_ref, z_ref, g_ref, u_scr):
    @pl.when(pl.program_id(1) == 0)
    def _():
        u = _ln(x_ref[...]) * (1.0 + sc_ref[...]) + sh_ref[...]
        u_hi, u_lo = _split2(u)
        u_scr[...] = u_hi
        wg = wg_ref[...]
        p = _dot(u_hi, wg)
        g = p[:, :LANES] + p[:, LANES:] + _dot(u_lo, wg[:, :LANES]) + bg_ref[...]
        g_ref[...] = g.T[:g_ref.shape[0], :]

    z_ref[...] = (_dot(u_scr[...], w_ref[...]) + b_ref[...]).astype(z_ref.dtype)


def _in_proj(x, sc, sh, w, b, wg, bg, tm):
    n, d = x.shape
    tn = 1024
    nz = w.shape[1] // tn * tn
    ng = bg.shape[1]
    return pl.pallas_call(
        _inproj_kernel,
        grid=(n // tm, nz // tn),
        in_specs=[pl.BlockSpec((tm, d), lambda i, j: (i, 0)),
                  pl.BlockSpec((1, d), lambda i, j: (0, 0)),
                  pl.BlockSpec((1, d), lambda i, j: (0, 0)),
                  pl.BlockSpec((d, tn), lambda i, j: (0, j)),
                  pl.BlockSpec((1, tn), lambda i, j: (0, j)),
                  pl.BlockSpec((d, 2 * ng), lambda i, j: (0, 0)),
                  pl.BlockSpec((1, ng), lambda i, j: (0, 0))],
        out_specs=[pl.BlockSpec((tm, tn), lambda i, j: (i, j)),
                   pl.BlockSpec((4 * N_HEADS, tm), lambda i, j: (0, i))],
        out_shape=[jax.ShapeDtypeStruct((n, nz), BF16),
                   jax.ShapeDtypeStruct((4 * N_HEADS, n), F32)],
        scratch_shapes=[pltpu.VMEM((tm, d), BF16)],
        compiler_params=_cparams(("arbitrary", "arbitrary")),
        name="in_proj",
    )(x, sc, sh, w, b, wg, bg)


def _convqk_kernel(xm_ref, prev_ref, next_ref, cw_ref, cb_ref, wq_ref, wk_ref,
                   xc_ref, q_ref, k_ref):
    i = pl.program_id(0)
    last = pl.num_programs(0) - 1
    tm = xm_ref.shape[0]
    xm = xm_ref[...].astype(F32)
    prev_row = jnp.where(i == 0, 0.0, prev_ref[...].astype(F32)[-1:, :])
    next_row = jnp.where(i == last, 0.0, next_ref[...].astype(F32)[:1, :])
    row = lax.broadcasted_iota(jnp.int32, xm.shape, 0)
    x_prev = jnp.where(row == 0, prev_row, pltpu.roll(xm, 1, 0))
    x_next = jnp.where(row == tm - 1, next_row, pltpu.roll(xm, tm - 1, 0))
    cw = cw_ref[...]
    y = cw[0:1, :] * x_prev + cw[1:2, :] * xm + cw[2:3, :] * x_next + cb_ref[...]
    xc = _silu(y).astype(BF16)
    xc_ref[...] = xc
    for h in range(N_HEADS):
        sl = slice(h * HEAD_DIM, (h + 1) * HEAD_DIM)
        q_ref[:, sl] = _dot(xc[:, sl], wq_ref[h]).astype(BF16)
        k_ref[:, sl] = (_dot(xc[:, sl], wk_ref[h]) * (HEAD_DIM ** -0.5)).astype(BF16)


def _conv_qk(z, xm_blk, cw, cb, wq, wk, tm):
    n = z.shape[0]
    w = N_HEADS * HEAD_DIM
    halo = 16
    nb = n // halo
    per = tm // halo
    out = jax.ShapeDtypeStruct((n, w), BF16)
    return pl.pallas_call(
        _convqk_kernel,
        grid=(n // tm,),
        in_specs=[pl.BlockSpec((tm, w), lambda i: (i, xm_blk)),
                  pl.BlockSpec((halo, w), lambda i: (jnp.maximum(i * per - 1, 0), xm_blk)),
                  pl.BlockSpec((halo, w), lambda i: (jnp.minimum((i + 1) * per, nb - 1), xm_blk)),
                  pl.BlockSpec((3, w), lambda i: (0, 0)),
                  pl.BlockSpec((1, w), lambda i: (0, 0)),
                  pl.BlockSpec((N_HEADS, HEAD_DIM, HEAD_DIM), lambda i: (0, 0, 0)),
                  pl.BlockSpec((N_HEADS, HEAD_DIM, HEAD_DIM), lambda i: (0, 0, 0))],
        out_specs=[pl.BlockSpec((tm, w), lambda i: (i, 0))] * 3,
        out_shape=[out, out, out],
        compiler_params=_cparams(("arbitrary",)),
        name="conv_qk",
    )(z, z, z, cw, cb, wq, wk)


def _dot3_right(a, t_bf16):
    hi, mid, lo = _split3(a)
    return _dot(hi, t_bf16) + _dot(mid, t_bf16) + _dot(lo, t_bf16)


def _gate_prep_kernel(gt_ref, rows_ref, amat_ref):
    L, H = CHUNK, N_HEADS
    ri = lax.broadcasted_iota(jnp.int32, (L, L), 0)
    ci = lax.broadcasted_iota(jnp.int32, (L, L), 1)
    eye = (ri == ci).astype(BF16)
    ones8 = jnp.ones((H, L), F32)
    pad = jnp.zeros((LANES - 4 * H, L), F32)
    lane = lax.broadcasted_iota(jnp.int32, (H, L), 1)

    def chunk(d, c):
        reverse = bool(d)
        lanes = slice(c * L, (c + 1) * L)
        tri = ((ri >= ci) if reverse else (ri <= ci)).astype(BF16)
        li = gt_ref[2 * d * H:(2 * d + 1) * H, lanes]
        lf = _log_sigmoid(gt_ref[(2 * d + 1) * H:(2 * d + 2) * H, lanes])
        b = _dot3_right(lf, tri)
        yield
        r = li - b
        terms = jnp.concatenate([t.astype(F32) for t in _split3(r)] + [ones8, pad], axis=0).astype(BF16)
        amat_ref[d, lanes, :] = _dot_nt(eye, terms).astype(BF16)
        big_r = r
        s = 1
        while s < L:
            yield
            if reverse:
                big_r = jnp.maximum(big_r, jnp.where(lane < L - s, pltpu.roll(big_r, L - s, 1), NEG))
            else:
                big_r = jnp.maximum(big_r, jnp.where(lane >= s, pltpu.roll(big_r, s, 1), NEG))
            s *= 2
        rows_ref[d, :, lanes] = jnp.concatenate([b, r, big_r], axis=0)

    _interleave([chunk(d, c) for d in (0, 1) for c in range(gt_ref.shape[1] // L)])


def _gate_prep(gt, tg):
    n = gt.shape[1]
    return pl.pallas_call(
        _gate_prep_kernel,
        grid=(n // tg,),
        in_specs=[pl.BlockSpec((4 * N_HEADS, tg), lambda i: (0, i))],
        out_specs=[pl.BlockSpec((2, 3 * N_HEADS, tg), lambda i: (0, 0, i)),
                   pl.BlockSpec((2, tg, LANES), lambda i: (0, i, 0))],
        out_shape=[jax.ShapeDtypeStruct((2, 3 * N_HEADS, n), F32),
                   jax.ShapeDtypeStruct((2, n, LANES), BF16)],
        compiler_params=_cparams(("arbitrary",)),
        name="gate_prep",
    )(gt)


def _mlstm_kernel(reverse, mode, *refs):
    q_ref, k_ref, v_ref, rows_ref, amat_ref, c0_ref, n0_ref, m0_ref = refs[:8]
    rest = refs[8:]
    if mode == "state":
        c_out, n_out, m_out, c_scr, n_scr, m_scr = rest
    elif mode == "h":
        h_out, c_scr, n_scr, m_scr = rest
    else:
        hb_ref, xc_ref, ob_ref, mhg_ref, skip_ref, y_out, c_scr, n_scr, m_scr = rest

    @pl.when(pl.program_id(0) == 0)
    def _():
        c_scr[...] = c0_ref[...]
        n_scr[...] = n0_ref[...]
        m_scr[...] = m0_ref[...]

    L, H = CHUNK, N_HEADS
    ri = lax.broadcasted_iota(jnp.int32, (L, L), 0)
    ci = lax.broadcasted_iota(jnp.int32, (L, L), 1)
    seen_t = (ri >= ci) if reverse else (ri <= ci)
    last = 0 if reverse else L - 1

    b = rows_ref[0:H, :]
    r = rows_ref[H:2 * H, :]
    big_r = rows_ref[2 * H:3 * H, :]
    m = m_scr[...]
    r_last = big_r[:, last:last + 1]
    big_m = jnp.maximum(big_r, m)
    a = jnp.exp(m - big_m)
    sc = jnp.exp(big_r - big_m)
    floor = jnp.exp(-(b + big_m))
    m_last = jnp.maximum(r_last, m)
    d1 = jnp.exp(m - m_last)
    d2 = jnp.exp(r_last - m_last)
    wk = jnp.exp(r - r_last)
    m_scr[...] = b[:, last:last + 1] + m_last

    a_mat = amat_ref[:, 0:4 * H]
    nr_terms = [t.astype(F32) for t in _split3(-big_r)]
    sub = lax.broadcasted_iota(jnp.int32, (H, L), 0)
    ones16 = jnp.ones((16, L), BF16)

    def head(h):
        sl = slice(h * HEAD_DIM, (h + 1) * HEAD_DIM)
        row = slice(h, h + 1)
        qh = q_ref[:, sl]
        kh = k_ref[:, sl]
        vt = v_ref[:, sl].T
        sel = (sub == h).astype(F32)
        dyn = jnp.where(sub == 0, nr_terms[0][row], jnp.where(sub == 1, nr_terms[1][row],
                        jnp.where(sub == 2, nr_terms[2][row], 0.0)))
        b_mat = jnp.concatenate([sel, sel, sel, dyn], axis=0).astype(BF16)
        arg = _dot(a_mat, b_mat)
        kq = _dot_nt(kh, qh)
        yield
        st = (kq * jnp.exp(jnp.where(seen_t, arg, NEG))).astype(BF16)
        c_prev = c_scr[h]
        n_prev = n_scr[row, :]
        wk16 = jnp.broadcast_to(wk[row], (16, L)).astype(BF16)
        vtw = (vt.astype(F32) * wk[row]).astype(BF16)
        upd = _dot(jnp.concatenate([vtw, wk16], axis=0), kh)
        if mode != "state":
            n16 = jnp.broadcast_to(n_prev, (16, HEAD_DIM)).astype(BF16)
            intra = _dot(jnp.concatenate([vt, ones16], axis=0), st)
            inter = _dot_nt(jnp.concatenate([c_prev.astype(BF16), n16], axis=0), qh)
        yield
        c_scr[h] = d1[row] * c_prev + d2[row] * upd[:L]
        n_scr[row, :] = d1[row] * n_prev + d2[row] * upd[L:L + 1]
        if mode != "state":
            num = a[row] * inter[:L] + sc[row] * intra[:L]
            den = a[row] * inter[L:L + 1] + sc[row] * intra[L:L + 1]
            ht = num * (1.0 / jnp.maximum(jnp.abs(den), floor[row]))
            if mode == "h":
                h_out[h] = ht.astype(h_out.dtype)
            else:
                hs = ht + hb_ref[h].astype(F32)
                mu = jnp.mean(hs, axis=0, keepdims=True)
                hc = hs - mu
                var = jnp.mean(hc * hc, axis=0, keepdims=True)
                hn = (hc * lax.rsqrt(var + LN_EPS)).T
                y = _sigmoid(ob_ref[:, sl].astype(F32)) * (
                    hn * mhg_ref[:, sl] + skip_ref[:, sl] * xc_ref[:, sl].astype(F32))
                y_out[:, sl] = y.astype(y_out.dtype)
        yield

    _interleave([head(h) for h in range(H)])

    if mode == "state":
        c_out[...] = c_scr[...]
        n_out[...] = n_scr[...]
        m_out[...] = m_scr[...]


def _mlstm(mode, reverse, q, k, z, v_blk, rows, amat, state, extra=()):
    n = q.shape[0]
    nc = n // CHUNK
    w = N_HEADS * HEAD_DIM
    c0, n0, m0 = state
    d = int(reverse)
    pos = (lambda c: nc - 1 - c) if reverse else (lambda c: c)
    row = lambda blk: pl.BlockSpec((CHUNK, w), lambda c: (pos(c), blk))
    full = lambda a: pl.BlockSpec(a.shape, lambda c: (0,) * a.ndim)
    ht_spec = pl.BlockSpec((None, N_HEADS, HEAD_DIM, CHUNK), lambda c: (pos(c), 0, 0, 0))
    in_specs = [row(0), row(0), row(v_blk),
                pl.BlockSpec((None, 3 * N_HEADS, CHUNK), lambda c: (d, 0, pos(c))),
                pl.BlockSpec((None, CHUNK, LANES), lambda c: (d, pos(c), 0)),
                full(c0), full(n0), full(m0)]
    args = [q, k, z, rows, amat, c0, n0, m0]
    scratch = [pltpu.VMEM(c0.shape, F32), pltpu.VMEM(n0.shape, F32), pltpu.VMEM(m0.shape, F32)]
    if mode == "state":
        out_specs = [full(c0), full(n0), full(m0)]
        out_shape = [jax.ShapeDtypeStruct(a.shape, F32) for a in state]
    elif mode == "h":
        out_specs = ht_spec
        out_shape = jax.ShapeDtypeStruct((nc, N_HEADS, HEAD_DIM, CHUNK), BF16)
    else:
        out_specs = row(0)
        out_shape = jax.ShapeDtypeStruct((n, w), BF16)
        hb, xc, ob_blk, mhg, skip = extra
        in_specs += [ht_spec, row(0), row(ob_blk), full(mhg), full(skip)]
        args += [hb, xc, z, mhg, skip]
    return pl.pallas_call(
        functools.partial(_mlstm_kernel, reverse, mode),
        grid=(nc,),
        in_specs=in_specs, out_specs=out_specs, out_shape=out_shape,
        scratch_shapes=scratch,
        compiler_params=_cparams(("arbitrary",)),
        name="mlstm_%s_%s" % (mode, "bwd" if reverse else "fwd"),
    )(*args)


def _merge_kernel(ua_ref, va_ref, yb_ref, ga0_ref, ga1_ref, gb0_ref, gb1_ref, x_ref,
                  ws_ref, bs_ref, sg_ref, sb_ref, pa_ref, pb_ref, wo_ref,
                  g1_ref, l1g_ref, l1b_ref, sc2_ref, sh2_ref, wr_ref, br_ref,
                  x1_ref, u2_ref, route_ref, route_t_ref, cnt_ref, a_scr, run_scr):
    tm = x_ref.shape[0]
    sub = MERGE_SUB

    @pl.when(pl.program_id(0) == 0)
    def _():
        run_scr[...] = jnp.zeros_like(run_scr)

    run = [run_scr[0:1, :]]

    def rows_of(r0):
        rs = slice(r0, r0 + sub)
        vn = (_ln(_gelu_tanh(va_ref[rs, :].astype(F32))) * sg_ref[...] + sb_ref[...]).astype(BF16)
        for c in range(sub // CHUNK):
            rows = slice(c * CHUNK, (c + 1) * CHUNK)
            dst = slice(r0 + c * CHUNK, r0 + (c + 1) * CHUNK)
            for g in range(N_HEADS):
                cols = slice(g * HEAD_DIM, (g + 1) * HEAD_DIM)
                mixed = _dot(ws_ref[g], vn[rows, cols]) + bs_ref[:, cols]
                a_scr[dst, cols] = (_gelu_tanh(ua_ref[dst, cols].astype(F32)) * mixed).astype(BF16)
        yield
        pa = _dot(a_scr[rs, :], pa_ref[...])
        pb = _dot(yb_ref[rs, :], pb_ref[...])
        yield
        ga = jnp.concatenate([ga0_ref[rs, :], ga1_ref[rs, :]], axis=1).astype(F32)
        gb = jnp.concatenate([gb0_ref[rs, :], gb1_ref[rs, :]], axis=1).astype(F32)
        mrg = (_sigmoid(ga) * pa + _sigmoid(gb) * pb).astype(BF16)
        yield
        y = _dot(mrg, wo_ref[...])
        yield
        x1 = _ln(ALPHA * x_ref[rs, :] + g1_ref[...] * y) * l1g_ref[...] + l1b_ref[...]
        x1_ref[rs, :] = x1
        u2 = _ln(x1) * (1.0 + sc2_ref[...]) + sh2_ref[...]
        _slab_store(u2_ref, (), u2, r0)
        yield
        logit = _dot(u2.astype(BF16), wr_ref[...]) + br_ref[...]
        lane = lax.broadcasted_iota(jnp.int32, logit.shape, 1)
        lane_f = lane.astype(F32)
        is_g = lane < N_GROUPS
        gmax = jnp.max(jnp.where(is_g, logit, NEG), axis=-1, keepdims=True)
        g_sel = jnp.min(jnp.where(is_g & (logit == gmax), lane_f, 1e9), axis=-1, keepdims=True)
        p_g = 1.0 / jnp.sum(jnp.where(is_g, jnp.exp(logit - gmax), 0.0), axis=-1, keepdims=True)
        lo = N_GROUPS + EXP_PER_GROUP * g_sel
        in_grp = (lane_f >= lo) & (lane_f < lo + EXP_PER_GROUP)
        el = jnp.where(in_grp, logit, NEG)
        e1max = jnp.max(el, axis=-1, keepdims=True)
        l1 = jnp.min(jnp.where(in_grp & (el == e1max), lane_f, 1e9), axis=-1, keepdims=True)
        el2 = jnp.where(lane_f == l1, NEG, el)
        e2max = jnp.max(el2, axis=-1, keepdims=True)
        l2 = jnp.min(jnp.where(in_grp & (el2 == e2max), lane_f, 1e9), axis=-1, keepdims=True)
        zsum = jnp.sum(jnp.where(in_grp, jnp.exp(el - e1max), 0.0), axis=-1, keepdims=True)
        p1 = 1.0 / zsum
        p2 = jnp.exp(e2max - e1max) / zsum
        w1 = p_g * p1 / (p1 + p2)
        w2 = p_g * p2 / (p1 + p2)
        e1 = l1 - N_GROUPS
        e2 = l2 - N_GROUPS
        oh1 = (lane_f == e1).astype(BF16)
        oh2 = (lane_f == e2).astype(BF16)
        ri = lax.broadcasted_iota(jnp.int32, (sub, sub), 0)
        ci = lax.broadcasted_iota(jnp.int32, (sub, sub), 1)
        strict = (ci < ri).astype(BF16)
        cnt1 = jnp.sum(oh1.astype(F32), axis=0, keepdims=True)
        cnt2 = jnp.sum(oh2.astype(F32), axis=0, keepdims=True)
        pre1 = _dot(strict, oh1) + run[0]
        pre2 = _dot(strict, oh2) + run[0] + cnt1
        rank1 = jnp.sum(oh1.astype(F32) * pre1, axis=-1, keepdims=True)
        rank2 = jnp.sum(oh2.astype(F32) * pre2, axis=-1, keepdims=True)
        run[0] = run[0] + cnt1 + cnt2
        route = jnp.where(lane == 0, e1, 0.0)
        route = jnp.where(lane == 1, e2, route)
        route = jnp.where(lane == 2, w1, route)
        route = jnp.where(lane == 3, w2, route)
        route = jnp.where(lane == 4, rank1, route)
        route = jnp.where(lane == 5, rank2, route)
        route_ref[rs, :] = route
        route_t_ref[:, rs] = route.T[:route_t_ref.shape[0], :]
        yield

    _interleave([rows_of(r0) for r0 in range(0, tm, sub)])
    run_scr[...] = jnp.broadcast_to(run[0], run_scr.shape)
    cnt_ref[...] = jnp.broadcast_to(run[0], cnt_ref.shape)


def _merge(z, yb, x, ws, bs, sg, sb, pa, pb, wo, g1, l1g, l1b, sc2, sh2, wr, br, tm):
    n, d = x.shape
    w = N_HEADS * HEAD_DIM
    zc = lambda blk: pl.BlockSpec((tm, w), lambda i: (i, blk))
    full = lambda a: pl.BlockSpec(a.shape, lambda i: (0,) * a.ndim)
    consts = [ws, bs, sg, sb, pa, pb, wo, g1, l1g, l1b, sc2, sh2, wr, br]
    return pl.pallas_call(
        _merge_kernel,
        grid=(n // tm,),
        in_specs=[zc(0), zc(1), pl.BlockSpec((tm, w), lambda i: (i, 0)),
                  zc(5), zc(6), zc(7), zc(8),
                  pl.BlockSpec((tm, d), lambda i: (i, 0))] + [full(a) for a in consts],
        out_specs=[pl.BlockSpec((tm, d), lambda i: (i, 0)),
                   pl.BlockSpec((tm * SLAB, d // SLAB), lambda i: (i, 0)),
                   pl.BlockSpec((tm, LANES), lambda i: (i, 0)),
                   pl.BlockSpec((8, tm), lambda i: (0, i)),
                   pl.BlockSpec((8, LANES), lambda i: (0, 0))],
        out_shape=[jax.ShapeDtypeStruct((n, d), F32),
                   jax.ShapeDtypeStruct((n * SLAB, d // SLAB), F32),
                   jax.ShapeDtypeStruct((n, LANES), F32),
                   jax.ShapeDtypeStruct((8, n), F32),
                   jax.ShapeDtypeStruct((8, LANES), F32)],
        scratch_shapes=[pltpu.VMEM((tm, w), BF16), pltpu.VMEM((8, LANES), F32)],
        compiler_params=_cparams(("arbitrary",)),
        name="merge",
    )(z, z, yb, z, z, z, z, x, *consts)


def _moe_kernel(be_ref, slot_ref, kidx_ref, nk_ref, nxt_ref, ubase_ref, nvalid_ref, nb_ref, stok_ref, sdst_ref,
                u_hbm, w1_hbm, w3_hbm, w2_hbm, y_hbm,
                xbuf, obuf, w1b, w3b, w2b, st1, st3, st2, pend, gsem, ssem, wsem, zsem):
    b = pl.program_id(0)
    nb = nb_ref[0]
    bm = xbuf.shape[1] // SLAB
    d = xbuf.shape[2] * SLAB
    kch = st1.shape[1]
    fch = st2.shape[1]
    nf = w2b.shape[1] // fch
    pad_rows = y_hbm.shape[0] - 2 * bm * SLAB

    def gather_start(blk, par):
        base = ubase_ref[blk]
        for t in range(bm):
            row = pl.multiple_of(stok_ref[base + t], SLAB)
            pltpu.make_async_copy(u_hbm.at[pl.ds(row, SLAB), :],
                                  xbuf.at[par, pl.ds(t * SLAB, SLAB), :], gsem.at[par]).start()

    def gather_wait(par):
        pltpu.make_async_copy(u_hbm.at[pl.ds(0, bm * SLAB), :], xbuf.at[par], gsem.at[par]).wait()

    def scatter_start(blk, par):
        base = ubase_ref[blk]
        nvalid = nvalid_ref[blk]
        pad = pad_rows + par * (bm * SLAB)
        for t in range(bm):
            row = pl.multiple_of(jnp.where(t < nvalid, sdst_ref[base + t], pad + t * SLAB), SLAB)
            pltpu.make_async_copy(obuf.at[par, pl.ds(t * SLAB, SLAB), :],
                                  y_hbm.at[pl.ds(row, SLAB), :], ssem.at[par]).start(priority=t % 2)

    def scatter_wait(par):
        pltpu.make_async_copy(obuf.at[par], y_hbm.at[pl.ds(0, bm * SLAB), :], ssem.at[par]).wait()

    def w_rows(f):
        hint = (lambda v, m: v) if isinstance(f, int) else pl.multiple_of
        return (pl.ds(hint(f * kch, kch), kch), pl.ds(hint(f * fch, fch), fch))

    def w_copies(e, f, i):
        r13, r2 = w_rows(f)
        return (pltpu.make_async_copy(w1_hbm.at[e, r13, :], st1.at[i], wsem.at[i, 0]),
                pltpu.make_async_copy(w3_hbm.at[e, r13, :], st3.at[i], wsem.at[i, 1]),
                pltpu.make_async_copy(w2_hbm.at[e, r2, :], st2.at[i], wsem.at[i, 2]))

    def w_start(e, f, i):
        for cp in w_copies(e, f, i):
            cp.start(priority=1)

    def w_wait(e, f, i):
        for cp in w_copies(e, f, i):
            cp.wait()

    def w_cast(s, f, i):
        r13, r2 = w_rows(f)
        w1b[s, r13, :] = st1[i].astype(BF16)
        w3b[s, r13, :] = st3[i].astype(BF16)
        w2b[s, r2, :] = st2[i].astype(BF16)

    def w_plan(blk):
        k = kidx_ref[blk]
        nk = nk_ref[blk]
        g_lo = (k * nf) // nk
        return nxt_ref[blk], g_lo, jnp.where(nxt_ref[blk] < N_EXPERTS, ((k + 1) * nf) // nk - g_lo, 0)

    @pl.when(b < nb)
    def _():
        par = lax.rem(b, 2)
        e = be_ref[b]
        s = slot_ref[b]
        e_next, g_lo, n_groups = w_plan(b)

        @pl.when(b == 0)
        def _():
            obuf[0] = jnp.zeros(obuf.shape[1:], obuf.dtype)
            zero = [pltpu.make_async_copy(obuf.at[0], y_hbm.at[pl.ds(pad_rows + i * bm * SLAB, bm * SLAB), :], zsem)
                    for i in range(2)]
            for cp in zero:
                cp.start()
            gather_start(0, 0)
            for f in range(nf):
                w_start(e, f, f % 2)
                w_wait(e, f, f % 2)
                w_cast(s, f, f % 2)
            for i in range(2):
                pend[2 * i] = s
                pend[2 * i + 1] = nf - 2 + i
            for cp in zero:
                cp.wait()

        gather_wait(par)

        @pl.when(b >= 2)
        def _():
            scatter_wait(par)

        e_prev, g_prev, n_prev = w_plan(jnp.maximum(b - 1, 0))

        @pl.when((b >= 1) & (n_prev == 1))
        def _():
            w_wait(e_prev, g_prev, 1 - par)

        @pl.when(n_groups > 0)
        def _():
            w_start(e_next, g_lo, par)

        gather_start(jnp.minimum(b + 1, nb - 1), 1 - par)
        w_cast(pend[2 * (1 - par)], pend[2 * (1 - par) + 1], 1 - par)
        x = _slab_load(xbuf, (par,), bm, d).astype(BF16)
        hidden = (_silu(_dot(x, w1b[s])) * _dot(x, w3b[s])).astype(BF16)
        _slab_store(obuf, (par,), _dot(hidden, w2b[s]))
        scatter_start(b, par)

        @pl.when(n_groups > 0)
        def _():
            pend[2 * par] = 1 - s
            pend[2 * par + 1] = g_lo + n_groups - 1

        @pl.when(n_groups > 1)
        def _():
            w_wait(e_next, g_lo, par)
            w_cast(1 - s, g_lo, par)

            def more(f, carry):
                w_start(e_next, f, par)
                w_wait(e_next, f, par)
                w_cast(1 - s, f, par)
                return carry

            lax.fori_loop(g_lo + 1, g_lo + n_groups, more, 0)

        @pl.when(b == nb - 1)
        def _():
            gather_wait(1 - par)
            scatter_wait(par)

            @pl.when(nb >= 2)
            def _():
                scatter_wait(1 - par)


def _moe(tables, u2, w1, w3, w2, max_blocks, n_out_rows):
    d = w1.shape[1]
    de = w1.shape[2]
    bm = MOE_BLOCK
    nf = MOE_WGROUPS
    any_spec = pl.BlockSpec(memory_space=pl.ANY)
    return pl.pallas_call(
        _moe_kernel,
        grid_spec=pltpu.PrefetchScalarGridSpec(
            num_scalar_prefetch=len(tables),
            grid=(max_blocks,),
            in_specs=[any_spec] * 4,
            out_specs=any_spec,
            scratch_shapes=[pltpu.VMEM((2, bm * SLAB, d // SLAB), F32), pltpu.VMEM((2, bm * SLAB, d // SLAB), F32),
                            pltpu.VMEM((2, d, de), BF16), pltpu.VMEM((2, d, de), BF16),
                            pltpu.VMEM((2, de, d), BF16),
                            pltpu.VMEM((2, d // nf, de), F32), pltpu.VMEM((2, d // nf, de), F32),
                            pltpu.VMEM((2, de // nf, d), F32), pltpu.SMEM((4,), jnp.int32),
                            pltpu.SemaphoreType.DMA((2,)), pltpu.SemaphoreType.DMA((2,)),
                            pltpu.SemaphoreType.DMA((2, 3)), pltpu.SemaphoreType.DMA(())]),
        out_shape=jax.ShapeDtypeStruct((n_out_rows * SLAB, d // SLAB), F32),
        compiler_params=_cparams(("arbitrary",)),
        name="moe",
    )(*tables, u2, w1, w3, w2)


def _combine_kernel(x1_ref, route_ref, y0_ref, y1_ref, g2_ref, lg_ref, lb_ref, o_ref):
    route = route_ref[...]
    tm, d = x1_ref.shape
    f = route[:, 2:3] * _slab_load(y0_ref, (), tm, d) + route[:, 3:4] * _slab_load(y1_ref, (), tm, d)
    o_ref[...] = _ln(ALPHA * x1_ref[...] + g2_ref[...] * f) * lg_ref[...] + lb_ref[...]


def _combine(x1, route, y2, g2, lg, lb, tm):
    n, d = x1.shape
    vec = pl.BlockSpec((1, d), lambda i: (0, 0))
    nt = n // tm
    return pl.pallas_call(
        _combine_kernel,
        grid=(nt,),
        in_specs=[pl.BlockSpec((tm, d), lambda i: (i, 0)),
                  pl.BlockSpec((tm, LANES), lambda i: (i, 0)),
                  pl.BlockSpec((tm * SLAB, d // SLAB), lambda i: (i, 0)),
                  pl.BlockSpec((tm * SLAB, d // SLAB), lambda i: (i + nt, 0)),
                  vec, vec, vec],
        out_specs=pl.BlockSpec((tm, d), lambda i: (i, 0)),
        out_shape=jax.ShapeDtypeStruct((n, d), F32),
        compiler_params=_cparams(("arbitrary",)),
        name="combine",
    )(x1, route, y2, y2, g2, lg, lb)


def _moe_tables(counts, nblk_e, blk_end, dest1, dest2, n, max_blocks):
    bm = MOE_BLOCK
    i32 = jnp.int32
    blk = jnp.arange(max_blocks)
    be = jnp.minimum(jnp.searchsorted(blk_end, blk, side='right'), N_EXPERTS - 1)
    kidx = blk - (blk_end - nblk_e)[be]
    nk = jnp.maximum(nblk_e[be], 1)
    live = nblk_e > 0
    slot = ((jnp.cumsum(live) - 1)[be]) % 2
    first_live_from = jnp.flip(lax.cummin(jnp.flip(jnp.where(live, jnp.arange(N_EXPERTS), N_EXPERTS))))
    nxt = jnp.concatenate([first_live_from[1:], jnp.full((1,), N_EXPERTS)])[be]
    ubase = (jnp.cumsum(counts) - counts)[be] + kidx * bm
    nvalid = jnp.clip(counts[be] - kidx * bm, 0, bm)
    nb = blk_end[-1:]
    j = jnp.arange(2 * n, dtype=i32)
    _, stok, sdst = lax.sort((jnp.concatenate([dest1, dest2]), (j % n) * SLAB, j * SLAB), num_keys=1)
    tail = jnp.zeros((bm,), i32)
    stok = jnp.concatenate([stok, tail])
    sdst = jnp.concatenate([sdst, tail])
    return [t.astype(i32) for t in (be, slot, kidx, nk, nxt, ubase, nvalid, nb, stok, sdst)]


def _layer(x, ctx, c, c_ctx, w_ada, b_ada, w_in, b_in, w_s, b_s, sgu_g, sgu_b, conv_w, conv_b,
           w_q, w_k, mh_g, skip, p_a, p_b, w_o, ln1_g, ln1_b, w_rg, b_rg, w_re, b_re,
           w1, w3, w2, ln2_g, ln2_b):
    n, d = x.shape
    w = N_HEADS * HEAD_DIM
    nz = 9 * w
    H = N_HEADS
    row = lambda a: a.reshape(1, -1)

    cc = jnp.zeros((8, d), F32).at[0].set(c[0]).at[1].set(c_ctx)
    mod = _ada(cc, w_ada, row(b_ada))
    sh1, sc1, g1, sh2, sc2, g2 = [mod[0:1, i * d:(i + 1) * d] for i in range(6)]
    sh1c, sc1c = mod[1:2, 0:d], mod[1:2, d:2 * d]

    wg = _gate_weights(w_in, 4 * H)
    bg2 = jnp.pad(row(b_in)[:, nz:], ((0, 0), (0, LANES - 4 * H)))
    w_main = w_in.astype(BF16)
    b_main = row(b_in)

    z, gt = _in_proj(x, sc1, sh1, w_main, b_main, wg, bg2, tm=min(n, 1024))
    zc, gct = _in_proj(ctx, sc1c, sh1c, w_main, b_main, wg, bg2, tm=ctx.shape[0])

    cw, cb = conv_w, row(conv_b)
    wq, wk = w_q.astype(BF16), w_k.astype(BF16)
    xc, q, k = _conv_qk(z, 2, cw, cb, wq, wk, tm=min(n, 512))
    _, qc, kc = _conv_qk(zc, 2, cw, cb, wq, wk, tm=ctx.shape[0])

    zero = (jnp.zeros((H, HEAD_DIM, HEAD_DIM), F32), jnp.zeros((H, HEAD_DIM), F32),
            jnp.full((H, LANES), NEG, F32))
    rows_c, amat_c = _gate_prep(gct, tg=ctx.shape[0])
    rows_x, amat_x = _gate_prep(gt, tg=min(n, 1024))
    st_f = _mlstm("state", False, qc, kc, zc, 3, rows_c, amat_c, zero)
    st_b = _mlstm("state", True, qc, kc, zc, 3, rows_c, amat_c, zero)
    hb = _mlstm("h", True, q, k, z, 3, rows_x, amat_x, st_b)
    yb = _mlstm("out", False, q, k, z, 3, rows_x, amat_x, st_f, extra=(hb, xc, 4, row(mh_g), row(skip)))

    bs_full = jnp.repeat(b_s.T, HEAD_DIM, axis=1)
    wr = jnp.zeros((d, LANES), F32).at[:, :N_GROUPS].set(w_rg).at[:, N_GROUPS:N_GROUPS + N_EXPERTS].set(w_re)
    br = jnp.zeros((1, LANES), F32).at[0, :N_GROUPS].set(b_rg).at[0, N_GROUPS:N_GROUPS + N_EXPERTS].set(b_re)
    x1, u2, route, route_t, cnt = _merge(z, yb, x, w_s.astype(BF16), bs_full, row(sgu_g), row(sgu_b),
                                         p_a.astype(BF16), p_b.astype(BF16), w_o.astype(BF16),
                                         g1, row(ln1_g), row(ln1_b), sc2, sh2, wr.astype(BF16), br, tm=min(n, 256))

    bm = MOE_BLOCK
    counts = cnt[0, :N_EXPERTS].astype(jnp.int32)
    nblk_e = (counts + bm - 1) // bm
    blk_end = jnp.cumsum(nblk_e)
    row_start = (blk_end - nblk_e) * bm
    eid = jnp.arange(N_EXPERTS, dtype=F32)[:, None]

    def dest_rows(e, rank):
        start = jnp.sum(jnp.where(e[None, :] == eid, row_start[:, None], 0), axis=0)
        return start + rank.astype(jnp.int32)

    dest1 = dest_rows(route_t[0], route_t[4])
    dest2 = dest_rows(route_t[1], route_t[5])
    max_blocks = (2 * n + N_EXPERTS * (bm - 1)) // bm
    tables = _moe_tables(counts, nblk_e, blk_end, dest1, dest2, n, max_blocks)
    y2 = _moe(tables, u2, w1, w3, w2, max_blocks, 2 * n + 2 * bm)
    return _combine(x1, route, y2, g2, row(ln2_g), row(ln2_b), tm=min(n, 512))


def kernel(x, c, ctx, c_ctx, w_ada, b_ada, w_in, b_in, w_s, b_s, sgu_g, sgu_b, conv_w, conv_b, w_q, w_k, mh_g, skip, p_a, p_b, w_o, ln1_g, ln1_b, w_rg, b_rg, w_re, b_re, w1, w3, w2, ln2_g, ln2_b):
    assert x.shape[0] == 1 and w_ada.shape[0] == DEPTH == 1
    out = _layer(x[0], ctx[0], c, c_ctx, w_ada[0], b_ada[0], w_in[0], b_in[0], w_s[0], b_s[0],
                 sgu_g[0], sgu_b[0], conv_w[0], conv_b[0], w_q[0], w_k[0], mh_g[0], skip[0],
                 p_a[0], p_b[0], w_o[0], ln1_g[0], ln1_b[0], w_rg[0], b_rg[0], w_re[0], b_re[0],
                 w1[0], w3[0], w2[0], ln2_g[0], ln2_b[0])
    return out[None]
```

```python
import functools

import jax
import jax.numpy as jnp
from jax import lax
from jax.experimental import pallas as pl
from jax.experimental.pallas import tpu as pltpu

F32 = jnp.float32
BF16 = jnp.bfloat16

CHUNK = 128
N_HEADS = 8
HEAD_DIM = 128
N_GROUPS = 4
EXP_PER_GROUP = 8
N_EXPERTS = N_GROUPS * EXP_PER_GROUP
LN_EPS = 1e-5
NEG = -1e30
DEPTH = 1
ALPHA = (2 * DEPTH) ** 0.25
LANES = 128
VMEM_LIMIT = 56 * 1024 * 1024

MERGE_SUB = 128
MOE_BLOCK = 256
MOE_WGROUPS = 4


def _cparams(sem):
    return pltpu.CompilerParams(dimension_semantics=sem, vmem_limit_bytes=VMEM_LIMIT)


def _ln(x):
    mu = jnp.mean(x, axis=-1, keepdims=True)
    xc = x - mu
    var = jnp.mean(xc * xc, axis=-1, keepdims=True)
    return xc * lax.rsqrt(var + LN_EPS)


def _sigmoid(x):
    return 1.0 / (1.0 + jnp.exp(-x))


def _silu(x):
    return x * _sigmoid(x)


def _gelu_tanh(x):
    return 0.5 * x * (1.0 + jnp.tanh(0.7978845608028654 * (x + 0.044715 * (x * x * x))))


def _log_sigmoid(x):
    return jnp.minimum(x, 0.0) - jnp.log(1.0 + jnp.exp(-jnp.abs(x)))


def _dot(a, b):
    return jnp.dot(a, b, preferred_element_type=F32)


def _dot_nt(a, b):
    return lax.dot_general(a, b, (((1,), (1,)), ((), ())), preferred_element_type=F32)


def _split2(a):
    hi = a.astype(BF16)
    lo = (a - hi.astype(F32)).astype(BF16)
    return hi, lo


def _split3(a):
    hi = a.astype(BF16)
    r = a - hi.astype(F32)
    mid = r.astype(BF16)
    lo = (r - mid.astype(F32)).astype(BF16)
    return hi, mid, lo


SLAB = 16


def _slab_store(ref, lead, x, r0=0):
    rows, d = x.shape
    per = d // SLAB // LANES
    for c in range(d // LANES):
        ref[lead + (pl.ds(r0 * SLAB + c // per, rows, stride=SLAB),
                    slice((c % per) * LANES, (c % per + 1) * LANES))] = x[:, c * LANES:(c + 1) * LANES]


def _interleave(gens):
    live = list(gens)
    while live:
        for g in list(live):
            try:
                next(g)
            except StopIteration:
                live.remove(g)


def _slab_load(ref, lead, rows, d):
    per = d // SLAB // LANES
    return jnp.concatenate(
        [ref[lead + (pl.ds(c // per, rows, stride=SLAB), slice((c % per) * LANES, (c % per + 1) * LANES))]
         for c in range(d // LANES)], axis=1)


def _ada_kernel(c_ref, w_ref, b_ref, o_ref):
    s_hi, s_lo = _split2(_silu(c_ref[...]))
    w_hi, w_lo = _split2(w_ref[...])
    o_ref[...] = _dot(s_hi, w_hi) + _dot(s_lo, w_hi) + _dot(s_hi, w_lo) + b_ref[...]


def _ada(cc, w, b):
    d, n = w.shape
    tn = 1024
    return pl.pallas_call(
        _ada_kernel,
        grid=(n // tn,),
        in_specs=[pl.BlockSpec((8, d), lambda j: (0, 0)),
                  pl.BlockSpec((d, tn), lambda j: (0, j)),
                  pl.BlockSpec((1, tn), lambda j: (0, j))],
        out_specs=pl.BlockSpec((8, tn), lambda j: (0, j)),
        out_shape=jax.ShapeDtypeStruct((8, n), F32),
        compiler_params=_cparams(("arbitrary",)),
        name="ada",
    )(cc, w, b)


def _gate_weights_kernel(wt_hbm, o_ref, buf, sem):
    ng = buf.shape[0]
    cp = pltpu.make_async_copy(wt_hbm.at[pl.ds(wt_hbm.shape[0] - ng, ng), :], buf, sem)
    cp.start()
    cp.wait()
    wg = jnp.concatenate([buf[...], jnp.zeros((LANES - ng, buf.shape[1]), F32)], axis=0)
    hi, lo = _split2(wg)
    o_ref[...] = jnp.concatenate([hi, lo], axis=0)


def _gate_weights(wt, ng):
    d = wt.shape[1]
    return pl.pallas_call(
        _gate_weights_kernel,
        in_specs=[pl.BlockSpec(memory_space=pl.ANY)],
        out_specs=pl.BlockSpec((2 * LANES, d), lambda: (0, 0)),
        out_shape=jax.ShapeDtypeStruct((2 * LANES, d), BF16),
        scratch_shapes=[pltpu.VMEM((ng, d), F32), pltpu.SemaphoreType.DMA(())],
        compiler_params=pltpu.CompilerParams(vmem_limit_bytes=VMEM_LIMIT),
        name="gate_weights",
    )(wt)


def _inproj_kernel(x_ref, sc_ref, sh_ref, w_ref, b_ref, wg_ref, bg_ref, z_ref, g_ref, u_scr):
    @pl.when(pl.program_id(1) == 0)
    def _():
        u = _ln(x_ref[...]) * (1.0 + sc_ref[...]) + sh_ref[...]
        u_hi, u_lo = _split2(u)
        u_scr[...] = u_hi
        wg = wg_ref[...]
        p = _dot_nt(u_hi, wg)
        g = p[:, :LANES] + p[:, LANES:] + _dot_nt(u_lo, wg[:LANES, :]) + bg_ref[...]
        g_ref[...] = g.T[:g_ref.shape[0], :]

    z_ref[...] = (_dot_nt(u_scr[...], w_ref[...]) + b_ref[...]).astype(z_ref.dtype)


def _in_proj(x, sc, sh, wt, b, wg, bg, tm):
    n, d = x.shape
    tn = 1024
    nz = wt.shape[0] // tn * tn
    ng = bg.shape[1]
    return pl.pallas_call(
        _inproj_kernel,
        grid=(n // tm, nz // tn),
        in_specs=[pl.BlockSpec((tm, d), lambda i, j: (i, 0)),
                  pl.BlockSpec((1, d), lambda i, j: (0, 0)),
                  pl.BlockSpec((1, d), lambda i, j: (0, 0)),
                  pl.BlockSpec((tn, d), lambda i, j: (j, 0)),
                  pl.BlockSpec((1, tn), lambda i, j: (0, j)),
                  pl.BlockSpec((2 * ng, d), lambda i, j: (0, 0)),
                  pl.BlockSpec((1, ng), lambda i, j: (0, 0))],
        out_specs=[pl.BlockSpec((tm, tn), lambda i, j: (i, j)),
                   pl.BlockSpec((4 * N_HEADS, tm), lambda i, j: (0, i))],
        out_shape=[jax.ShapeDtypeStruct((n, nz), BF16),
                   jax.ShapeDtypeStruct((4 * N_HEADS, n), F32)],
        scratch_shapes=[pltpu.VMEM((tm, d), BF16)],
        compiler_params=_cparams(("arbitrary", "arbitrary")),
        name="in_proj",
    )(x, sc, sh, wt, b, wg, bg)


def _convqk_kernel(xm_ref, prev_ref, next_ref, cw_ref, cb_ref, wq_ref, wk_ref,
                   xc_ref, q_ref, k_ref):
    i = pl.program_id(0)
    last = pl.num_programs(0) - 1
    tm = xm_ref.shape[0]
    xm = xm_ref[...].astype(F32)
    prev_row = jnp.where(i == 0, 0.0, prev_ref[...].astype(F32)[-1:, :])
    next_row = jnp.where(i == last, 0.0, next_ref[...].astype(F32)[:1, :])
    row = lax.broadcasted_iota(jnp.int32, xm.shape, 0)
    x_prev = jnp.where(row == 0, prev_row, pltpu.roll(xm, 1, 0))
    x_next = jnp.where(row == tm - 1, next_row, pltpu.roll(xm, tm - 1, 0))
    cw = cw_ref[...]
    y = cw[0:1, :] * x_prev + cw[1:2, :] * xm + cw[2:3, :] * x_next + cb_ref[...]
    xc = _silu(y).astype(BF16)
    xc_ref[...] = xc
    for h in range(N_HEADS):
        sl = slice(h * HEAD_DIM, (h + 1) * HEAD_DIM)
        q_ref[:, sl] = _dot(xc[:, sl], wq_ref[h]).astype(BF16)
        k_ref[:, sl] = (_dot(xc[:, sl], wk_ref[h]) * (HEAD_DIM ** -0.5)).astype(BF16)


def _conv_qk(z, xm_blk, cw, cb, wq, wk, tm):
    n = z.shape[0]
    w = N_HEADS * HEAD_DIM
    halo = 16
    nb = n // halo
    per = tm // halo
    out = jax.ShapeDtypeStruct((n, w), BF16)
    return pl.pallas_call(
        _convqk_kernel,
        grid=(n // tm,),
        in_specs=[pl.BlockSpec((tm, w), lambda i: (i, xm_blk)),
                  pl.BlockSpec((halo, w), lambda i: (jnp.maximum(i * per - 1, 0), xm_blk)),
                  pl.BlockSpec((halo, w), lambda i: (jnp.minimum((i + 1) * per, nb - 1), xm_blk)),
                  pl.BlockSpec((3, w), lambda i: (0, 0)),
                  pl.BlockSpec((1, w), lambda i: (0, 0)),
                  pl.BlockSpec((N_HEADS, HEAD_DIM, HEAD_DIM), lambda i: (0, 0, 0)),
                  pl.BlockSpec((N_HEADS, HEAD_DIM, HEAD_DIM), lambda i: (0, 0, 0))],
        out_specs=[pl.BlockSpec((tm, w), lambda i: (i, 0))] * 3,
        out_shape=[out, out, out],
        compiler_params=_cparams(("arbitrary",)),
        name="conv_qk",
    )(z, z, z, cw, cb, wq, wk)


def _dot3_right(a, t_bf16):
    hi, mid, lo = _split3(a)
    return _dot(hi, t_bf16) + _dot(mid, t_bf16) + _dot(lo, t_bf16)


def _gate_prep_kernel(gt_ref, rows_ref, amat_ref):
    L, H = CHUNK, N_HEADS
    ri = lax.broadcasted_iota(jnp.int32, (L, L), 0)
    ci = lax.broadcasted_iota(jnp.int32, (L, L), 1)
    eye = (ri == ci).astype(BF16)
    ones8 = jnp.ones((H, L), F32)
    pad = jnp.zeros((LANES - 4 * H, L), F32)
    lane = lax.broadcasted_iota(jnp.int32, (H, L), 1)

    def chunk(d, c):
        reverse = bool(d)
        lanes = slice(c * L, (c + 1) * L)
        tri = ((ri >= ci) if reverse else (ri <= ci)).astype(BF16)
        li = gt_ref[2 * d * H:(2 * d + 1) * H, lanes]
        lf = _log_sigmoid(gt_ref[(2 * d + 1) * H:(2 * d + 2) * H, lanes])
        b = _dot3_right(lf, tri)
        yield
        r = li - b
        terms = jnp.concatenate([t.astype(F32) for t in _split3(r)] + [ones8, pad], axis=0).astype(BF16)
        amat_ref[d, lanes, :] = _dot_nt(eye, terms).astype(BF16)
        big_r = r
        s = 1
        while s < L:
            yield
            if reverse:
                big_r = jnp.maximum(big_r, jnp.where(lane < L - s, pltpu.roll(big_r, L - s, 1), NEG))
            else:
                big_r = jnp.maximum(big_r, jnp.where(lane >= s, pltpu.roll(big_r, s, 1), NEG))
            s *= 2
        rows_ref[d, :, lanes] = jnp.concatenate([b, r, big_r], axis=0)

    _interleave([chunk(d, c) for d in (0, 1) for c in range(gt_ref.shape[1] // L)])


def _gate_prep(gt, tg):
    n = gt.shape[1]
    return pl.pallas_call(
        _gate_prep_kernel,
        grid=(n // tg,),
        in_specs=[pl.BlockSpec((4 * N_HEADS, tg), lambda i: (0, i))],
        out_specs=[pl.BlockSpec((2, 3 * N_HEADS, tg), lambda i: (0, 0, i)),
                   pl.BlockSpec((2, tg, LANES), lambda i: (0, i, 0))],
        out_shape=[jax.ShapeDtypeStruct((2, 3 * N_HEADS, n), F32),
                   jax.ShapeDtypeStruct((2, n, LANES), BF16)],
        compiler_params=_cparams(("arbitrary",)),
        name="gate_prep",
    )(gt)


def _mlstm_kernel(reverse, mode, *refs):
    q_ref, k_ref, v_ref, rows_ref, amat_ref, c0_ref, n0_ref, m0_ref = refs[:8]
    rest = refs[8:]
    if mode == "state":
        c_out, n_out, m_out, c_scr, n_scr, m_scr = rest
    elif mode == "h":
        h_out, c_scr, n_scr, m_scr = rest
    else:
        hb_ref, xc_ref, ob_ref, mhg_ref, skip_ref, y_out, c_scr, n_scr, m_scr = rest

    @pl.when(pl.program_id(0) == 0)
    def _():
        c_scr[...] = c0_ref[...]
        n_scr[...] = n0_ref[...]
        m_scr[...] = m0_ref[...]

    L, H = CHUNK, N_HEADS
    ri = lax.broadcasted_iota(jnp.int32, (L, L), 0)
    ci = lax.broadcasted_iota(jnp.int32, (L, L), 1)
    seen_t = (ri >= ci) if reverse else (ri <= ci)
    last = 0 if reverse else L - 1

    b = rows_ref[0:H, :]
    r = rows_ref[H:2 * H, :]
    big_r = rows_ref[2 * H:3 * H, :]
    m = m_scr[...]
    r_last = big_r[:, last:last + 1]
    big_m = jnp.maximum(big_r, m)
    a = jnp.exp(m - big_m)
    sc = jnp.exp(big_r - big_m)
    floor = jnp.exp(-(b + big_m))
    m_last = jnp.maximum(r_last, m)
    d1 = jnp.exp(m - m_last)
    d2 = jnp.exp(r_last - m_last)
    wk = jnp.exp(r - r_last)
    m_scr[...] = b[:, last:last + 1] + m_last

    a_mat = amat_ref[:, 0:4 * H]
    nr_terms = [t.astype(F32) for t in _split3(-big_r)]
    sub = lax.broadcasted_iota(jnp.int32, (H, L), 0)
    ones16 = jnp.ones((16, L), BF16)

    def head(h):
        sl = slice(h * HEAD_DIM, (h + 1) * HEAD_DIM)
        row = slice(h, h + 1)
        qh = q_ref[:, sl]
        kh = k_ref[:, sl]
        vt = v_ref[:, sl].T
        sel = (sub == h).astype(F32)
        dyn = jnp.where(sub == 0, nr_terms[0][row], jnp.where(sub == 1, nr_terms[1][row],
                        jnp.where(sub == 2, nr_terms[2][row], 0.0)))
        b_mat = jnp.concatenate([sel, sel, sel, dyn], axis=0).astype(BF16)
        arg = _dot(a_mat, b_mat)
        kq = _dot_nt(kh, qh)
        yield
        st = (kq * jnp.exp(jnp.where(seen_t, arg, NEG))).astype(BF16)
        c_prev = c_scr[h]
        n_prev = n_scr[row, :]
        wk16 = jnp.broadcast_to(wk[row], (16, L)).astype(BF16)
        vtw = (vt.astype(F32) * wk[row]).astype(BF16)
        upd = _dot(jnp.concatenate([vtw, wk16], axis=0), kh)
        if mode != "state":
            n16 = jnp.broadcast_to(n_prev, (16, HEAD_DIM)).astype(BF16)
            intra = _dot(jnp.concatenate([vt, ones16], axis=0), st)
            inter = _dot_nt(jnp.concatenate([c_prev.astype(BF16), n16], axis=0), qh)
        yield
        c_scr[h] = d1[row] * c_prev + d2[row] * upd[:L]
        n_scr[row, :] = d1[row] * n_prev + d2[row] * upd[L:L + 1]
        if mode != "state":
            num = a[row] * inter[:L] + sc[row] * intra[:L]
            den = a[row] * inter[L:L + 1] + sc[row] * intra[L:L + 1]
            ht = num * (1.0 / jnp.maximum(jnp.abs(den), floor[row]))
            if mode == "h":
                h_out[h] = ht.astype(h_out.dtype)
            else:
                hs = ht + hb_ref[h].astype(F32)
                mu = jnp.mean(hs, axis=0, keepdims=True)
                hc = hs - mu
                var = jnp.mean(hc * hc, axis=0, keepdims=True)
                hn = (hc * lax.rsqrt(var + LN_EPS)).T
                y = _sigmoid(ob_ref[:, sl].astype(F32)) * (
                    hn * mhg_ref[:, sl] + skip_ref[:, sl] * xc_ref[:, sl].astype(F32))
                y_out[:, sl] = y.astype(y_out.dtype)
        yield

    _interleave([head(h) for h in range(H)])

    if mode == "state":
        c_out[...] = c_scr[...]
        n_out[...] = n_scr[...]
        m_out[...] = m_scr[...]


def _mlstm(mode, reverse, q, k, z, v_blk, rows, amat, state, extra=()):
    n = q.shape[0]
    nc = n // CHUNK
    w = N_HEADS * HEAD_DIM
    c0, n0, m0 = state
    d = int(reverse)
    pos = (lambda c: nc - 1 - c) if reverse else (lambda c: c)
    row = lambda blk: pl.BlockSpec((CHUNK, w), lambda c: (pos(c), blk))
    full = lambda a: pl.BlockSpec(a.shape, lambda c: (0,) * a.ndim)
    ht_spec = pl.BlockSpec((None, N_HEADS, HEAD_DIM, CHUNK), lambda c: (pos(c), 0, 0, 0))
    in_specs = [row(0), row(0), row(v_blk),
                pl.BlockSpec((None, 3 * N_HEADS, CHUNK), lambda c: (d, 0, pos(c))),
                pl.BlockSpec((None, CHUNK, LANES), lambda c: (d, pos(c), 0)),
                full(c0), full(n0), full(m0)]
    args = [q, k, z, rows, amat, c0, n0, m0]
    scratch = [pltpu.VMEM(c0.shape, F32), pltpu.VMEM(n0.shape, F32), pltpu.VMEM(m0.shape, F32)]
    if mode == "state":
        out_specs = [full(c0), full(n0), full(m0)]
        out_shape = [jax.ShapeDtypeStruct(a.shape, F32) for a in state]
    elif mode == "h":
        out_specs = ht_spec
        out_shape = jax.ShapeDtypeStruct((nc, N_HEADS, HEAD_DIM, CHUNK), BF16)
    else:
        out_specs = row(0)
        out_shape = jax.ShapeDtypeStruct((n, w), BF16)
        hb, xc, ob_blk, mhg, skip = extra
        in_specs += [ht_spec, row(0), row(ob_blk), full(mhg), full(skip)]
        args += [hb, xc, z, mhg, skip]
    return pl.pallas_call(
        functools.partial(_mlstm_kernel, reverse, mode),
        grid=(nc,),
        in_specs=in_specs, out_specs=out_specs, out_shape=out_shape,
        scratch_shapes=scratch,
        compiler_params=_cparams(("arbitrary",)),
        name="mlstm_%s_%s" % (mode, "bwd" if reverse else "fwd"),
    )(*args)


def _merge_kernel(ua_ref, va_ref, yb_ref, ga0_ref, ga1_ref, gb0_ref, gb1_ref, x_ref,
                  ws_ref, bs_ref, sg_ref, sb_ref, pa_ref, pb_ref, wo_ref,
                  g1_ref, l1g_ref, l1b_ref, sc2_ref, sh2_ref, wr_ref, br_ref,
                  x1_ref, u2_ref, route_ref, route_t_ref, cnt_ref, a_scr, run_scr):
    tm = x_ref.shape[0]
    sub = MERGE_SUB

    @pl.when(pl.program_id(0) == 0)
    def _():
        run_scr[...] = jnp.zeros_like(run_scr)

    run = [run_scr[0:1, :]]

    def rows_of(r0):
        rs = slice(r0, r0 + sub)
        vn = (_ln(_gelu_tanh(va_ref[rs, :].astype(F32))) * sg_ref[...] + sb_ref[...]).astype(BF16)
        for c in range(sub // CHUNK):
            rows = slice(c * CHUNK, (c + 1) * CHUNK)
            dst = slice(r0 + c * CHUNK, r0 + (c + 1) * CHUNK)
            for g in range(N_HEADS):
                cols = slice(g * HEAD_DIM, (g + 1) * HEAD_DIM)
                mixed = _dot(ws_ref[g], vn[rows, cols]) + bs_ref[:, cols]
                a_scr[dst, cols] = (_gelu_tanh(ua_ref[dst, cols].astype(F32)) * mixed).astype(BF16)
        yield
        pa = _dot(a_scr[rs, :], pa_ref[...])
        pb = _dot(yb_ref[rs, :], pb_ref[...])
        yield
        ga = jnp.concatenate([ga0_ref[rs, :], ga1_ref[rs, :]], axis=1).astype(F32)
        gb = jnp.concatenate([gb0_ref[rs, :], gb1_ref[rs, :]], axis=1).astype(F32)
        mrg = (_sigmoid(ga) * pa + _sigmoid(gb) * pb).astype(BF16)
        yield
        y = _dot(mrg, wo_ref[...])
        yield
        x1 = _ln(ALPHA * x_ref[rs, :] + g1_ref[...] * y) * l1g_ref[...] + l1b_ref[...]
        x1_ref[rs, :] = x1
        u2 = _ln(x1) * (1.0 + sc2_ref[...]) + sh2_ref[...]
        _slab_store(u2_ref, (), u2, r0)
        yield
        logit = _dot(u2.astype(BF16), wr_ref[...]) + br_ref[...]
        lane = lax.broadcasted_iota(jnp.int32, logit.shape, 1)
        lane_f = lane.astype(F32)
        is_g = lane < N_GROUPS
        gmax = jnp.max(jnp.where(is_g, logit, NEG), axis=-1, keepdims=True)
        g_sel = jnp.min(jnp.where(is_g & (logit == gmax), lane_f, 1e9), axis=-1, keepdims=True)
        p_g = 1.0 / jnp.sum(jnp.where(is_g, jnp.exp(logit - gmax), 0.0), axis=-1, keepdims=True)
        lo = N_GROUPS + EXP_PER_GROUP * g_sel
        in_grp = (lane_f >= lo) & (lane_f < lo + EXP_PER_GROUP)
        el = jnp.where(in_grp, logit, NEG)
        e1max = jnp.max(el, axis=-1, keepdims=True)
        l1 = jnp.min(jnp.where(in_grp & (el == e1max), lane_f, 1e9), axis=-1, keepdims=True)
        el2 = jnp.where(lane_f == l1, NEG, el)
        e2max = jnp.max(el2, axis=-1, keepdims=True)
        l2 = jnp.min(jnp.where(in_grp & (el2 == e2max), lane_f, 1e9), axis=-1, keepdims=True)
        zsum = jnp.sum(jnp.where(in_grp, jnp.exp(el - e1max), 0.0), axis=-1, keepdims=True)
        p1 = 1.0 / zsum
        p2 = jnp.exp(e2max - e1max) / zsum
        w1 = p_g * p1 / (p1 + p2)
        w2 = p_g * p2 / (p1 + p2)
        e1 = l1 - N_GROUPS
        e2 = l2 - N_GROUPS
        oh1 = (lane_f == e1).astype(BF16)
        oh2 = (lane_f == e2).astype(BF16)
        ri = lax.broadcasted_iota(jnp.int32, (sub, sub), 0)
        ci = lax.broadcasted_iota(jnp.int32, (sub, sub), 1)
        strict = (ci < ri).astype(BF16)
        cnt1 = jnp.sum(oh1.astype(F32), axis=0, keepdims=True)
        cnt2 = jnp.sum(oh2.astype(F32), axis=0, keepdims=True)
        pre1 = _dot(strict, oh1) + run[0]
        pre2 = _dot(strict, oh2) + run[0] + cnt1
        rank1 = jnp.sum(oh1.astype(F32) * pre1, axis=-1, keepdims=True)
        rank2 = jnp.sum(oh2.astype(F32) * pre2, axis=-1, keepdims=True)
        run[0] = run[0] + cnt1 + cnt2
        route = jnp.where(lane == 0, e1, 0.0)
        route = jnp.where(lane == 1, e2, route)
        route = jnp.where(lane == 2, w1, route)
        route = jnp.where(lane == 3, w2, route)
        route = jnp.where(lane == 4, rank1, route)
        route = jnp.where(lane == 5, rank2, route)
        route_ref[rs, :] = route
        route_t_ref[:, rs] = route.T[:route_t_ref.shape[0], :]
        yield

    _interleave([rows_of(r0) for r0 in range(0, tm, sub)])
    run_scr[...] = jnp.broadcast_to(run[0], run_scr.shape)
    cnt_ref[...] = jnp.broadcast_to(run[0], cnt_ref.shape)


def _merge(z, yb, x, ws, bs, sg, sb, pa, pb, wo, g1, l1g, l1b, sc2, sh2, wr, br, tm):
    n, d = x.shape
    w = N_HEADS * HEAD_DIM
    zc = lambda blk: pl.BlockSpec((tm, w), lambda i: (i, blk))
    full = lambda a: pl.BlockSpec(a.shape, lambda i: (0,) * a.ndim)
    consts = [ws, bs, sg, sb, pa, pb, wo, g1, l1g, l1b, sc2, sh2, wr, br]
    return pl.pallas_call(
        _merge_kernel,
        grid=(n // tm,),
        in_specs=[zc(0), zc(1), pl.BlockSpec((tm, w), lambda i: (i, 0)),
                  zc(5), zc(6), zc(7), zc(8),
                  pl.BlockSpec((tm, d), lambda i: (i, 0))] + [full(a) for a in consts],
        out_specs=[pl.BlockSpec((tm, d), lambda i: (i, 0)),
                   pl.BlockSpec((tm * SLAB, d // SLAB), lambda i: (i, 0)),
                   pl.BlockSpec((tm, LANES), lambda i: (i, 0)),
                   pl.BlockSpec((8, tm), lambda i: (0, i)),
                   pl.BlockSpec((8, LANES), lambda i: (0, 0))],
        out_shape=[jax.ShapeDtypeStruct((n, d), F32),
                   jax.ShapeDtypeStruct((n * SLAB, d // SLAB), F32),
                   jax.ShapeDtypeStruct((n, LANES), F32),
                   jax.ShapeDtypeStruct((8, n), F32),
                   jax.ShapeDtypeStruct((8, LANES), F32)],
        scratch_shapes=[pltpu.VMEM((tm, w), BF16), pltpu.VMEM((8, LANES), F32)],
        compiler_params=_cparams(("arbitrary",)),
        name="merge",
    )(z, z, yb, z, z, z, z, x, *consts)


def _moe_kernel(be_ref, slot_ref, kidx_ref, nk_ref, nxt_ref, ubase_ref, nvalid_ref, nb_ref, stok_ref, sdst_ref,
                u_hbm, w1_hbm, w3_hbm, w2_hbm, y_hbm,
                xbuf, obuf, w1b, w3b, w2b, st1, st3, st2, pend, gsem, ssem, wsem, zsem):
    b = pl.program_id(0)
    nb = nb_ref[0]
    bm = xbuf.shape[1] // SLAB
    d = xbuf.shape[2] * SLAB
    kch = st1.shape[1]
    fch = st2.shape[1]
    nf = w2b.shape[1] // fch
    pad_rows = y_hbm.shape[0] - 2 * bm * SLAB

    def gather_start(blk, par):
        base = ubase_ref[blk]
        for t in range(bm):
            row = pl.multiple_of(stok_ref[base + t], SLAB)
            pltpu.make_async_copy(u_hbm.at[pl.ds(row, SLAB), :],
                                  xbuf.at[par, pl.ds(t * SLAB, SLAB), :], gsem.at[par]).start()

    def gather_wait(par):
        pltpu.make_async_copy(u_hbm.at[pl.ds(0, bm * SLAB), :], xbuf.at[par], gsem.at[par]).wait()

    def scatter_start(blk, par):
        base = ubase_ref[blk]
        nvalid = nvalid_ref[blk]
        pad = pad_rows + par * (bm * SLAB)
        for t in range(bm):
            row = pl.multiple_of(jnp.where(t < nvalid, sdst_ref[base + t], pad + t * SLAB), SLAB)
            pltpu.make_async_copy(obuf.at[par, pl.ds(t * SLAB, SLAB), :],
                                  y_hbm.at[pl.ds(row, SLAB), :], ssem.at[par]).start(priority=t % 2)

    def scatter_wait(par):
        pltpu.make_async_copy(obuf.at[par], y_hbm.at[pl.ds(0, bm * SLAB), :], ssem.at[par]).wait()

    def w_rows(f):
        hint = (lambda v, m: v) if isinstance(f, int) else pl.multiple_of
        return (pl.ds(hint(f * kch, kch), kch), pl.ds(hint(f * fch, fch), fch))

    def w_copies(e, f, i):
        r13, r2 = w_rows(f)
        return (pltpu.make_async_copy(w1_hbm.at[e, r13, :], st1.at[i], wsem.at[i, 0]),
                pltpu.make_async_copy(w3_hbm.at[e, r13, :], st3.at[i], wsem.at[i, 1]),
                pltpu.make_async_copy(w2_hbm.at[e, r2, :], st2.at[i], wsem.at[i, 2]))

    def w_start(e, f, i):
        for cp in w_copies(e, f, i):
            cp.start(priority=1)

    def w_wait(e, f, i):
        for cp in w_copies(e, f, i):
            cp.wait()

    def w_cast(s, f, i):
        r13, r2 = w_rows(f)
        w1b[s, r13, :] = st1[i].astype(BF16)
        w3b[s, r13, :] = st3[i].astype(BF16)
        w2b[s, r2, :] = st2[i].astype(BF16)

    def w_plan(blk):
        k = kidx_ref[blk]
        nk = nk_ref[blk]
        g_lo = (k * nf) // nk
        return nxt_ref[blk], g_lo, jnp.where(nxt_ref[blk] < N_EXPERTS, ((k + 1) * nf) // nk - g_lo, 0)

    @pl.when(b < nb)
    def _():
        par = lax.rem(b, 2)
        e = be_ref[b]
        s = slot_ref[b]
        e_next, g_lo, n_groups = w_plan(b)

        @pl.when(b == 0)
        def _():
            obuf[0] = jnp.zeros(obuf.shape[1:], obuf.dtype)
            zero = [pltpu.make_async_copy(obuf.at[0], y_hbm.at[pl.ds(pad_rows + i * bm * SLAB, bm * SLAB), :], zsem)
                    for i in range(2)]
            for cp in zero:
                cp.start()
            gather_start(0, 0)
            for f in range(nf):
                w_start(e, f, f % 2)
                w_wait(e, f, f % 2)
                w_cast(s, f, f % 2)
            for i in range(2):
                pend[2 * i] = s
                pend[2 * i + 1] = nf - 2 + i
            for cp in zero:
                cp.wait()

        gather_wait(par)

        @pl.when(b >= 2)
        def _():
            scatter_wait(par)

        e_prev, g_prev, n_prev = w_plan(jnp.maximum(b - 1, 0))

        @pl.when((b >= 1) & (n_prev == 1))
        def _():
            w_wait(e_prev, g_prev, 1 - par)

        @pl.when(n_groups > 0)
        def _():
            w_start(e_next, g_lo, par)

        gather_start(jnp.minimum(b + 1, nb - 1), 1 - par)
        w_cast(pend[2 * (1 - par)], pend[2 * (1 - par) + 1], 1 - par)
        x = _slab_load(xbuf, (par,), bm, d).astype(BF16)
        hidden = (_silu(_dot(x, w1b[s])) * _dot(x, w3b[s])).astype(BF16)
        _slab_store(obuf, (par,), _dot(hidden, w2b[s]))
        scatter_start(b, par)

        @pl.when(n_groups > 0)
        def _():
            pend[2 * par] = 1 - s
            pend[2 * par + 1] = g_lo + n_groups - 1

        @pl.when(n_groups > 1)
        def _():
            w_wait(e_next, g_lo, par)
            w_cast(1 - s, g_lo, par)

            def more(f, carry):
                w_start(e_next, f, par)
                w_wait(e_next, f, par)
                w_cast(1 - s, f, par)
                return carry

            lax.fori_loop(g_lo + 1, g_lo + n_groups, more, 0)

        @pl.when(b == nb - 1)
        def _():
            gather_wait(1 - par)
            scatter_wait(par)

            @pl.when(nb >= 2)
            def _():
                scatter_wait(1 - par)


def _moe(tables, u2, w1, w3, w2, max_blocks, n_out_rows):
    d = w1.shape[1]
    de = w1.shape[2]
    bm = MOE_BLOCK
    nf = MOE_WGROUPS
    any_spec = pl.BlockSpec(memory_space=pl.ANY)
    return pl.pallas_call(
        _moe_kernel,
        grid_spec=pltpu.PrefetchScalarGridSpec(
            num_scalar_prefetch=len(tables),
            grid=(max_blocks,),
            in_specs=[any_spec] * 4,
            out_specs=any_spec,
            scratch_shapes=[pltpu.VMEM((2, bm * SLAB, d // SLAB), F32), pltpu.VMEM((2, bm * SLAB, d // SLAB), F32),
                            pltpu.VMEM((2, d, de), BF16), pltpu.VMEM((2, d, de), BF16),
                            pltpu.VMEM((2, de, d), BF16),
                            pltpu.VMEM((2, d // nf, de), F32), pltpu.VMEM((2, d // nf, de), F32),
                            pltpu.VMEM((2, de // nf, d), F32), pltpu.SMEM((4,), jnp.int32),
                            pltpu.SemaphoreType.DMA((2,)), pltpu.SemaphoreType.DMA((2,)),
                            pltpu.SemaphoreType.DMA((2, 3)), pltpu.SemaphoreType.DMA(())]),
        out_shape=jax.ShapeDtypeStruct((n_out_rows * SLAB, d // SLAB), F32),
        compiler_params=_cparams(("arbitrary",)),
        name="moe",
    )(*tables, u2, w1, w3, w2)


def _combine_kernel(x1_ref, route_ref, y0_ref, y1_ref, g2_ref, lg_ref, lb_ref, o_ref):
    route = route_ref[...]
    tm, d = x1_ref.shape
    f = route[:, 2:3] * _slab_load(y0_ref, (), tm, d) + route[:, 3:4] * _slab_load(y1_ref, (), tm, d)
    o_ref[...] = _ln(ALPHA * x1_ref[...] + g2_ref[...] * f) * lg_ref[...] + lb_ref[...]


def _combine(x1, route, y2, g2, lg, lb, tm):
    n, d = x1.shape
    vec = pl.BlockSpec((1, d), lambda i: (0, 0))
    nt = n // tm
    return pl.pallas_call(
        _combine_kernel,
        grid=(nt,),
        in_specs=[pl.BlockSpec((tm, d), lambda i: (i, 0)),
                  pl.BlockSpec((tm, LANES), lambda i: (i, 0)),
                  pl.BlockSpec((tm * SLAB, d // SLAB), lambda i: (i, 0)),
                  pl.BlockSpec((tm * SLAB, d // SLAB), lambda i: (i + nt, 0)),
                  vec, vec, vec],
        out_specs=pl.BlockSpec((tm, d), lambda i: (i, 0)),
        out_shape=jax.ShapeDtypeStruct((n, d), F32),
        compiler_params=_cparams(("arbitrary",)),
        name="combine",
    )(x1, route, y2, y2, g2, lg, lb)


def _moe_tables(counts, nblk_e, blk_end, dest1, dest2, n, max_blocks):
    bm = MOE_BLOCK
    i32 = jnp.int32
    blk = jnp.arange(max_blocks)
    be = jnp.minimum(jnp.sum(blk[:, None] >= blk_end[None, :], axis=1), N_EXPERTS - 1)
    kidx = blk - (blk_end - nblk_e)[be]
    nk = jnp.maximum(nblk_e[be], 1)
    live = nblk_e > 0
    slot = ((jnp.cumsum(live) - 1)[be]) % 2
    first_live_from = jnp.flip(lax.cummin(jnp.flip(jnp.where(live, jnp.arange(N_EXPERTS), N_EXPERTS))))
    nxt = jnp.concatenate([first_live_from[1:], jnp.full((1,), N_EXPERTS)])[be]
    ubase = (jnp.cumsum(counts) - counts)[be] + kidx * bm
    nvalid = jnp.clip(counts[be] - kidx * bm, 0, bm)
    nb = blk_end[-1:]
    j = jnp.arange(2 * n, dtype=i32)
    _, stok, sdst = lax.sort((jnp.concatenate([dest1, dest2]), (j % n) * SLAB, j * SLAB), num_keys=1)
    tail = jnp.zeros((bm,), i32)
    stok = jnp.concatenate([stok, tail])
    sdst = jnp.concatenate([sdst, tail])
    return [t.astype(i32) for t in (be, slot, kidx, nk, nxt, ubase, nvalid, nb, stok, sdst)]


def _layer(x, ctx, c, c_ctx, w_ada, b_ada, w_in, b_in, w_s, b_s, sgu_g, sgu_b, conv_w, conv_b,
           w_q, w_k, mh_g, skip, p_a, p_b, w_o, ln1_g, ln1_b, w_rg, b_rg, w_re, b_re,
           w1, w3, w2, ln2_g, ln2_b):
    n, d = x.shape
    w = N_HEADS * HEAD_DIM
    nz = 9 * w
    H = N_HEADS
    row = lambda a: a.reshape(1, -1)

    cc = jnp.zeros((8, d), F32).at[0].set(c[0]).at[1].set(c_ctx)
    mod = _ada(cc, w_ada, row(b_ada))
    sh1, sc1, g1, sh2, sc2, g2 = [mod[0:1, i * d:(i + 1) * d] for i in range(6)]
    sh1c, sc1c = mod[1:2, 0:d], mod[1:2, d:2 * d]

    w_t = w_in.T
    wg = _gate_weights(w_t, 4 * H)
    bg2 = jnp.pad(row(b_in)[:, nz:], ((0, 0), (0, LANES - 4 * H)))
    w_main = w_t.astype(BF16)
    b_main = row(b_in)

    z, gt = _in_proj(x, sc1, sh1, w_main, b_main, wg, bg2, tm=min(n, 1024))
    zc, gct = _in_proj(ctx, sc1c, sh1c, w_main, b_main, wg, bg2, tm=ctx.shape[0])

    cw, cb = conv_w, row(conv_b)
    wq, wk = w_q.astype(BF16), w_k.astype(BF16)
    xc, q, k = _conv_qk(z, 2, cw, cb, wq, wk, tm=min(n, 512))
    _, qc, kc = _conv_qk(zc, 2, cw, cb, wq, wk, tm=ctx.shape[0])

    zero = (jnp.zeros((H, HEAD_DIM, HEAD_DIM), F32), jnp.zeros((H, HEAD_DIM), F32),
            jnp.full((H, LANES), NEG, F32))
    rows_c, amat_c = _gate_prep(gct, tg=ctx.shape[0])
    rows_x, amat_x = _gate_prep(gt, tg=min(n, 1024))
    st_f = _mlstm("state", False, qc, kc, zc, 3, rows_c, amat_c, zero)
    st_b = _mlstm("state", True, qc, kc, zc, 3, rows_c, amat_c, zero)
    hb = _mlstm("h", True, q, k, z, 3, rows_x, amat_x, st_b)
    yb = _mlstm("out", False, q, k, z, 3, rows_x, amat_x, st_f, extra=(hb, xc, 4, row(mh_g), row(skip)))

    bs_full = jnp.repeat(b_s.T, HEAD_DIM, axis=1)
    wr = jnp.zeros((d, LANES), F32).at[:, :N_GROUPS].set(w_rg).at[:, N_GROUPS:N_GROUPS + N_EXPERTS].set(w_re)
    br = jnp.zeros((1, LANES), F32).at[0, :N_GROUPS].set(b_rg).at[0, N_GROUPS:N_GROUPS + N_EXPERTS].set(b_re)
    x1, u2, route, route_t, cnt = _merge(z, yb, x, w_s.astype(BF16), bs_full, row(sgu_g), row(sgu_b),
                                         p_a.astype(BF16), p_b.astype(BF16), w_o.astype(BF16),
                                         g1, row(ln1_g), row(ln1_b), sc2, sh2, wr.astype(BF16), br, tm=min(n, 256))

    bm = MOE_BLOCK
    counts = cnt[0, :N_EXPERTS].astype(jnp.int32)
    nblk_e = (counts + bm - 1) // bm
    blk_end = jnp.cumsum(nblk_e)
    row_start = (blk_end - nblk_e) * bm
    eid = jnp.arange(N_EXPERTS, dtype=F32)[:, None]

    def dest_rows(e, rank):
        start = jnp.sum(jnp.where(e[None, :] == eid, row_start[:, None], 0), axis=0)
        return start + rank.astype(jnp.int32)

    dest1 = dest_rows(route_t[0], route_t[4])
    dest2 = dest_rows(route_t[1], route_t[5])
    max_blocks = (2 * n + N_EXPERTS * (bm - 1)) // bm
    tables = _moe_tables(counts, nblk_e, blk_end, dest1, dest2, n, max_blocks)
    y2 = _moe(tables, u2, w1, w3, w2, max_blocks, 2 * n + 2 * bm)
    return _combine(x1, route, y2, g2, row(ln2_g), row(ln2_b), tm=min(n, 512))


def kernel(x, c, ctx, c_ctx, w_ada, b_ada, w_in, b_in, w_s, b_s, sgu_g, sgu_b, conv_w, conv_b, w_q, w_k, mh_g, skip, p_a, p_b, w_o, ln1_g, ln1_b, w_rg, b_rg, w_re, b_re, w1, w3, w2, ln2_g, ln2_b):
    assert x.shape[0] == 1 and w_ada.shape[0] == DEPTH == 1
    out = _layer(x[0], ctx[0], c, c_ctx, w_ada[0], b_ada[0], w_in[0], b_in[0], w_s[0], b_s[0],
                 sgu_g[0], sgu_b[0], conv_w[0], conv_b[0], w_q[0], w_k[0], mh_g[0], skip[0],
                 p_a[0], p_b[0], w_o[0], ln1_g[0], ln1_b[0], w_rg[0], b_rg[0], w_re[0], b_re[0],
                 w1[0], w3[0], w2[0], ln2_g[0], ln2_b[0])
    return out[None]
```

```python
import functools

import jax
import jax.numpy as jnp
from jax import lax
from jax.experimental import pallas as pl
from jax.experimental.pallas import tpu as pltpu

F32 = jnp.float32
BF16 = jnp.bfloat16

CHUNK = 128
N_HEADS = 8
HEAD_DIM = 128
N_GROUPS = 4
EXP_PER_GROUP = 8
N_EXPERTS = N_GROUPS * EXP_PER_GROUP
LN_EPS = 1e-5
NEG = -1e30
DEPTH = 1
ALPHA = (2 * DEPTH) ** 0.25
LANES = 128
VMEM_LIMIT = 56 * 1024 * 1024

MERGE_SUB = 128
MOE_BLOCK = 256
MOE_WGROUPS = 4


def _cparams(sem):
    return pltpu.CompilerParams(dimension_semantics=sem, vmem_limit_bytes=VMEM_LIMIT)


def _ln(x):
    mu = jnp.mean(x, axis=-1, keepdims=True)
    xc = x - mu
    var = jnp.mean(xc * xc, axis=-1, keepdims=True)
    return xc * lax.rsqrt(var + LN_EPS)


def _sigmoid(x):
    return 1.0 / (1.0 + jnp.exp(-x))


def _silu(x):
    return x * _sigmoid(x)


def _gelu_tanh(x):
    return 0.5 * x * (1.0 + jnp.tanh(0.7978845608028654 * (x + 0.044715 * (x * x * x))))


def _log_sigmoid(x):
    return jnp.minimum(x, 0.0) - jnp.log(1.0 + jnp.exp(-jnp.abs(x)))


def _dot(a, b):
    return jnp.dot(a, b, preferred_element_type=F32)


def _dot_nt(a, b):
    return lax.dot_general(a, b, (((1,), (1,)), ((), ())), preferred_element_type=F32)


def _split2(a):
    hi = a.astype(BF16)
    lo = (a - hi.astype(F32)).astype(BF16)
    return hi, lo


def _split3(a):
    hi = a.astype(BF16)
    r = a - hi.astype(F32)
    mid = r.astype(BF16)
    lo = (r - mid.astype(F32)).astype(BF16)
    return hi, mid, lo


SLAB = 16


def _slab_store(ref, lead, x, r0=0):
    rows, d = x.shape
    per = d // SLAB // LANES
    for c in range(d // LANES):
        ref[lead + (pl.ds(r0 * SLAB + c // per, rows, stride=SLAB),
                    slice((c % per) * LANES, (c % per + 1) * LANES))] = x[:, c * LANES:(c + 1) * LANES]


def _interleave(gens):
    live = list(gens)
    while live:
        for g in list(live):
            try:
                next(g)
            except StopIteration:
                live.remove(g)


def _slab_load(ref, lead, rows, d):
    per = d // SLAB // LANES
    return jnp.concatenate(
        [ref[lead + (pl.ds(c // per, rows, stride=SLAB), slice((c % per) * LANES, (c % per + 1) * LANES))]
         for c in range(d // LANES)], axis=1)


def _ada_kernel(c_ref, w_ref, b_ref, o_ref):
    s_hi, s_lo = _split2(_silu(c_ref[...]))
    w_hi, w_lo = _split2(w_ref[...])
    o_ref[...] = _dot(s_hi, w_hi) + _dot(s_lo, w_hi) + _dot(s_hi, w_lo) + b_ref[...]


def _ada(cc, w, b):
    d, n = w.shape
    tn = 2048
    return pl.pallas_call(
        _ada_kernel,
        grid=(n // tn,),
        in_specs=[pl.BlockSpec((8, d), lambda j: (0, 0)),
                  pl.BlockSpec((d, tn), lambda j: (0, j)),
                  pl.BlockSpec((1, tn), lambda j: (0, j))],
        out_specs=pl.BlockSpec((8, tn), lambda j: (0, j)),
        out_shape=jax.ShapeDtypeStruct((8, n), F32),
        compiler_params=_cparams(("arbitrary",)),
        name="ada",
    )(cc, w, b)


def _gate_weights_kernel(wt_hbm, o_ref, buf, sem):
    ng = buf.shape[0]
    cp = pltpu.make_async_copy(wt_hbm.at[pl.ds(wt_hbm.shape[0] - ng, ng), :], buf, sem)
    cp.start()
    cp.wait()
    wg = jnp.concatenate([buf[...], jnp.zeros((LANES - ng, buf.shape[1]), F32)], axis=0)
    hi, lo = _split2(wg)
    o_ref[...] = jnp.concatenate([hi, lo], axis=0)


def _gate_weights(wt, ng):
    d = wt.shape[1]
    return pl.pallas_call(
        _gate_weights_kernel,
        in_specs=[pl.BlockSpec(memory_space=pl.ANY)],
        out_specs=pl.BlockSpec((2 * LANES, d), lambda: (0, 0)),
        out_shape=jax.ShapeDtypeStruct((2 * LANES, d), BF16),
        scratch_shapes=[pltpu.VMEM((ng, d), F32), pltpu.SemaphoreType.DMA(())],
        compiler_params=pltpu.CompilerParams(vmem_limit_bytes=VMEM_LIMIT),
        name="gate_weights",
    )(wt)


def _inproj_kernel(x_ref, sc_ref, sh_ref, w_ref, b_ref, wg_ref, bg_ref, z_ref, g_ref, u_scr):
    @pl.when(pl.program_id(1) == 0)
    def _():
        u = _ln(x_ref[...]) * (1.0 + sc_ref[...]) + sh_ref[...]
        u_hi, u_lo = _split2(u)
        u_scr[...] = u_hi
        wg = wg_ref[...]
        p = _dot_nt(u_hi, wg)
        g = p[:, :LANES] + p[:, LANES:] + _dot_nt(u_lo, wg[:LANES, :]) + bg_ref[...]
        g_ref[...] = g.T[:g_ref.shape[0], :]

    z_ref[...] = (_dot_nt(u_scr[...], w_ref[...].astype(BF16)) + b_ref[...]).astype(z_ref.dtype)


def _in_proj(x, sc, sh, wt, b, wg, bg, tm):
    n, d = x.shape
    tn = 1024
    nz = wt.shape[0] // tn * tn
    ng = bg.shape[1]
    return pl.pallas_call(
        _inproj_kernel,
        grid=(n // tm, nz // tn),
        in_specs=[pl.BlockSpec((tm, d), lambda i, j: (i, 0)),
                  pl.BlockSpec((1, d), lambda i, j: (0, 0)),
                  pl.BlockSpec((1, d), lambda i, j: (0, 0)),
                  pl.BlockSpec((tn, d), lambda i, j: (j, 0)),
                  pl.BlockSpec((1, tn), lambda i, j: (0, j)),
                  pl.BlockSpec((2 * ng, d), lambda i, j: (0, 0)),
                  pl.BlockSpec((1, ng), lambda i, j: (0, 0))],
        out_specs=[pl.BlockSpec((tm, tn), lambda i, j: (i, j)),
                   pl.BlockSpec((4 * N_HEADS, tm), lambda i, j: (0, i))],
        out_shape=[jax.ShapeDtypeStruct((n, nz), BF16),
                   jax.ShapeDtypeStruct((4 * N_HEADS, n), F32)],
        scratch_shapes=[pltpu.VMEM((tm, d), BF16)],
        compiler_params=_cparams(("arbitrary", "arbitrary")),
        name="in_proj",
    )(x, sc, sh, wt, b, wg, bg)


def _convqk_kernel(xm_ref, prev_ref, next_ref, cw_ref, cb_ref, wq_ref, wk_ref,
                   xc_ref, q_ref, k_ref):
    i = pl.program_id(0)
    last = pl.num_programs(0) - 1
    tm = xm_ref.shape[0]
    xm = xm_ref[...].astype(F32)
    prev_row = jnp.where(i == 0, 0.0, prev_ref[...].astype(F32)[-1:, :])
    next_row = jnp.where(i == last, 0.0, next_ref[...].astype(F32)[:1, :])
    row = lax.broadcasted_iota(jnp.int32, xm.shape, 0)
    x_prev = jnp.where(row == 0, prev_row, pltpu.roll(xm, 1, 0))
    x_next = jnp.where(row == tm - 1, next_row, pltpu.roll(xm, tm - 1, 0))
    cw = cw_ref[...]
    y = cw[0:1, :] * x_prev + cw[1:2, :] * xm + cw[2:3, :] * x_next + cb_ref[...]
    xc = _silu(y).astype(BF16)
    xc_ref[...] = xc
    for h in range(N_HEADS):
        sl = slice(h * HEAD_DIM, (h + 1) * HEAD_DIM)
        q_ref[:, sl] = _dot(xc[:, sl], wq_ref[h]).astype(BF16)
        k_ref[:, sl] = (_dot(xc[:, sl], wk_ref[h]) * (HEAD_DIM ** -0.5)).astype(BF16)


def _conv_qk(z, xm_blk, cw, cb, wq, wk, tm):
    n = z.shape[0]
    w = N_HEADS * HEAD_DIM
    halo = 16
    nb = n // halo
    per = tm // halo
    out = jax.ShapeDtypeStruct((n, w), BF16)
    return pl.pallas_call(
        _convqk_kernel,
        grid=(n // tm,),
        in_specs=[pl.BlockSpec((tm, w), lambda i: (i, xm_blk)),
                  pl.BlockSpec((halo, w), lambda i: (jnp.maximum(i * per - 1, 0), xm_blk)),
                  pl.BlockSpec((halo, w), lambda i: (jnp.minimum((i + 1) * per, nb - 1), xm_blk)),
                  pl.BlockSpec((3, w), lambda i: (0, 0)),
                  pl.BlockSpec((1, w), lambda i: (0, 0)),
                  pl.BlockSpec((N_HEADS, HEAD_DIM, HEAD_DIM), lambda i: (0, 0, 0)),
                  pl.BlockSpec((N_HEADS, HEAD_DIM, HEAD_DIM), lambda i: (0, 0, 0))],
        out_specs=[pl.BlockSpec((tm, w), lambda i: (i, 0))] * 3,
        out_shape=[out, out, out],
        compiler_params=_cparams(("arbitrary",)),
        name="conv_qk",
    )(z, z, z, cw, cb, wq, wk)


def _dot3_right(a, t_bf16):
    hi, mid, lo = _split3(a)
    return _dot(hi, t_bf16) + _dot(mid, t_bf16) + _dot(lo, t_bf16)


def _gate_prep_kernel(gt_ref, rows_ref, amat_ref):
    L, H = CHUNK, N_HEADS
    ri = lax.broadcasted_iota(jnp.int32, (L, L), 0)
    ci = lax.broadcasted_iota(jnp.int32, (L, L), 1)
    eye = (ri == ci).astype(BF16)
    ones8 = jnp.ones((H, L), F32)
    pad = jnp.zeros((LANES - 4 * H, L), F32)
    lane = lax.broadcasted_iota(jnp.int32, (H, L), 1)

    def chunk(d, c):
        reverse = bool(d)
        lanes = slice(c * L, (c + 1) * L)
        tri = ((ri >= ci) if reverse else (ri <= ci)).astype(BF16)
        li = gt_ref[2 * d * H:(2 * d + 1) * H, lanes]
        lf = _log_sigmoid(gt_ref[(2 * d + 1) * H:(2 * d + 2) * H, lanes])
        b = _dot3_right(lf, tri)
        yield
        r = li - b
        terms = jnp.concatenate([t.astype(F32) for t in _split3(r)] + [ones8, pad], axis=0).astype(BF16)
        amat_ref[d, lanes, :] = _dot_nt(eye, terms).astype(BF16)
        big_r = r
        s = 1
        while s < L:
            yield
            if reverse:
                big_r = jnp.maximum(big_r, jnp.where(lane < L - s, pltpu.roll(big_r, L - s, 1), NEG))
            else:
                big_r = jnp.maximum(big_r, jnp.where(lane >= s, pltpu.roll(big_r, s, 1), NEG))
            s *= 2
        rows_ref[d, :, lanes] = jnp.concatenate([b, r, big_r], axis=0)

    _interleave([chunk(d, c) for d in (0, 1) for c in range(gt_ref.shape[1] // L)])


def _gate_prep(gt, tg):
    n = gt.shape[1]
    return pl.pallas_call(
        _gate_prep_kernel,
        grid=(n // tg,),
        in_specs=[pl.BlockSpec((4 * N_HEADS, tg), lambda i: (0, i))],
        out_specs=[pl.BlockSpec((2, 3 * N_HEADS, tg), lambda i: (0, 0, i)),
                   pl.BlockSpec((2, tg, LANES), lambda i: (0, i, 0))],
        out_shape=[jax.ShapeDtypeStruct((2, 3 * N_HEADS, n), F32),
                   jax.ShapeDtypeStruct((2, n, LANES), BF16)],
        compiler_params=_cparams(("arbitrary",)),
        name="gate_prep",
    )(gt)


def _mlstm_kernel(reverse, mode, *refs):
    q_ref, k_ref, v_ref, rows_ref, amat_ref, c0_ref, n0_ref, m0_ref = refs[:8]
    rest = refs[8:]
    if mode == "state":
        c_out, n_out, m_out, c_scr, n_scr, m_scr = rest
    elif mode == "h":
        h_out, c_scr, n_scr, m_scr = rest
    else:
        hb_ref, xc_ref, ob_ref, mhg_ref, skip_ref, y_out, c_scr, n_scr, m_scr = rest

    @pl.when(pl.program_id(0) == 0)
    def _():
        c_scr[...] = c0_ref[...]
        n_scr[...] = n0_ref[...]
        m_scr[...] = m0_ref[...]

    L, H = CHUNK, N_HEADS
    ri = lax.broadcasted_iota(jnp.int32, (L, L), 0)
    ci = lax.broadcasted_iota(jnp.int32, (L, L), 1)
    seen_t = (ri >= ci) if reverse else (ri <= ci)
    last = 0 if reverse else L - 1

    b = rows_ref[0:H, :]
    r = rows_ref[H:2 * H, :]
    big_r = rows_ref[2 * H:3 * H, :]
    m = m_scr[...]
    r_last = big_r[:, last:last + 1]
    big_m = jnp.maximum(big_r, m)
    a = jnp.exp(m - big_m)
    sc = jnp.exp(big_r - big_m)
    floor = jnp.exp(-(b + big_m))
    m_last = jnp.maximum(r_last, m)
    d1 = jnp.exp(m - m_last)
    d2 = jnp.exp(r_last - m_last)
    wk = jnp.exp(r - r_last)
    m_scr[...] = b[:, last:last + 1] + m_last

    a_mat = amat_ref[:, 0:4 * H]
    nr_terms = [t.astype(F32) for t in _split3(-big_r)]
    sub = lax.broadcasted_iota(jnp.int32, (H, L), 0)
    ones16 = jnp.ones((16, L), BF16)

    def head(h):
        sl = slice(h * HEAD_DIM, (h + 1) * HEAD_DIM)
        row = slice(h, h + 1)
        qh = q_ref[:, sl]
        kh = k_ref[:, sl]
        vt = v_ref[:, sl].T
        sel = (sub == h).astype(F32)
        dyn = jnp.where(sub == 0, nr_terms[0][row], jnp.where(sub == 1, nr_terms[1][row],
                        jnp.where(sub == 2, nr_terms[2][row], 0.0)))
        b_mat = jnp.concatenate([sel, sel, sel, dyn], axis=0).astype(BF16)
        arg = _dot(a_mat, b_mat)
        kq = _dot_nt(kh, qh)
        yield
        st = (kq * jnp.exp(jnp.where(seen_t, arg, NEG))).astype(BF16)
        c_prev = c_scr[h]
        n_prev = n_scr[row, :]
        wk16 = jnp.broadcast_to(wk[row], (16, L)).astype(BF16)
        vtw = (vt.astype(F32) * wk[row]).astype(BF16)
        upd = _dot(jnp.concatenate([vtw, wk16], axis=0), kh)
        if mode != "state":
            n16 = jnp.broadcast_to(n_prev, (16, HEAD_DIM)).astype(BF16)
            intra = _dot(jnp.concatenate([vt, ones16], axis=0), st)
            inter = _dot_nt(jnp.concatenate([c_prev.astype(BF16), n16], axis=0), qh)
        yield
        c_scr[h] = d1[row] * c_prev + d2[row] * upd[:L]
        n_scr[row, :] = d1[row] * n_prev + d2[row] * upd[L:L + 1]
        if mode != "state":
            num = a[row] * inter[:L] + sc[row] * intra[:L]
            den = a[row] * inter[L:L + 1] + sc[row] * intra[L:L + 1]
            ht = num * (1.0 / jnp.maximum(jnp.abs(den), floor[row]))
            if mode == "h":
                h_out[h] = ht.astype(h_out.dtype)
            else:
                hs = ht + hb_ref[h].astype(F32)
                mu = jnp.mean(hs, axis=0, keepdims=True)
                hc = hs - mu
                var = jnp.mean(hc * hc, axis=0, keepdims=True)
                hn = (hc * lax.rsqrt(var + LN_EPS)).T
                y = _sigmoid(ob_ref[:, sl].astype(F32)) * (
                    hn * mhg_ref[:, sl] + skip_ref[:, sl] * xc_ref[:, sl].astype(F32))
                y_out[:, sl] = y.astype(y_out.dtype)
        yield

    _interleave([head(h) for h in range(H)])

    if mode == "state":
        c_out[...] = c_scr[...]
        n_out[...] = n_scr[...]
        m_out[...] = m_scr[...]


def _mlstm(mode, reverse, q, k, z, v_blk, rows, amat, state, extra=()):
    n = q.shape[0]
    nc = n // CHUNK
    w = N_HEADS * HEAD_DIM
    c0, n0, m0 = state
    d = int(reverse)
    pos = (lambda c: nc - 1 - c) if reverse else (lambda c: c)
    row = lambda blk: pl.BlockSpec((CHUNK, w), lambda c: (pos(c), blk))
    full = lambda a: pl.BlockSpec(a.shape, lambda c: (0,) * a.ndim)
    ht_spec = pl.BlockSpec((None, N_HEADS, HEAD_DIM, CHUNK), lambda c: (pos(c), 0, 0, 0))
    in_specs = [row(0), row(0), row(v_blk),
                pl.BlockSpec((None, 3 * N_HEADS, CHUNK), lambda c: (d, 0, pos(c))),
                pl.BlockSpec((None, CHUNK, LANES), lambda c: (d, pos(c), 0)),
                full(c0), full(n0), full(m0)]
    args = [q, k, z, rows, amat, c0, n0, m0]
    scratch = [pltpu.VMEM(c0.shape, F32), pltpu.VMEM(n0.shape, F32), pltpu.VMEM(m0.shape, F32)]
    if mode == "state":
        out_specs = [full(c0), full(n0), full(m0)]
        out_shape = [jax.ShapeDtypeStruct(a.shape, F32) for a in state]
    elif mode == "h":
        out_specs = ht_spec
        out_shape = jax.ShapeDtypeStruct((nc, N_HEADS, HEAD_DIM, CHUNK), BF16)
    else:
        out_specs = row(0)
        out_shape = jax.ShapeDtypeStruct((n, w), BF16)
        hb, xc, ob_blk, mhg, skip = extra
        in_specs += [ht_spec, row(0), row(ob_blk), full(mhg), full(skip)]
        args += [hb, xc, z, mhg, skip]
    return pl.pallas_call(
        functools.partial(_mlstm_kernel, reverse, mode),
        grid=(nc,),
        in_specs=in_specs, out_specs=out_specs, out_shape=out_shape,
        scratch_shapes=scratch,
        compiler_params=_cparams(("arbitrary",)),
        name="mlstm_%s_%s" % (mode, "bwd" if reverse else "fwd"),
    )(*args)


def _merge_kernel(ua_ref, va_ref, yb_ref, ga0_ref, ga1_ref, gb0_ref, gb1_ref, x_ref,
                  ws_ref, bs_ref, sg_ref, sb_ref, pa_ref, pb_ref, wo_ref,
                  g1_ref, l1g_ref, l1b_ref, sc2_ref, sh2_ref, wr_ref, br_ref,
                  x1_ref, u2_ref, route_ref, route_t_ref, cnt_ref, a_scr, run_scr):
    tm = x_ref.shape[0]
    sub = MERGE_SUB

    @pl.when(pl.program_id(0) == 0)
    def _():
        run_scr[...] = jnp.zeros_like(run_scr)

    run = [run_scr[0:1, :]]

    def rows_of(r0):
        rs = slice(r0, r0 + sub)
        vn = (_ln(_gelu_tanh(va_ref[rs, :].astype(F32))) * sg_ref[...] + sb_ref[...]).astype(BF16)
        for c in range(sub // CHUNK):
            rows = slice(c * CHUNK, (c + 1) * CHUNK)
            dst = slice(r0 + c * CHUNK, r0 + (c + 1) * CHUNK)
            for g in range(N_HEADS):
                cols = slice(g * HEAD_DIM, (g + 1) * HEAD_DIM)
                mixed = _dot(ws_ref[g], vn[rows, cols]) + bs_ref[:, cols]
                a_scr[dst, cols] = (_gelu_tanh(ua_ref[dst, cols].astype(F32)) * mixed).astype(BF16)
        yield
        pa = _dot(a_scr[rs, :], pa_ref[...])
        pb = _dot(yb_ref[rs, :], pb_ref[...])
        yield
        ga = jnp.concatenate([ga0_ref[rs, :], ga1_ref[rs, :]], axis=1).astype(F32)
        gb = jnp.concatenate([gb0_ref[rs, :], gb1_ref[rs, :]], axis=1).astype(F32)
        mrg = (_sigmoid(ga) * pa + _sigmoid(gb) * pb).astype(BF16)
        yield
        y = _dot(mrg, wo_ref[...])
        yield
        x1 = _ln(ALPHA * x_ref[rs, :] + g1_ref[...] * y) * l1g_ref[...] + l1b_ref[...]
        x1_ref[rs, :] = x1
        u2 = _ln(x1) * (1.0 + sc2_ref[...]) + sh2_ref[...]
        _slab_store(u2_ref, (), u2, r0)
        yield
        logit = _dot(u2.astype(BF16), wr_ref[...]) + br_ref[...]
        lane = lax.broadcasted_iota(jnp.int32, logit.shape, 1)
        lane_f = lane.astype(F32)
        is_g = lane < N_GROUPS
        gmax = jnp.max(jnp.where(is_g, logit, NEG), axis=-1, keepdims=True)
        g_sel = jnp.min(jnp.where(is_g & (logit == gmax), lane_f, 1e9), axis=-1, keepdims=True)
        p_g = 1.0 / jnp.sum(jnp.where(is_g, jnp.exp(logit - gmax), 0.0), axis=-1, keepdims=True)
        lo = N_GROUPS + EXP_PER_GROUP * g_sel
        in_grp = (lane_f >= lo) & (lane_f < lo + EXP_PER_GROUP)
        el = jnp.where(in_grp, logit, NEG)
        e1max = jnp.max(el, axis=-1, keepdims=True)
        l1 = jnp.min(jnp.where(in_grp & (el == e1max), lane_f, 1e9), axis=-1, keepdims=True)
        el2 = jnp.where(lane_f == l1, NEG, el)
        e2max = jnp.max(el2, axis=-1, keepdims=True)
        l2 = jnp.min(jnp.where(in_grp & (el2 == e2max), lane_f, 1e9), axis=-1, keepdims=True)
        zsum = jnp.sum(jnp.where(in_grp, jnp.exp(el - e1max), 0.0), axis=-1, keepdims=True)
        p1 = 1.0 / zsum
        p2 = jnp.exp(e2max - e1max) / zsum
        w1 = p_g * p1 / (p1 + p2)
        w2 = p_g * p2 / (p1 + p2)
        e1 = l1 - N_GROUPS
        e2 = l2 - N_GROUPS
        oh1 = (lane_f == e1).astype(BF16)
        oh2 = (lane_f == e2).astype(BF16)
        ri = lax.broadcasted_iota(jnp.int32, (sub, sub), 0)
        ci = lax.broadcasted_iota(jnp.int32, (sub, sub), 1)
        strict = (ci < ri).astype(BF16)
        cnt1 = jnp.sum(oh1.astype(F32), axis=0, keepdims=True)
        cnt2 = jnp.sum(oh2.astype(F32), axis=0, keepdims=True)
        pre1 = _dot(strict, oh1) + run[0]
        pre2 = _dot(strict, oh2) + run[0] + cnt1
        rank1 = jnp.sum(oh1.astype(F32) * pre1, axis=-1, keepdims=True)
        rank2 = jnp.sum(oh2.astype(F32) * pre2, axis=-1, keepdims=True)
        run[0] = run[0] + cnt1 + cnt2
        route = jnp.where(lane == 0, e1, 0.0)
        route = jnp.where(lane == 1, e2, route)
        route = jnp.where(lane == 2, w1, route)
        route = jnp.where(lane == 3, w2, route)
        route = jnp.where(lane == 4, rank1, route)
        route = jnp.where(lane == 5, rank2, route)
        route_ref[rs, :] = route
        route_t_ref[:, rs] = route.T[:route_t_ref.shape[0], :]
        yield

    _interleave([rows_of(r0) for r0 in range(0, tm, sub)])
    run_scr[...] = jnp.broadcast_to(run[0], run_scr.shape)
    cnt_ref[...] = jnp.broadcast_to(run[0], cnt_ref.shape)


def _merge(z, yb, x, ws, bs, sg, sb, pa, pb, wo, g1, l1g, l1b, sc2, sh2, wr, br, tm):
    n, d = x.shape
    w = N_HEADS * HEAD_DIM
    zc = lambda blk: pl.BlockSpec((tm, w), lambda i: (i, blk))
    full = lambda a: pl.BlockSpec(a.shape, lambda i: (0,) * a.ndim)
    consts = [ws, bs, sg, sb, pa, pb, wo, g1, l1g, l1b, sc2, sh2, wr, br]
    return pl.pallas_call(
        _merge_kernel,
        grid=(n // tm,),
        in_specs=[zc(0), zc(1), pl.BlockSpec((tm, w), lambda i: (i, 0)),
                  zc(5), zc(6), zc(7), zc(8),
                  pl.BlockSpec((tm, d), lambda i: (i, 0))] + [full(a) for a in consts],
        out_specs=[pl.BlockSpec((tm, d), lambda i: (i, 0)),
                   pl.BlockSpec((tm * SLAB, d // SLAB), lambda i: (i, 0)),
                   pl.BlockSpec((tm, LANES), lambda i: (i, 0)),
                   pl.BlockSpec((8, tm), lambda i: (0, i)),
                   pl.BlockSpec((8, LANES), lambda i: (0, 0))],
        out_shape=[jax.ShapeDtypeStruct((n, d), F32),
                   jax.ShapeDtypeStruct((n * SLAB, d // SLAB), F32),
                   jax.ShapeDtypeStruct((n, LANES), F32),
                   jax.ShapeDtypeStruct((8, n), F32),
                   jax.ShapeDtypeStruct((8, LANES), F32)],
        scratch_shapes=[pltpu.VMEM((tm, w), BF16), pltpu.VMEM((8, LANES), F32)],
        compiler_params=_cparams(("arbitrary",)),
        name="merge",
    )(z, z, yb, z, z, z, z, x, *consts)


def _moe_kernel(be_ref, slot_ref, kidx_ref, nk_ref, nxt_ref, ubase_ref, nvalid_ref, nb_ref, stok_ref, sdst_ref,
                u_hbm, w1_hbm, w3_hbm, w2_hbm, y_hbm,
                xbuf, obuf, w1b, w3b, w2b, st1, st3, st2, pend, gsem, ssem, wsem, zsem):
    b = pl.program_id(0)
    nb = nb_ref[0]
    bm = xbuf.shape[1] // SLAB
    d = xbuf.shape[2] * SLAB
    kch = st1.shape[1]
    fch = st2.shape[1]
    nf = w2b.shape[1] // fch
    pad_rows = y_hbm.shape[0] - 2 * bm * SLAB

    def gather_start(blk, par):
        base = ubase_ref[blk]
        for t in range(bm):
            row = pl.multiple_of(stok_ref[base + t], SLAB)
            pltpu.make_async_copy(u_hbm.at[pl.ds(row, SLAB), :],
                                  xbuf.at[par, pl.ds(t * SLAB, SLAB), :], gsem.at[par]).start()

    def gather_wait(par):
        pltpu.make_async_copy(u_hbm.at[pl.ds(0, bm * SLAB), :], xbuf.at[par], gsem.at[par]).wait()

    def scatter_start(blk, par):
        base = ubase_ref[blk]
        nvalid = nvalid_ref[blk]
        pad = pad_rows + par * (bm * SLAB)
        for t in range(bm):
            row = pl.multiple_of(jnp.where(t < nvalid, sdst_ref[base + t], pad + t * SLAB), SLAB)
            pltpu.make_async_copy(obuf.at[par, pl.ds(t * SLAB, SLAB), :],
                                  y_hbm.at[pl.ds(row, SLAB), :], ssem.at[par]).start(priority=t % 2)

    def scatter_wait(par):
        pltpu.make_async_copy(obuf.at[par], y_hbm.at[pl.ds(0, bm * SLAB), :], ssem.at[par]).wait()

    def w_rows(f):
        hint = (lambda v, m: v) if isinstance(f, int) else pl.multiple_of
        return (pl.ds(hint(f * kch, kch), kch), pl.ds(hint(f * fch, fch), fch))

    def w_copies(e, f, i):
        r13, r2 = w_rows(f)
        return (pltpu.make_async_copy(w1_hbm.at[e, r13, :], st1.at[i], wsem.at[i, 0]),
                pltpu.make_async_copy(w3_hbm.at[e, r13, :], st3.at[i], wsem.at[i, 1]),
                pltpu.make_async_copy(w2_hbm.at[e, r2, :], st2.at[i], wsem.at[i, 2]))

    def w_start(e, f, i):
        for cp in w_copies(e, f, i):
            cp.start(priority=1)

    def w_wait(e, f, i):
        for cp in w_copies(e, f, i):
            cp.wait()

    def w_cast(s, f, i):
        r13, r2 = w_rows(f)
        w1b[s, r13, :] = st1[i].astype(BF16)
        w3b[s, r13, :] = st3[i].astype(BF16)
        w2b[s, r2, :] = st2[i].astype(BF16)

    def w_plan(blk):
        k = kidx_ref[blk]
        nk = nk_ref[blk]
        g_lo = (k * nf) // nk
        return nxt_ref[blk], g_lo, jnp.where(nxt_ref[blk] < N_EXPERTS, ((k + 1) * nf) // nk - g_lo, 0)

    @pl.when(b < nb)
    def _():
        par = lax.rem(b, 2)
        e = be_ref[b]
        s = slot_ref[b]
        e_next, g_lo, n_groups = w_plan(b)

        @pl.when(b == 0)
        def _():
            obuf[0] = jnp.zeros(obuf.shape[1:], obuf.dtype)
            zero = [pltpu.make_async_copy(obuf.at[0], y_hbm.at[pl.ds(pad_rows + i * bm * SLAB, bm * SLAB), :], zsem)
                    for i in range(2)]
            for cp in zero:
                cp.start()
            gather_start(0, 0)
            for f in range(nf):
                w_start(e, f, f % 2)
                w_wait(e, f, f % 2)
                w_cast(s, f, f % 2)
            for i in range(2):
                pend[2 * i] = s
                pend[2 * i + 1] = nf - 2 + i
            for cp in zero:
                cp.wait()

        gather_wait(par)

        @pl.when(b >= 2)
        def _():
            scatter_wait(par)

        e_prev, g_prev, n_prev = w_plan(jnp.maximum(b - 1, 0))

        @pl.when((b >= 1) & (n_prev == 1))
        def _():
            w_wait(e_prev, g_prev, 1 - par)

        @pl.when(n_groups > 0)
        def _():
            w_start(e_next, g_lo, par)

        gather_start(jnp.minimum(b + 1, nb - 1), 1 - par)
        w_cast(pend[2 * (1 - par)], pend[2 * (1 - par) + 1], 1 - par)
        x = _slab_load(xbuf, (par,), bm, d).astype(BF16)
        hidden = (_silu(_dot(x, w1b[s])) * _dot(x, w3b[s])).astype(BF16)
        _slab_store(obuf, (par,), _dot(hidden, w2b[s]))
        scatter_start(b, par)

        @pl.when(n_groups > 0)
        def _():
            pend[2 * par] = 1 - s
            pend[2 * par + 1] = g_lo + n_groups - 1

        @pl.when(n_groups > 1)
        def _():
            w_wait(e_next, g_lo, par)
            w_cast(1 - s, g_lo, par)

            def more(f, carry):
                w_start(e_next, f, par)
                w_wait(e_next, f, par)
                w_cast(1 - s, f, par)
                return carry

            lax.fori_loop(g_lo + 1, g_lo + n_groups, more, 0)

        @pl.when(b == nb - 1)
        def _():
            gather_wait(1 - par)
            scatter_wait(par)

            @pl.when(nb >= 2)
            def _():
                scatter_wait(1 - par)


def _moe(tables, u2, w1, w3, w2, max_blocks, n_out_rows):
    d = w1.shape[1]
    de = w1.shape[2]
    bm = MOE_BLOCK
    nf = MOE_WGROUPS
    any_spec = pl.BlockSpec(memory_space=pl.ANY)
    return pl.pallas_call(
        _moe_kernel,
        grid_spec=pltpu.PrefetchScalarGridSpec(
            num_scalar_prefetch=len(tables),
            grid=(max_blocks,),
            in_specs=[any_spec] * 4,
            out_specs=any_spec,
            scratch_shapes=[pltpu.VMEM((2, bm * SLAB, d // SLAB), F32), pltpu.VMEM((2, bm * SLAB, d // SLAB), F32),
                            pltpu.VMEM((2, d, de), BF16), pltpu.VMEM((2, d, de), BF16),
                            pltpu.VMEM((2, de, d), BF16),
                            pltpu.VMEM((2, d // nf, de), F32), pltpu.VMEM((2, d // nf, de), F32),
                            pltpu.VMEM((2, de // nf, d), F32), pltpu.SMEM((4,), jnp.int32),
                            pltpu.SemaphoreType.DMA((2,)), pltpu.SemaphoreType.DMA((2,)),
                            pltpu.SemaphoreType.DMA((2, 3)), pltpu.SemaphoreType.DMA(())]),
        out_shape=jax.ShapeDtypeStruct((n_out_rows * SLAB, d // SLAB), F32),
        compiler_params=_cparams(("arbitrary",)),
        name="moe",
    )(*tables, u2, w1, w3, w2)


def _combine_kernel(x1_ref, route_ref, y0_ref, y1_ref, g2_ref, lg_ref, lb_ref, o_ref):
    route = route_ref[...]
    tm, d = x1_ref.shape
    f = route[:, 2:3] * _slab_load(y0_ref, (), tm, d) + route[:, 3:4] * _slab_load(y1_ref, (), tm, d)
    o_ref[...] = _ln(ALPHA * x1_ref[...] + g2_ref[...] * f) * lg_ref[...] + lb_ref[...]


def _combine(x1, route, y2, g2, lg, lb, tm):
    n, d = x1.shape
    vec = pl.BlockSpec((1, d), lambda i: (0, 0))
    nt = n // tm
    return pl.pallas_call(
        _combine_kernel,
        grid=(nt,),
        in_specs=[pl.BlockSpec((tm, d), lambda i: (i, 0)),
                  pl.BlockSpec((tm, LANES), lambda i: (i, 0)),
                  pl.BlockSpec((tm * SLAB, d // SLAB), lambda i: (i, 0)),
                  pl.BlockSpec((tm * SLAB, d // SLAB), lambda i: (i + nt, 0)),
                  vec, vec, vec],
        out_specs=pl.BlockSpec((tm, d), lambda i: (i, 0)),
        out_shape=jax.ShapeDtypeStruct((n, d), F32),
        compiler_params=_cparams(("arbitrary",)),
        name="combine",
    )(x1, route, y2, y2, g2, lg, lb)


def _moe_tables(counts, nblk_e, blk_end, dest1, dest2, n, max_blocks):
    bm = MOE_BLOCK
    i32 = jnp.int32
    blk = jnp.arange(max_blocks)
    be = jnp.minimum(jnp.sum(blk[:, None] >= blk_end[None, :], axis=1), N_EXPERTS - 1)
    kidx = blk - (blk_end - nblk_e)[be]
    nk = jnp.maximum(nblk_e[be], 1)
    live = nblk_e > 0
    slot = ((jnp.cumsum(live) - 1)[be]) % 2
    first_live_from = jnp.flip(lax.cummin(jnp.flip(jnp.where(live, jnp.arange(N_EXPERTS), N_EXPERTS))))
    nxt = jnp.concatenate([first_live_from[1:], jnp.full((1,), N_EXPERTS)])[be]
    ubase = (jnp.cumsum(counts) - counts)[be] + kidx * bm
    nvalid = jnp.clip(counts[be] - kidx * bm, 0, bm)
    nb = blk_end[-1:]
    j = jnp.arange(2 * n, dtype=i32)
    _, stok, sdst = lax.sort((jnp.concatenate([dest1, dest2]), (j % n) * SLAB, j * SLAB), num_keys=1)
    tail = jnp.zeros((bm,), i32)
    stok = jnp.concatenate([stok, tail])
    sdst = jnp.concatenate([sdst, tail])
    return [t.astype(i32) for t in (be, slot, kidx, nk, nxt, ubase, nvalid, nb, stok, sdst)]


def _layer(x, ctx, c, c_ctx, w_ada, b_ada, w_in, b_in, w_s, b_s, sgu_g, sgu_b, conv_w, conv_b,
           w_q, w_k, mh_g, skip, p_a, p_b, w_o, ln1_g, ln1_b, w_rg, b_rg, w_re, b_re,
           w1, w3, w2, ln2_g, ln2_b):
    n, d = x.shape
    w = N_HEADS * HEAD_DIM
    nz = 9 * w
    H = N_HEADS
    row = lambda a: a.reshape(1, -1)

    cc = jnp.zeros((8, d), F32).at[0].set(c[0]).at[1].set(c_ctx)
    mod = _ada(cc, w_ada, row(b_ada))
    sh1, sc1, g1, sh2, sc2, g2 = [mod[0:1, i * d:(i + 1) * d] for i in range(6)]
    sh1c, sc1c = mod[1:2, 0:d], mod[1:2, d:2 * d]

    w_t = w_in.T
    wg = _gate_weights(w_t, 4 * H)
    bg2 = jnp.pad(row(b_in)[:, nz:], ((0, 0), (0, LANES - 4 * H)))
    b_main = row(b_in)

    z, gt = _in_proj(x, sc1, sh1, w_t, b_main, wg, bg2, tm=min(n, 1024))
    zc, gct = _in_proj(ctx, sc1c, sh1c, w_t, b_main, wg, bg2, tm=ctx.shape[0])

    cw, cb = conv_w, row(conv_b)
    wq, wk = w_q.astype(BF16), w_k.astype(BF16)
    xc, q, k = _conv_qk(z, 2, cw, cb, wq, wk, tm=min(n, 512))
    _, qc, kc = _conv_qk(zc, 2, cw, cb, wq, wk, tm=ctx.shape[0])

    zero = (jnp.zeros((H, HEAD_DIM, HEAD_DIM), F32), jnp.zeros((H, HEAD_DIM), F32),
            jnp.full((H, LANES), NEG, F32))
    rows_c, amat_c = _gate_prep(gct, tg=ctx.shape[0])
    rows_x, amat_x = _gate_prep(gt, tg=min(n, 1024))
    st_f = _mlstm("state", False, qc, kc, zc, 3, rows_c, amat_c, zero)
    st_b = _mlstm("state", True, qc, kc, zc, 3, rows_c, amat_c, zero)
    hb = _mlstm("h", True, q, k, z, 3, rows_x, amat_x, st_b)
    yb = _mlstm("out", False, q, k, z, 3, rows_x, amat_x, st_f, extra=(hb, xc, 4, row(mh_g), row(skip)))

    bs_full = jnp.repeat(b_s.T, HEAD_DIM, axis=1)
    wr = jnp.zeros((d, LANES), F32).at[:, :N_GROUPS].set(w_rg).at[:, N_GROUPS:N_GROUPS + N_EXPERTS].set(w_re)
    br = jnp.zeros((1, LANES), F32).at[0, :N_GROUPS].set(b_rg).at[0, N_GROUPS:N_GROUPS + N_EXPERTS].set(b_re)
    x1, u2, route, route_t, cnt = _merge(z, yb, x, w_s.astype(BF16), bs_full, row(sgu_g), row(sgu_b),
                                         p_a.astype(BF16), p_b.astype(BF16), w_o.astype(BF16),
                                         g1, row(ln1_g), row(ln1_b), sc2, sh2, wr.astype(BF16), br, tm=min(n, 256))

    bm = MOE_BLOCK
    counts = cnt[0, :N_EXPERTS].astype(jnp.int32)
    nblk_e = (counts + bm - 1) // bm
    blk_end = jnp.cumsum(nblk_e)
    row_start = (blk_end - nblk_e) * bm
    eid = jnp.arange(N_EXPERTS, dtype=F32)[:, None]

    def dest_rows(e, rank):
        start = jnp.sum(jnp.where(e[None, :] == eid, row_start[:, None], 0), axis=0)
        return start + rank.astype(jnp.int32)

    dest1 = dest_rows(route_t[0], route_t[4])
    dest2 = dest_rows(route_t[1], route_t[5])
    max_blocks = (2 * n + N_EXPERTS * (bm - 1)) // bm
    tables = _moe_tables(counts, nblk_e, blk_end, dest1, dest2, n, max_blocks)
    y2 = _moe(tables, u2, w1, w3, w2, max_blocks, 2 * n + 2 * bm)
    return _combine(x1, route, y2, g2, row(ln2_g), row(ln2_b), tm=min(n, 512))


def kernel(x, c, ctx, c_ctx, w_ada, b_ada, w_in, b_in, w_s, b_s, sgu_g, sgu_b, conv_w, conv_b, w_q, w_k, mh_g, skip, p_a, p_b, w_o, ln1_g, ln1_b, w_rg, b_rg, w_re, b_re, w1, w3, w2, ln2_g, ln2_b):
    assert x.shape[0] == 1 and w_ada.shape[0] == DEPTH == 1
    out = _layer(x[0], ctx[0], c, c_ctx, w_ada[0], b_ada[0], w_in[0], b_in[0], w_s[0], b_s[0],
                 sgu_g[0], sgu_b[0], conv_w[0], conv_b[0], w_q[0], w_k[0], mh_g[0], skip[0],
                 p_a[0], p_b[0], w_o[0], ln1_g[0], ln1_b[0], w_rg[0], b_rg[0], w_re[0], b_re[0],
                 w1[0], w3[0], w2[0], ln2_g[0], ln2_b[0])
    return out[None]
```

```python
import functools

import jax
import jax.numpy as jnp
from jax import lax
from jax.experimental import pallas as pl
from jax.experimental.pallas import tpu as pltpu

F32 = jnp.float32
BF16 = jnp.bfloat16

CHUNK = 128
N_HEADS = 8
HEAD_DIM = 128
N_GROUPS = 4
EXP_PER_GROUP = 8
N_EXPERTS = N_GROUPS * EXP_PER_GROUP
LN_EPS = 1e-5
NEG = -1e30
DEPTH = 1
ALPHA = (2 * DEPTH) ** 0.25
LANES = 128
VMEM_LIMIT = 56 * 1024 * 1024

MERGE_SUB = 128
MOE_BLOCK = 256
MOE_WGROUPS = 4


def _cparams(sem):
    return pltpu.CompilerParams(dimension_semantics=sem, vmem_limit_bytes=VMEM_LIMIT)


def _ln(x):
    mu = jnp.mean(x, axis=-1, keepdims=True)
    xc = x - mu
    var = jnp.mean(xc * xc, axis=-1, keepdims=True)
    return xc * lax.rsqrt(var + LN_EPS)


def _sigmoid(x):
    return 1.0 / (1.0 + jnp.exp(-x))


def _silu(x):
    return x * _sigmoid(x)


def _gelu_tanh(x):
    return 0.5 * x * (1.0 + jnp.tanh(0.7978845608028654 * (x + 0.044715 * (x * x * x))))


def _log_sigmoid(x):
    return jnp.minimum(x, 0.0) - jnp.log(1.0 + jnp.exp(-jnp.abs(x)))


def _dot(a, b):
    return jnp.dot(a, b, preferred_element_type=F32)


def _dot_nt(a, b):
    return lax.dot_general(a, b, (((1,), (1,)), ((), ())), preferred_element_type=F32)


def _split2(a):
    hi = a.astype(BF16)
    lo = (a - hi.astype(F32)).astype(BF16)
    return hi, lo


def _split3(a):
    hi = a.astype(BF16)
    r = a - hi.astype(F32)
    mid = r.astype(BF16)
    lo = (r - mid.astype(F32)).astype(BF16)
    return hi, mid, lo


SLAB = 8
U32 = jnp.uint32


def _pack_bf16(v):
    half = v.shape[1] // 2
    lo = lax.bitcast_convert_type(v[:, :half].astype(BF16).astype(F32), U32)
    hi = lax.bitcast_convert_type(v[:, half:].astype(BF16).astype(F32), U32)
    return (lo >> 16) | (hi & jnp.uint32(0xFFFF0000))


def _unpack_bf16(w):
    lo = lax.bitcast_convert_type(w << 16, F32)
    hi = lax.bitcast_convert_type(w & jnp.uint32(0xFFFF0000), F32)
    return jnp.concatenate([lo, hi], axis=1)


def _slab_shape(rows, d):
    return (rows * SLAB, d // 2 // SLAB)


def _slab_store(ref, lead, x, r0=0):
    rows, d = x.shape
    per = d // SLAB // LANES
    for c in range(d // LANES):
        ref[lead + (pl.ds(r0 * SLAB + c // per, rows, stride=SLAB),
                    slice((c % per) * LANES, (c % per + 1) * LANES))] = x[:, c * LANES:(c + 1) * LANES]


def _interleave(gens):
    live = list(gens)
    while live:
        for g in list(live):
            try:
                next(g)
            except StopIteration:
                live.remove(g)


def _slab_load(ref, lead, rows, d):
    per = d // SLAB // LANES
    return jnp.concatenate(
        [ref[lead + (pl.ds(c // per, rows, stride=SLAB), slice((c % per) * LANES, (c % per + 1) * LANES))]
         for c in range(d // LANES)], axis=1)


def _ada_kernel(c_ref, w_ref, b_ref, o_ref):
    s_hi, s_lo = _split2(_silu(c_ref[...]))
    w_hi, w_lo = _split2(w_ref[...])
    o_ref[...] = _dot(s_hi, w_hi) + _dot(s_lo, w_hi) + _dot(s_hi, w_lo) + b_ref[...]


def _ada(cc, w, b):
    d, n = w.shape
    tn = 1024
    return pl.pallas_call(
        _ada_kernel,
        grid=(n // tn,),
        in_specs=[pl.BlockSpec((8, d), lambda j: (0, 0)),
                  pl.BlockSpec((d, tn), lambda j: (0, j)),
                  pl.BlockSpec((1, tn), lambda j: (0, j))],
        out_specs=pl.BlockSpec((8, tn), lambda j: (0, j)),
        out_shape=jax.ShapeDtypeStruct((8, n), F32),
        compiler_params=_cparams(("arbitrary",)),
        name="ada",
    )(cc, w, b)


def _gate_weights_kernel(wt_hbm, o_ref, buf, sem):
    ng = buf.shape[0]
    cp = pltpu.make_async_copy(wt_hbm.at[pl.ds(wt_hbm.shape[0] - ng, ng), :], buf, sem)
    cp.start()
    cp.wait()
    wg = jnp.concatenate([buf[...], jnp.zeros((LANES - ng, buf.shape[1]), F32)], axis=0)
    hi, lo = _split2(wg)
    o_ref[...] = jnp.concatenate([hi, lo], axis=0)


def _gate_weights(wt, ng):
    d = wt.shape[1]
    return pl.pallas_call(
        _gate_weights_kernel,
        in_specs=[pl.BlockSpec(memory_space=pl.ANY)],
        out_specs=pl.BlockSpec((2 * LANES, d), lambda: (0, 0)),
        out_shape=jax.ShapeDtypeStruct((2 * LANES, d), BF16),
        scratch_shapes=[pltpu.VMEM((ng, d), F32), pltpu.SemaphoreType.DMA(())],
        compiler_params=pltpu.CompilerParams(vmem_limit_bytes=VMEM_LIMIT),
        name="gate_weights",
    )(wt)


def _inproj_kernel(x_ref, sc_ref, sh_ref, w_ref, b_ref, wg_ref, bg_ref, z_ref, g_ref, u_scr):
    @pl.when(pl.program_id(1) == 0)
    def _():
        u = _ln(x_ref[...]) * (1.0 + sc_ref[...]) + sh_ref[...]
        u_hi, u_lo = _split2(u)
        u_scr[...] = u_hi
        wg = wg_ref[...]
        p = _dot_nt(u_hi, wg)
        g = p[:, :LANES] + p[:, LANES:] + _dot_nt(u_lo, wg[:LANES, :]) + bg_ref[...]
        g_ref[...] = g.T[:g_ref.shape[0], :]

    z_ref[...] = (_dot_nt(u_scr[...], w_ref[...]) + b_ref[...]).astype(z_ref.dtype)


def _in_proj(x, sc, sh, wt, b, wg, bg, tm):
    n, d = x.shape
    tn = 1024
    nz = wt.shape[0] // tn * tn
    ng = bg.shape[1]
    return pl.pallas_call(
        _inproj_kernel,
        grid=(n // tm, nz // tn),
        in_specs=[pl.BlockSpec((tm, d), lambda i, j: (i, 0)),
                  pl.BlockSpec((1, d), lambda i, j: (0, 0)),
                  pl.BlockSpec((1, d), lambda i, j: (0, 0)),
                  pl.BlockSpec((tn, d), lambda i, j: (j, 0)),
                  pl.BlockSpec((1, tn), lambda i, j: (0, j)),
                  pl.BlockSpec((2 * ng, d), lambda i, j: (0, 0)),
                  pl.BlockSpec((1, ng), lambda i, j: (0, 0))],
        out_specs=[pl.BlockSpec((tm, tn), lambda i, j: (i, j)),
                   pl.BlockSpec((4 * N_HEADS, tm), lambda i, j: (0, i))],
        out_shape=[jax.ShapeDtypeStruct((n, nz), BF16),
                   jax.ShapeDtypeStruct((4 * N_HEADS, n), F32)],
        scratch_shapes=[pltpu.VMEM((tm, d), BF16)],
        compiler_params=_cparams(("arbitrary", "arbitrary")),
        name="in_proj",
    )(x, sc, sh, wt, b, wg, bg)


def _convqk_kernel(xm_ref, prev_ref, next_ref, cw_ref, cb_ref, wq_ref, wk_ref,
                   xc_ref, q_ref, k_ref):
    i = pl.program_id(0)
    last = pl.num_programs(0) - 1
    tm = xm_ref.shape[0]
    xm = xm_ref[...].astype(F32)
    prev_row = jnp.where(i == 0, 0.0, prev_ref[...].astype(F32)[-1:, :])
    next_row = jnp.where(i == last, 0.0, next_ref[...].astype(F32)[:1, :])
    row = lax.broadcasted_iota(jnp.int32, xm.shape, 0)
    x_prev = jnp.where(row == 0, prev_row, pltpu.roll(xm, 1, 0))
    x_next = jnp.where(row == tm - 1, next_row, pltpu.roll(xm, tm - 1, 0))
    cw = cw_ref[...]
    y = cw[0:1, :] * x_prev + cw[1:2, :] * xm + cw[2:3, :] * x_next + cb_ref[...]
    xc = _silu(y).astype(BF16)
    xc_ref[...] = xc
    for h in range(N_HEADS):
        sl = slice(h * HEAD_DIM, (h + 1) * HEAD_DIM)
        q_ref[:, sl] = _dot(xc[:, sl], wq_ref[h]).astype(BF16)
        k_ref[:, sl] = (_dot(xc[:, sl], wk_ref[h]) * (HEAD_DIM ** -0.5)).astype(BF16)


def _conv_qk(z, xm_blk, cw, cb, wq, wk, tm):
    n = z.shape[0]
    w = N_HEADS * HEAD_DIM
    halo = 16
    nb = n // halo
    per = tm // halo
    out = jax.ShapeDtypeStruct((n, w), BF16)
    return pl.pallas_call(
        _convqk_kernel,
        grid=(n // tm,),
        in_specs=[pl.BlockSpec((tm, w), lambda i: (i, xm_blk)),
                  pl.BlockSpec((halo, w), lambda i: (jnp.maximum(i * per - 1, 0), xm_blk)),
                  pl.BlockSpec((halo, w), lambda i: (jnp.minimum((i + 1) * per, nb - 1), xm_blk)),
                  pl.BlockSpec((3, w), lambda i: (0, 0)),
                  pl.BlockSpec((1, w), lambda i: (0, 0)),
                  pl.BlockSpec((N_HEADS, HEAD_DIM, HEAD_DIM), lambda i: (0, 0, 0)),
                  pl.BlockSpec((N_HEADS, HEAD_DIM, HEAD_DIM), lambda i: (0, 0, 0))],
        out_specs=[pl.BlockSpec((tm, w), lambda i: (i, 0))] * 3,
        out_shape=[out, out, out],
        compiler_params=_cparams(("arbitrary",)),
        name="conv_qk",
    )(z, z, z, cw, cb, wq, wk)


def _dot3_right(a, t_bf16):
    hi, mid, lo = _split3(a)
    return _dot(hi, t_bf16) + _dot(mid, t_bf16) + _dot(lo, t_bf16)


def _gate_prep_kernel(gt_ref, rows_ref, amat_ref):
    L, H = CHUNK, N_HEADS
    ri = lax.broadcasted_iota(jnp.int32, (L, L), 0)
    ci = lax.broadcasted_iota(jnp.int32, (L, L), 1)
    eye = (ri == ci).astype(BF16)
    ones8 = jnp.ones((H, L), F32)
    pad = jnp.zeros((LANES - 4 * H, L), F32)
    lane = lax.broadcasted_iota(jnp.int32, (H, L), 1)

    def chunk(d, c):
        reverse = bool(d)
        lanes = slice(c * L, (c + 1) * L)
        tri = ((ri >= ci) if reverse else (ri <= ci)).astype(BF16)
        li = gt_ref[2 * d * H:(2 * d + 1) * H, lanes]
        lf = _log_sigmoid(gt_ref[(2 * d + 1) * H:(2 * d + 2) * H, lanes])
        b = _dot3_right(lf, tri)
        yield
        r = li - b
        terms = jnp.concatenate([t.astype(F32) for t in _split3(r)] + [ones8, pad], axis=0).astype(BF16)
        amat_ref[d, lanes, :] = _dot_nt(eye, terms).astype(BF16)
        big_r = r
        s = 1
        while s < L:
            yield
            if reverse:
                big_r = jnp.maximum(big_r, jnp.where(lane < L - s, pltpu.roll(big_r, L - s, 1), NEG))
            else:
                big_r = jnp.maximum(big_r, jnp.where(lane >= s, pltpu.roll(big_r, s, 1), NEG))
            s *= 2
        rows_ref[d, :, lanes] = jnp.concatenate([b, r, big_r], axis=0)

    _interleave([chunk(d, c) for d in (0, 1) for c in range(gt_ref.shape[1] // L)])


def _gate_prep(gt, tg):
    n = gt.shape[1]
    return pl.pallas_call(
        _gate_prep_kernel,
        grid=(n // tg,),
        in_specs=[pl.BlockSpec((4 * N_HEADS, tg), lambda i: (0, i))],
        out_specs=[pl.BlockSpec((2, 3 * N_HEADS, tg), lambda i: (0, 0, i)),
                   pl.BlockSpec((2, tg, LANES), lambda i: (0, i, 0))],
        out_shape=[jax.ShapeDtypeStruct((2, 3 * N_HEADS, n), F32),
                   jax.ShapeDtypeStruct((2, n, LANES), BF16)],
        compiler_params=_cparams(("arbitrary",)),
        name="gate_prep",
    )(gt)


def _mlstm_kernel(reverse, mode, *refs):
    q_ref, k_ref, v_ref, rows_ref, amat_ref, c0_ref, n0_ref, m0_ref = refs[:8]
    rest = refs[8:]
    if mode == "state":
        c_out, n_out, m_out, c_scr, n_scr, m_scr = rest
    elif mode == "h":
        h_out, c_scr, n_scr, m_scr = rest
    else:
        hb_ref, xc_ref, ob_ref, mhg_ref, skip_ref, y_out, c_scr, n_scr, m_scr = rest

    @pl.when(pl.program_id(0) == 0)
    def _():
        c_scr[...] = c0_ref[...]
        n_scr[...] = n0_ref[...]
        m_scr[...] = m0_ref[...]

    L, H = CHUNK, N_HEADS
    ri = lax.broadcasted_iota(jnp.int32, (L, L), 0)
    ci = lax.broadcasted_iota(jnp.int32, (L, L), 1)
    seen_t = (ri >= ci) if reverse else (ri <= ci)
    last = 0 if reverse else L - 1

    b = rows_ref[0:H, :]
    r = rows_ref[H:2 * H, :]
    big_r = rows_ref[2 * H:3 * H, :]
    m = m_scr[...]
    r_last = big_r[:, last:last + 1]
    big_m = jnp.maximum(big_r, m)
    a = jnp.exp(m - big_m)
    sc = jnp.exp(big_r - big_m)
    floor = jnp.exp(-(b + big_m))
    m_last = jnp.maximum(r_last, m)
    d1 = jnp.exp(m - m_last)
    d2 = jnp.exp(r_last - m_last)
    wk = jnp.exp(r - r_last)
    m_scr[...] = b[:, last:last + 1] + m_last

    a_mat = amat_ref[:, 0:4 * H]
    nr_terms = [t.astype(F32) for t in _split3(-big_r)]
    sub = lax.broadcasted_iota(jnp.int32, (H, L), 0)
    ones16 = jnp.ones((16, L), BF16)

    def head(h):
        sl = slice(h * HEAD_DIM, (h + 1) * HEAD_DIM)
        row = slice(h, h + 1)
        qh = q_ref[:, sl]
        kh = k_ref[:, sl]
        vt = v_ref[:, sl].T
        sel = (sub == h).astype(F32)
        dyn = jnp.where(sub == 0, nr_terms[0][row], jnp.where(sub == 1, nr_terms[1][row],
                        jnp.where(sub == 2, nr_terms[2][row], 0.0)))
        b_mat = jnp.concatenate([sel, sel, sel, dyn], axis=0).astype(BF16)
        arg = _dot(a_mat, b_mat)
        kq = _dot_nt(kh, qh)
        yield
        st = (kq * jnp.exp(jnp.where(seen_t, arg, NEG))).astype(BF16)
        c_prev = c_scr[h]
        n_prev = n_scr[row, :]
        wk16 = jnp.broadcast_to(wk[row], (16, L)).astype(BF16)
        vtw = (vt.astype(F32) * wk[row]).astype(BF16)
        upd = _dot(jnp.concatenate([vtw, wk16], axis=0), kh)
        if mode != "state":
            n16 = jnp.broadcast_to(n_prev, (16, HEAD_DIM)).astype(BF16)
            intra = _dot(jnp.concatenate([vt, ones16], axis=0), st)
            inter = _dot_nt(jnp.concatenate([c_prev.astype(BF16), n16], axis=0), qh)
        yield
        c_scr[h] = d1[row] * c_prev + d2[row] * upd[:L]
        n_scr[row, :] = d1[row] * n_prev + d2[row] * upd[L:L + 1]
        if mode != "state":
            num = a[row] * inter[:L] + sc[row] * intra[:L]
            den = a[row] * inter[L:L + 1] + sc[row] * intra[L:L + 1]
            ht = num * (1.0 / jnp.maximum(jnp.abs(den), floor[row]))
            if mode == "h":
                h_out[h] = ht.astype(h_out.dtype)
            else:
                hs = ht + hb_ref[h].astype(F32)
                mu = jnp.mean(hs, axis=0, keepdims=True)
                hc = hs - mu
                var = jnp.mean(hc * hc, axis=0, keepdims=True)
                hn = (hc * lax.rsqrt(var + LN_EPS)).T
                y = _sigmoid(ob_ref[:, sl].astype(F32)) * (
                    hn * mhg_ref[:, sl] + skip_ref[:, sl] * xc_ref[:, sl].astype(F32))
                y_out[:, sl] = y.astype(y_out.dtype)
        yield

    _interleave([head(h) for h in range(H)])

    if mode == "state":
        c_out[...] = c_scr[...]
        n_out[...] = n_scr[...]
        m_out[...] = m_scr[...]


def _mlstm(mode, reverse, q, k, z, v_blk, rows, amat, state, extra=()):
    n = q.shape[0]
    nc = n // CHUNK
    w = N_HEADS * HEAD_DIM
    c0, n0, m0 = state
    d = int(reverse)
    pos = (lambda c: nc - 1 - c) if reverse else (lambda c: c)
    row = lambda blk: pl.BlockSpec((CHUNK, w), lambda c: (pos(c), blk))
    full = lambda a: pl.BlockSpec(a.shape, lambda c: (0,) * a.ndim)
    ht_spec = pl.BlockSpec((None, N_HEADS, HEAD_DIM, CHUNK), lambda c: (pos(c), 0, 0, 0))
    in_specs = [row(0), row(0), row(v_blk),
                pl.BlockSpec((None, 3 * N_HEADS, CHUNK), lambda c: (d, 0, pos(c))),
                pl.BlockSpec((None, CHUNK, LANES), lambda c: (d, pos(c), 0)),
                full(c0), full(n0), full(m0)]
    args = [q, k, z, rows, amat, c0, n0, m0]
    scratch = [pltpu.VMEM(c0.shape, F32), pltpu.VMEM(n0.shape, F32), pltpu.VMEM(m0.shape, F32)]
    if mode == "state":
        out_specs = [full(c0), full(n0), full(m0)]
        out_shape = [jax.ShapeDtypeStruct(a.shape, F32) for a in state]
    elif mode == "h":
        out_specs = ht_spec
        out_shape = jax.ShapeDtypeStruct((nc, N_HEADS, HEAD_DIM, CHUNK), BF16)
    else:
        out_specs = row(0)
        out_shape = jax.ShapeDtypeStruct((n, w), BF16)
        hb, xc, ob_blk, mhg, skip = extra
        in_specs += [ht_spec, row(0), row(ob_blk), full(mhg), full(skip)]
        args += [hb, xc, z, mhg, skip]
    return pl.pallas_call(
        functools.partial(_mlstm_kernel, reverse, mode),
        grid=(nc,),
        in_specs=in_specs, out_specs=out_specs, out_shape=out_shape,
        scratch_shapes=scratch,
        compiler_params=_cparams(("arbitrary",)),
        name="mlstm_%s_%s" % (mode, "bwd" if reverse else "fwd"),
    )(*args)


def _merge_kernel(ua_ref, va_ref, yb_ref, ga0_ref, ga1_ref, gb0_ref, gb1_ref, x_ref,
                  ws_ref, bs_ref, sg_ref, sb_ref, pa_ref, pb_ref, wo_ref,
                  g1_ref, l1g_ref, l1b_ref, sc2_ref, sh2_ref, wr_ref, br_ref,
                  x1_ref, u2_ref, route_ref, route_t_ref, cnt_ref, a_scr, run_scr):
    tm = x_ref.shape[0]
    sub = MERGE_SUB

    @pl.when(pl.program_id(0) == 0)
    def _():
        run_scr[...] = jnp.zeros_like(run_scr)

    run = [run_scr[0:1, :]]

    def rows_of(r0):
        rs = slice(r0, r0 + sub)
        vn = (_ln(_gelu_tanh(va_ref[rs, :].astype(F32))) * sg_ref[...] + sb_ref[...]).astype(BF16)
        for c in range(sub // CHUNK):
            rows = slice(c * CHUNK, (c + 1) * CHUNK)
            dst = slice(r0 + c * CHUNK, r0 + (c + 1) * CHUNK)
            for g in range(N_HEADS):
                cols = slice(g * HEAD_DIM, (g + 1) * HEAD_DIM)
                mixed = _dot(ws_ref[g], vn[rows, cols]) + bs_ref[:, cols]
                a_scr[dst, cols] = (_gelu_tanh(ua_ref[dst, cols].astype(F32)) * mixed).astype(BF16)
        yield
        pa = _dot(a_scr[rs, :], pa_ref[...])
        pb = _dot(yb_ref[rs, :], pb_ref[...])
        yield
        ga = jnp.concatenate([ga0_ref[rs, :], ga1_ref[rs, :]], axis=1).astype(F32)
        gb = jnp.concatenate([gb0_ref[rs, :], gb1_ref[rs, :]], axis=1).astype(F32)
        mrg = (_sigmoid(ga) * pa + _sigmoid(gb) * pb).astype(BF16)
        yield
        y = _dot(mrg, wo_ref[...])
        yield
        x1 = _ln(ALPHA * x_ref[rs, :] + g1_ref[...] * y) * l1g_ref[...] + l1b_ref[...]
        x1_ref[rs, :] = x1
        u2 = _ln(x1) * (1.0 + sc2_ref[...]) + sh2_ref[...]
        _slab_store(u2_ref, (), _pack_bf16(u2), r0)
        yield
        logit = _dot(u2.astype(BF16), wr_ref[...]) + br_ref[...]
        lane = lax.broadcasted_iota(jnp.int32, logit.shape, 1)
        lane_f = lane.astype(F32)
        is_g = lane < N_GROUPS
        gmax = jnp.max(jnp.where(is_g, logit, NEG), axis=-1, keepdims=True)
        g_sel = jnp.min(jnp.where(is_g & (logit == gmax), lane_f, 1e9), axis=-1, keepdims=True)
        p_g = 1.0 / jnp.sum(jnp.where(is_g, jnp.exp(logit - gmax), 0.0), axis=-1, keepdims=True)
        lo = N_GROUPS + EXP_PER_GROUP * g_sel
        in_grp = (lane_f >= lo) & (lane_f < lo + EXP_PER_GROUP)
        el = jnp.where(in_grp, logit, NEG)
        e1max = jnp.max(el, axis=-1, keepdims=True)
        l1 = jnp.min(jnp.where(in_grp & (el == e1max), lane_f, 1e9), axis=-1, keepdims=True)
        el2 = jnp.where(lane_f == l1, NEG, el)
        e2max = jnp.max(el2, axis=-1, keepdims=True)
        l2 = jnp.min(jnp.where(in_grp & (el2 == e2max), lane_f, 1e9), axis=-1, keepdims=True)
        zsum = jnp.sum(jnp.where(in_grp, jnp.exp(el - e1max), 0.0), axis=-1, keepdims=True)
        p1 = 1.0 / zsum
        p2 = jnp.exp(e2max - e1max) / zsum
        w1 = p_g * p1 / (p1 + p2)
        w2 = p_g * p2 / (p1 + p2)
        e1 = l1 - N_GROUPS
        e2 = l2 - N_GROUPS
        oh1 = (lane_f == e1).astype(BF16)
        oh2 = (lane_f == e2).astype(BF16)
        ri = lax.broadcasted_iota(jnp.int32, (sub, sub), 0)
        ci = lax.broadcasted_iota(jnp.int32, (sub, sub), 1)
        strict = (ci < ri).astype(BF16)
        cnt1 = jnp.sum(oh1.astype(F32), axis=0, keepdims=True)
        cnt2 = jnp.sum(oh2.astype(F32), axis=0, keepdims=True)
        pre1 = _dot(strict, oh1) + run[0]
        pre2 = _dot(strict, oh2) + run[0] + cnt1
        rank1 = jnp.sum(oh1.astype(F32) * pre1, axis=-1, keepdims=True)
        rank2 = jnp.sum(oh2.astype(F32) * pre2, axis=-1, keepdims=True)
        run[0] = run[0] + cnt1 + cnt2
        route = jnp.where(lane == 0, e1, 0.0)
        route = jnp.where(lane == 1, e2, route)
        route = jnp.where(lane == 2, w1, route)
        route = jnp.where(lane == 3, w2, route)
        route = jnp.where(lane == 4, rank1, route)
        route = jnp.where(lane == 5, rank2, route)
        route_ref[rs, :] = route
        route_t_ref[:, rs] = route.T[:route_t_ref.shape[0], :]
        yield

    _interleave([rows_of(r0) for r0 in range(0, tm, sub)])
    run_scr[...] = jnp.broadcast_to(run[0], run_scr.shape)
    cnt_ref[...] = jnp.broadcast_to(run[0], cnt_ref.shape)


def _merge(z, yb, x, ws, bs, sg, sb, pa, pb, wo, g1, l1g, l1b, sc2, sh2, wr, br, tm):
    n, d = x.shape
    w = N_HEADS * HEAD_DIM
    zc = lambda blk: pl.BlockSpec((tm, w), lambda i: (i, blk))
    full = lambda a: pl.BlockSpec(a.shape, lambda i: (0,) * a.ndim)
    consts = [ws, bs, sg, sb, pa, pb, wo, g1, l1g, l1b, sc2, sh2, wr, br]
    return pl.pallas_call(
        _merge_kernel,
        grid=(n // tm,),
        in_specs=[zc(0), zc(1), pl.BlockSpec((tm, w), lambda i: (i, 0)),
                  zc(5), zc(6), zc(7), zc(8),
                  pl.BlockSpec((tm, d), lambda i: (i, 0))] + [full(a) for a in consts],
        out_specs=[pl.BlockSpec((tm, d), lambda i: (i, 0)),
                   pl.BlockSpec(_slab_shape(tm, d), lambda i: (i, 0)),
                   pl.BlockSpec((tm, LANES), lambda i: (i, 0)),
                   pl.BlockSpec((8, tm), lambda i: (0, i)),
                   pl.BlockSpec((8, LANES), lambda i: (0, 0))],
        out_shape=[jax.ShapeDtypeStruct((n, d), F32),
                   jax.ShapeDtypeStruct(_slab_shape(n, d), U32),
                   jax.ShapeDtypeStruct((n, LANES), F32),
                   jax.ShapeDtypeStruct((8, n), F32),
                   jax.ShapeDtypeStruct((8, LANES), F32)],
        scratch_shapes=[pltpu.VMEM((tm, w), BF16), pltpu.VMEM((8, LANES), F32)],
        compiler_params=_cparams(("arbitrary",)),
        name="merge",
    )(z, z, yb, z, z, z, z, x, *consts)


def _moe_kernel(be_ref, slot_ref, kidx_ref, nk_ref, nxt_ref, ubase_ref, nvalid_ref, nb_ref, stok_ref, sdst_ref,
                u_hbm, w1_hbm, w3_hbm, w2_hbm, y_hbm,
                xbuf, obuf, w1b, w3b, w2b, st1, st3, st2, pend, gsem, ssem, wsem, zsem):
    b = pl.program_id(0)
    nb = nb_ref[0]
    bm = xbuf.shape[1] // SLAB
    d = xbuf.shape[2] * SLAB * 2
    kch = st1.shape[1]
    fch = st2.shape[1]
    nf = w2b.shape[1] // fch
    pad_rows = y_hbm.shape[0] - 2 * bm * SLAB

    def gather_start(blk, par):
        base = ubase_ref[blk]
        for t in range(bm):
            row = pl.multiple_of(stok_ref[base + t], SLAB)
            pltpu.make_async_copy(u_hbm.at[pl.ds(row, SLAB), :],
                                  xbuf.at[par, pl.ds(t * SLAB, SLAB), :], gsem.at[par]).start()

    def gather_wait(par):
        pltpu.make_async_copy(u_hbm.at[pl.ds(0, bm * SLAB), :], xbuf.at[par], gsem.at[par]).wait()

    def scatter_start(blk, par):
        base = ubase_ref[blk]
        nvalid = nvalid_ref[blk]
        pad = pad_rows + par * (bm * SLAB)
        for t in range(bm):
            row = pl.multiple_of(jnp.where(t < nvalid, sdst_ref[base + t], pad + t * SLAB), SLAB)
            pltpu.make_async_copy(obuf.at[par, pl.ds(t * SLAB, SLAB), :],
                                  y_hbm.at[pl.ds(row, SLAB), :], ssem.at[par]).start(priority=t % 2)

    def scatter_wait(par):
        pltpu.make_async_copy(obuf.at[par], y_hbm.at[pl.ds(0, bm * SLAB), :], ssem.at[par]).wait()

    def w_rows(f):
        hint = (lambda v, m: v) if isinstance(f, int) else pl.multiple_of
        return (pl.ds(hint(f * kch, kch), kch), pl.ds(hint(f * fch, fch), fch))

    def w_copies(e, f, i):
        r13, r2 = w_rows(f)
        return (pltpu.make_async_copy(w1_hbm.at[e, r13, :], st1.at[i], wsem.at[i, 0]),
                pltpu.make_async_copy(w3_hbm.at[e, r13, :], st3.at[i], wsem.at[i, 1]),
                pltpu.make_async_copy(w2_hbm.at[e, r2, :], st2.at[i], wsem.at[i, 2]))

    def w_start(e, f, i):
        for cp in w_copies(e, f, i):
            cp.start(priority=1)

    def w_wait(e, f, i):
        for cp in w_copies(e, f, i):
            cp.wait()

    def w_cast(s, f, i):
        r13, r2 = w_rows(f)
        w1b[s, r13, :] = st1[i].astype(BF16)
        w3b[s, r13, :] = st3[i].astype(BF16)
        w2b[s, r2, :] = st2[i].astype(BF16)

    def w_plan(blk):
        k = kidx_ref[blk]
        nk = nk_ref[blk]
        g_lo = (k * nf) // nk
        return nxt_ref[blk], g_lo, jnp.where(nxt_ref[blk] < N_EXPERTS, ((k + 1) * nf) // nk - g_lo, 0)

    @pl.when(b < nb)
    def _():
        par = lax.rem(b, 2)
        e = be_ref[b]
        s = slot_ref[b]
        e_next, g_lo, n_groups = w_plan(b)

        @pl.when(b == 0)
        def _():
            obuf[0] = jnp.zeros(obuf.shape[1:], obuf.dtype)
            zero = [pltpu.make_async_copy(obuf.at[0], y_hbm.at[pl.ds(pad_rows + i * bm * SLAB, bm * SLAB), :], zsem)
                    for i in range(2)]
            for cp in zero:
                cp.start()
            gather_start(0, 0)
            for f in range(nf):
                w_start(e, f, f % 2)
                w_wait(e, f, f % 2)
                w_cast(s, f, f % 2)
            for i in range(2):
                pend[2 * i] = s
                pend[2 * i + 1] = nf - 2 + i
            for cp in zero:
                cp.wait()

        gather_wait(par)

        @pl.when(b >= 2)
        def _():
            scatter_wait(par)

        e_prev, g_prev, n_prev = w_plan(jnp.maximum(b - 1, 0))

        @pl.when((b >= 1) & (n_prev == 1))
        def _():
            w_wait(e_prev, g_prev, 1 - par)

        @pl.when(n_groups > 0)
        def _():
            w_start(e_next, g_lo, par)

        gather_start(jnp.minimum(b + 1, nb - 1), 1 - par)
        w_cast(pend[2 * (1 - par)], pend[2 * (1 - par) + 1], 1 - par)
        x = _unpack_bf16(_slab_load(xbuf, (par,), bm, d // 2)).astype(BF16)
        hidden = (_silu(_dot(x, w1b[s])) * _dot(x, w3b[s])).astype(BF16)
        _slab_store(obuf, (par,), _pack_bf16(_dot(hidden, w2b[s])))
        scatter_start(b, par)

        @pl.when(n_groups > 0)
        def _():
            pend[2 * par] = 1 - s
            pend[2 * par + 1] = g_lo + n_groups - 1

        @pl.when(n_groups > 1)
        def _():
            w_wait(e_next, g_lo, par)
            w_cast(1 - s, g_lo, par)

            def more(f, carry):
                w_start(e_next, f, par)
                w_wait(e_next, f, par)
                w_cast(1 - s, f, par)
                return carry

            lax.fori_loop(g_lo + 1, g_lo + n_groups, more, 0)

        @pl.when(b == nb - 1)
        def _():
            gather_wait(1 - par)
            scatter_wait(par)

            @pl.when(nb >= 2)
            def _():
                scatter_wait(1 - par)


def _moe(tables, u2, w1, w3, w2, max_blocks, n_out_rows):
    d = w1.shape[1]
    de = w1.shape[2]
    bm = MOE_BLOCK
    nf = MOE_WGROUPS
    any_spec = pl.BlockSpec(memory_space=pl.ANY)
    return pl.pallas_call(
        _moe_kernel,
        grid_spec=pltpu.PrefetchScalarGridSpec(
            num_scalar_prefetch=len(tables),
            grid=(max_blocks,),
            in_specs=[any_spec] * 4,
            out_specs=any_spec,
            scratch_shapes=[pltpu.VMEM((2,) + _slab_shape(bm, d), U32), pltpu.VMEM((2,) + _slab_shape(bm, d), U32),
                            pltpu.VMEM((2, d, de), BF16), pltpu.VMEM((2, d, de), BF16),
                            pltpu.VMEM((2, de, d), BF16),
                            pltpu.VMEM((2, d // nf, de), F32), pltpu.VMEM((2, d // nf, de), F32),
                            pltpu.VMEM((2, de // nf, d), F32), pltpu.SMEM((4,), jnp.int32),
                            pltpu.SemaphoreType.DMA((2,)), pltpu.SemaphoreType.DMA((2,)),
                            pltpu.SemaphoreType.DMA((2, 3)), pltpu.SemaphoreType.DMA(())]),
        out_shape=jax.ShapeDtypeStruct(_slab_shape(n_out_rows, d), U32),
        compiler_params=_cparams(("arbitrary",)),
        name="moe",
    )(*tables, u2, w1, w3, w2)


def _combine_kernel(x1_ref, route_ref, y0_ref, y1_ref, g2_ref, lg_ref, lb_ref, o_ref):
    route = route_ref[...]
    tm, d = x1_ref.shape
    y0 = _unpack_bf16(_slab_load(y0_ref, (), tm, d // 2))
    y1 = _unpack_bf16(_slab_load(y1_ref, (), tm, d // 2))
    f = route[:, 2:3] * y0 + route[:, 3:4] * y1
    o_ref[...] = _ln(ALPHA * x1_ref[...] + g2_ref[...] * f) * lg_ref[...] + lb_ref[...]


def _combine(x1, route, y2, g2, lg, lb, tm):
    n, d = x1.shape
    vec = pl.BlockSpec((1, d), lambda i: (0, 0))
    nt = n // tm
    return pl.pallas_call(
        _combine_kernel,
        grid=(nt,),
        in_specs=[pl.BlockSpec((tm, d), lambda i: (i, 0)),
                  pl.BlockSpec((tm, LANES), lambda i: (i, 0)),
                  pl.BlockSpec(_slab_shape(tm, d), lambda i: (i, 0)),
                  pl.BlockSpec(_slab_shape(tm, d), lambda i: (i + nt, 0)),
                  vec, vec, vec],
        out_specs=pl.BlockSpec((tm, d), lambda i: (i, 0)),
        out_shape=jax.ShapeDtypeStruct((n, d), F32),
        compiler_params=_cparams(("arbitrary",)),
        name="combine",
    )(x1, route, y2, y2, g2, lg, lb)


def _moe_tables(counts, nblk_e, blk_end, dest1, dest2, n, max_blocks):
    bm = MOE_BLOCK
    i32 = jnp.int32
    blk = jnp.arange(max_blocks)
    be = jnp.minimum(jnp.sum(blk[:, None] >= blk_end[None, :], axis=1), N_EXPERTS - 1)
    kidx = blk - (blk_end - nblk_e)[be]
    nk = jnp.maximum(nblk_e[be], 1)
    live = nblk_e > 0
    slot = ((jnp.cumsum(live) - 1)[be]) % 2
    first_live_from = jnp.flip(lax.cummin(jnp.flip(jnp.where(live, jnp.arange(N_EXPERTS), N_EXPERTS))))
    nxt = jnp.concatenate([first_live_from[1:], jnp.full((1,), N_EXPERTS)])[be]
    ubase = (jnp.cumsum(counts) - counts)[be] + kidx * bm
    nvalid = jnp.clip(counts[be] - kidx * bm, 0, bm)
    nb = blk_end[-1:]
    j = jnp.arange(2 * n, dtype=i32)
    _, stok, sdst = lax.sort((jnp.concatenate([dest1, dest2]), (j % n) * SLAB, j * SLAB), num_keys=1)
    tail = jnp.zeros((bm,), i32)
    stok = jnp.concatenate([stok, tail])
    sdst = jnp.concatenate([sdst, tail])
    return [t.astype(i32) for t in (be, slot, kidx, nk, nxt, ubase, nvalid, nb, stok, sdst)]


def _layer(x, ctx, c, c_ctx, w_ada, b_ada, w_in, b_in, w_s, b_s, sgu_g, sgu_b, conv_w, conv_b,
           w_q, w_k, mh_g, skip, p_a, p_b, w_o, ln1_g, ln1_b, w_rg, b_rg, w_re, b_re,
           w1, w3, w2, ln2_g, ln2_b):
    n, d = x.shape
    w = N_HEADS * HEAD_DIM
    nz = 9 * w
    H = N_HEADS
    row = lambda a: a.reshape(1, -1)

    cc = jnp.zeros((8, d), F32).at[0].set(c[0]).at[1].set(c_ctx)
    mod = _ada(cc, w_ada, row(b_ada))
    sh1, sc1, g1, sh2, sc2, g2 = [mod[0:1, i * d:(i + 1) * d] for i in range(6)]
    sh1c, sc1c = mod[1:2, 0:d], mod[1:2, d:2 * d]

    w_t = w_in.T
    wg = _gate_weights(w_t, 4 * H)
    bg2 = jnp.pad(row(b_in)[:, nz:], ((0, 0), (0, LANES - 4 * H)))
    w_main = w_t.astype(BF16)
    b_main = row(b_in)

    z, gt = _in_proj(x, sc1, sh1, w_main, b_main, wg, bg2, tm=min(n, 1024))
    zc, gct = _in_proj(ctx, sc1c, sh1c, w_main, b_main, wg, bg2, tm=ctx.shape[0])

    cw, cb = conv_w, row(conv_b)
    wq, wk = w_q.astype(BF16), w_k.astype(BF16)
    xc, q, k = _conv_qk(z, 2, cw, cb, wq, wk, tm=min(n, 512))
    _, qc, kc = _conv_qk(zc, 2, cw, cb, wq, wk, tm=ctx.shape[0])

    zero = (jnp.zeros((H, HEAD_DIM, HEAD_DIM), F32), jnp.zeros((H, HEAD_DIM), F32),
            jnp.full((H, LANES), NEG, F32))
    rows_c, amat_c = _gate_prep(gct, tg=ctx.shape[0])
    rows_x, amat_x = _gate_prep(gt, tg=min(n, 1024))
    st_f = _mlstm("state", False, qc, kc, zc, 3, rows_c, amat_c, zero)
    st_b = _mlstm("state", True, qc, kc, zc, 3, rows_c, amat_c, zero)
    hb = _mlstm("h", True, q, k, z, 3, rows_x, amat_x, st_b)
    yb = _mlstm("out", False, q, k, z, 3, rows_x, amat_x, st_f, extra=(hb, xc, 4, row(mh_g), row(skip)))

    bs_full = jnp.repeat(b_s.T, HEAD_DIM, axis=1)
    wr = jnp.zeros((d, LANES), F32).at[:, :N_GROUPS].set(w_rg).at[:, N_GROUPS:N_GROUPS + N_EXPERTS].set(w_re)
    br = jnp.zeros((1, LANES), F32).at[0, :N_GROUPS].set(b_rg).at[0, N_GROUPS:N_GROUPS + N_EXPERTS].set(b_re)
    x1, u2, route, route_t, cnt = _merge(z, yb, x, w_s.astype(BF16), bs_full, row(sgu_g), row(sgu_b),
                                         p_a.astype(BF16), p_b.astype(BF16), w_o.astype(BF16),
                                         g1, row(ln1_g), row(ln1_b), sc2, sh2, wr.astype(BF16), br, tm=min(n, 256))

    bm = MOE_BLOCK
    counts = cnt[0, :N_EXPERTS].astype(jnp.int32)
    nblk_e = (counts + bm - 1) // bm
    blk_end = jnp.cumsum(nblk_e)
    row_start = (blk_end - nblk_e) * bm
    eid = jnp.arange(N_EXPERTS, dtype=F32)[:, None]

    def dest_rows(e, rank):
        start = jnp.sum(jnp.where(e[None, :] == eid, row_start[:, None], 0), axis=0)
        return start + rank.astype(jnp.int32)

    dest1 = dest_rows(route_t[0], route_t[4])
    dest2 = dest_rows(route_t[1], route_t[5])
    max_blocks = (2 * n + N_EXPERTS * (bm - 1)) // bm
    tables = _moe_tables(counts, nblk_e, blk_end, dest1, dest2, n, max_blocks)
    y2 = _moe(tables, u2, w1, w3, w2, max_blocks, 2 * n + 2 * bm)
    return _combine(x1, route, y2, g2, row(ln2_g), row(ln2_b), tm=min(n, 512))


def kernel(x, c, ctx, c_ctx, w_ada, b_ada, w_in, b_in, w_s, b_s, sgu_g, sgu_b, conv_w, conv_b, w_q, w_k, mh_g, skip, p_a, p_b, w_o, ln1_g, ln1_b, w_rg, b_rg, w_re, b_re, w1, w3, w2, ln2_g, ln2_b):
    assert x.shape[0] == 1 and w_ada.shape[0] == DEPTH == 1
    out = _layer(x[0], ctx[0], c, c_ctx, w_ada[0], b_ada[0], w_in[0], b_in[0], w_s[0], b_s[0],
                 sgu_g[0], sgu_b[0], conv_w[0], conv_b[0], w_q[0], w_k[0], mh_g[0], skip[0],
                 p_a[0], p_b[0], w_o[0], ln1_g[0], ln1_b[0], w_rg[0], b_rg[0], w_re[0], b_re[0],
                 w1[0], w3[0], w2[0], ln2_g[0], ln2_b[0])
    return out[None]
```

```python
import functools

import jax
import jax.numpy as jnp
from jax import lax
from jax.experimental import pallas as pl
from jax.experimental.pallas import tpu as pltpu

F32 = jnp.float32
BF16 = jnp.bfloat16

CHUNK = 128
N_HEADS = 8
HEAD_DIM = 128
N_GROUPS = 4
EXP_PER_GROUP = 8
N_EXPERTS = N_GROUPS * EXP_PER_GROUP
LN_EPS = 1e-5
NEG = -1e30
DEPTH = 1
ALPHA = (2 * DEPTH) ** 0.25
LANES = 128
VMEM_LIMIT = 56 * 1024 * 1024

INPROJ_TN = 1536
MERGE_SUB = 128
MOE_BLOCK = 256
MOE_WGROUPS = 4


def _cparams(sem):
    return pltpu.CompilerParams(dimension_semantics=sem, vmem_limit_bytes=VMEM_LIMIT)


def _ln(x):
    mu = jnp.mean(x, axis=-1, keepdims=True)
    xc = x - mu
    var = jnp.mean(xc * xc, axis=-1, keepdims=True)
    return xc * lax.rsqrt(var + LN_EPS)


def _sigmoid(x):
    return 1.0 / (1.0 + jnp.exp(-x))


def _silu(x):
    return x * _sigmoid(x)


def _gelu_tanh(x):
    return 0.5 * x * (1.0 + jnp.tanh(0.7978845608028654 * (x + 0.044715 * (x * x * x))))


def _log_sigmoid(x):
    return jnp.minimum(x, 0.0) - jnp.log(1.0 + jnp.exp(-jnp.abs(x)))


def _dot(a, b):
    return jnp.dot(a, b, preferred_element_type=F32)


def _dot_nt(a, b):
    return lax.dot_general(a, b, (((1,), (1,)), ((), ())), preferred_element_type=F32)


def _split2(a):
    hi = a.astype(BF16)
    lo = (a - hi.astype(F32)).astype(BF16)
    return hi, lo


def _split3(a):
    hi = a.astype(BF16)
    r = a - hi.astype(F32)
    mid = r.astype(BF16)
    lo = (r - mid.astype(F32)).astype(BF16)
    return hi, mid, lo


SLAB = 8
U32 = jnp.uint32


def _pack_bf16(v):
    half = v.shape[1] // 2
    lo = lax.bitcast_convert_type(v[:, :half].astype(BF16).astype(F32), U32)
    hi = lax.bitcast_convert_type(v[:, half:].astype(BF16).astype(F32), U32)
    return (lo >> 16) | (hi & jnp.uint32(0xFFFF0000))


def _unpack_bf16(w):
    lo = lax.bitcast_convert_type(w << 16, F32)
    hi = lax.bitcast_convert_type(w & jnp.uint32(0xFFFF0000), F32)
    return jnp.concatenate([lo, hi], axis=1)


def _slab_shape(rows, d):
    return (rows * SLAB, d // 2 // SLAB)


def _slab_store(ref, lead, x, r0=0):
    rows, d = x.shape
    per = d // SLAB // LANES
    for c in range(d // LANES):
        ref[lead + (pl.ds(r0 * SLAB + c // per, rows, stride=SLAB),
                    slice((c % per) * LANES, (c % per + 1) * LANES))] = x[:, c * LANES:(c + 1) * LANES]


def _interleave(gens):
    live = list(gens)
    while live:
        for g in list(live):
            try:
                next(g)
            except StopIteration:
                live.remove(g)


def _slab_load(ref, lead, rows, d):
    per = d // SLAB // LANES
    return jnp.concatenate(
        [ref[lead + (pl.ds(c // per, rows, stride=SLAB), slice((c % per) * LANES, (c % per + 1) * LANES))]
         for c in range(d // LANES)], axis=1)


def _ada_kernel(c_ref, w_ref, b_ref, o_ref):
    s_hi, s_lo = _split2(_silu(c_ref[...]))
    w_hi, w_lo = _split2(w_ref[...])
    o_ref[...] = _dot(s_hi, w_hi) + _dot(s_lo, w_hi) + _dot(s_hi, w_lo) + b_ref[...]


def _ada(cc, w, b):
    d, n = w.shape
    tn = 1024
    return pl.pallas_call(
        _ada_kernel,
        grid=(n // tn,),
        in_specs=[pl.BlockSpec((8, d), lambda j: (0, 0)),
                  pl.BlockSpec((d, tn), lambda j: (0, j)),
                  pl.BlockSpec((1, tn), lambda j: (0, j))],
        out_specs=pl.BlockSpec((8, tn), lambda j: (0, j)),
        out_shape=jax.ShapeDtypeStruct((8, n), F32),
        compiler_params=_cparams(("arbitrary",)),
        name="ada",
    )(cc, w, b)


def _gate_weights_kernel(wt_hbm, o_ref, buf, sem):
    ng = buf.shape[0]
    cp = pltpu.make_async_copy(wt_hbm.at[pl.ds(wt_hbm.shape[0] - ng, ng), :], buf, sem)
    cp.start()
    cp.wait()
    wg = jnp.concatenate([buf[...], jnp.zeros((LANES - ng, buf.shape[1]), F32)], axis=0)
    hi, lo = _split2(wg)
    o_ref[...] = jnp.concatenate([hi, lo], axis=0)


def _gate_weights(wt, ng):
    d = wt.shape[1]
    return pl.pallas_call(
        _gate_weights_kernel,
        in_specs=[pl.BlockSpec(memory_space=pl.ANY)],
        out_specs=pl.BlockSpec((2 * LANES, d), lambda: (0, 0)),
        out_shape=jax.ShapeDtypeStruct((2 * LANES, d), BF16),
        scratch_shapes=[pltpu.VMEM((ng, d), F32), pltpu.SemaphoreType.DMA(())],
        compiler_params=pltpu.CompilerParams(vmem_limit_bytes=VMEM_LIMIT),
        name="gate_weights",
    )(wt)


def _inproj_kernel(x_ref, sc_ref, sh_ref, w_ref, b_ref, wg_ref, bg_ref, z_ref, g_ref, u_scr):
    @pl.when(pl.program_id(1) == 0)
    def _():
        u = _ln(x_ref[...]) * (1.0 + sc_ref[...]) + sh_ref[...]
        u_hi, u_lo = _split2(u)
        u_scr[...] = u_hi
        wg = wg_ref[...]
        p = _dot_nt(u_hi, wg)
        g = p[:, :LANES] + p[:, LANES:] + _dot_nt(u_lo, wg[:LANES, :]) + bg_ref[...]
        g_ref[...] = g.T[:g_ref.shape[0], :]

    z_ref[...] = (_dot_nt(u_scr[...], w_ref[...]) + b_ref[...]).astype(z_ref.dtype)


def _in_proj(x, sc, sh, wt, b, wg, bg, tm):
    n, d = x.shape
    tn = INPROJ_TN
    nz = wt.shape[0] // tn * tn
    ng = bg.shape[1]
    return pl.pallas_call(
        _inproj_kernel,
        grid=(n // tm, nz // tn),
        in_specs=[pl.BlockSpec((tm, d), lambda i, j: (i, 0)),
                  pl.BlockSpec((1, d), lambda i, j: (0, 0)),
                  pl.BlockSpec((1, d), lambda i, j: (0, 0)),
                  pl.BlockSpec((tn, d), lambda i, j: (j, 0)),
                  pl.BlockSpec((1, tn), lambda i, j: (0, j)),
                  pl.BlockSpec((2 * ng, d), lambda i, j: (0, 0)),
                  pl.BlockSpec((1, ng), lambda i, j: (0, 0))],
        out_specs=[pl.BlockSpec((tm, tn), lambda i, j: (i, j)),
                   pl.BlockSpec((4 * N_HEADS, tm), lambda i, j: (0, i))],
        out_shape=[jax.ShapeDtypeStruct((n, nz), BF16),
                   jax.ShapeDtypeStruct((4 * N_HEADS, n), F32)],
        scratch_shapes=[pltpu.VMEM((tm, d), BF16)],
        compiler_params=_cparams(("arbitrary", "arbitrary")),
        name="in_proj",
    )(x, sc, sh, wt, b, wg, bg)


def _convqk_kernel(xm_ref, prev_ref, next_ref, cw_ref, cb_ref, wq_ref, wk_ref,
                   xc_ref, q_ref, k_ref):
    i = pl.program_id(0)
    last = pl.num_programs(0) - 1
    tm = xm_ref.shape[0]
    xm = xm_ref[...].astype(F32)
    prev_row = jnp.where(i == 0, 0.0, prev_ref[...].astype(F32)[-1:, :])
    next_row = jnp.where(i == last, 0.0, next_ref[...].astype(F32)[:1, :])
    row = lax.broadcasted_iota(jnp.int32, xm.shape, 0)
    x_prev = jnp.where(row == 0, prev_row, pltpu.roll(xm, 1, 0))
    x_next = jnp.where(row == tm - 1, next_row, pltpu.roll(xm, tm - 1, 0))
    cw = cw_ref[...]
    y = cw[0:1, :] * x_prev + cw[1:2, :] * xm + cw[2:3, :] * x_next + cb_ref[...]
    xc = _silu(y).astype(BF16)
    xc_ref[...] = xc
    for h in range(N_HEADS):
        sl = slice(h * HEAD_DIM, (h + 1) * HEAD_DIM)
        q_ref[:, sl] = _dot(xc[:, sl], wq_ref[h]).astype(BF16)
        k_ref[:, sl] = (_dot(xc[:, sl], wk_ref[h]) * (HEAD_DIM ** -0.5)).astype(BF16)


def _conv_qk(z, xm_blk, cw, cb, wq, wk, tm):
    n = z.shape[0]
    w = N_HEADS * HEAD_DIM
    halo = 16
    nb = n // halo
    per = tm // halo
    out = jax.ShapeDtypeStruct((n, w), BF16)
    return pl.pallas_call(
        _convqk_kernel,
        grid=(n // tm,),
        in_specs=[pl.BlockSpec((tm, w), lambda i: (i, xm_blk)),
                  pl.BlockSpec((halo, w), lambda i: (jnp.maximum(i * per - 1, 0), xm_blk)),
                  pl.BlockSpec((halo, w), lambda i: (jnp.minimum((i + 1) * per, nb - 1), xm_blk)),
                  pl.BlockSpec((3, w), lambda i: (0, 0)),
                  pl.BlockSpec((1, w), lambda i: (0, 0)),
                  pl.BlockSpec((N_HEADS, HEAD_DIM, HEAD_DIM), lambda i: (0, 0, 0)),
                  pl.BlockSpec((N_HEADS, HEAD_DIM, HEAD_DIM), lambda i: (0, 0, 0))],
        out_specs=[pl.BlockSpec((tm, w), lambda i: (i, 0))] * 3,
        out_shape=[out, out, out],
        compiler_params=_cparams(("arbitrary",)),
        name="conv_qk",
    )(z, z, z, cw, cb, wq, wk)


def _dot3_right(a, t_bf16):
    hi, mid, lo = _split3(a)
    return _dot(hi, t_bf16) + _dot(mid, t_bf16) + _dot(lo, t_bf16)


def _gate_prep_kernel(gt_ref, rows_ref, amat_ref):
    L, H = CHUNK, N_HEADS
    ri = lax.broadcasted_iota(jnp.int32, (L, L), 0)
    ci = lax.broadcasted_iota(jnp.int32, (L, L), 1)
    eye = (ri == ci).astype(BF16)
    ones8 = jnp.ones((H, L), F32)
    pad = jnp.zeros((LANES - 4 * H, L), F32)
    lane = lax.broadcasted_iota(jnp.int32, (H, L), 1)

    def chunk(d, c):
        reverse = bool(d)
        lanes = slice(c * L, (c + 1) * L)
        tri = ((ri >= ci) if reverse else (ri <= ci)).astype(BF16)
        li = gt_ref[2 * d * H:(2 * d + 1) * H, lanes]
        lf = _log_sigmoid(gt_ref[(2 * d + 1) * H:(2 * d + 2) * H, lanes])
        b = _dot3_right(lf, tri)
        yield
        r = li - b
        terms = jnp.concatenate([t.astype(F32) for t in _split3(r)] + [ones8, pad], axis=0).astype(BF16)
        amat_ref[d, lanes, :] = _dot_nt(eye, terms).astype(BF16)
        big_r = r
        s = 1
        while s < L:
            yield
            if reverse:
                big_r = jnp.maximum(big_r, jnp.where(lane < L - s, pltpu.roll(big_r, L - s, 1), NEG))
            else:
                big_r = jnp.maximum(big_r, jnp.where(lane >= s, pltpu.roll(big_r, s, 1), NEG))
            s *= 2
        rows_ref[d, :, lanes] = jnp.concatenate([b, r, big_r], axis=0)

    _interleave([chunk(d, c) for d in (0, 1) for c in range(gt_ref.shape[1] // L)])


def _gate_prep(gt, tg):
    n = gt.shape[1]
    return pl.pallas_call(
        _gate_prep_kernel,
        grid=(n // tg,),
        in_specs=[pl.BlockSpec((4 * N_HEADS, tg), lambda i: (0, i))],
        out_specs=[pl.BlockSpec((2, 3 * N_HEADS, tg), lambda i: (0, 0, i)),
                   pl.BlockSpec((2, tg, LANES), lambda i: (0, i, 0))],
        out_shape=[jax.ShapeDtypeStruct((2, 3 * N_HEADS, n), F32),
                   jax.ShapeDtypeStruct((2, n, LANES), BF16)],
        compiler_params=_cparams(("arbitrary",)),
        name="gate_prep",
    )(gt)


def _mlstm_kernel(reverse, mode, *refs):
    q_ref, k_ref, v_ref, rows_ref, amat_ref, c0_ref, n0_ref, m0_ref = refs[:8]
    rest = refs[8:]
    if mode == "state":
        c_out, n_out, m_out, c_scr, n_scr, m_scr = rest
    elif mode == "h":
        h_out, c_scr, n_scr, m_scr = rest
    else:
        hb_ref, xc_ref, ob_ref, mhg_ref, skip_ref, y_out, c_scr, n_scr, m_scr = rest

    @pl.when(pl.program_id(0) == 0)
    def _():
        c_scr[...] = c0_ref[...]
        n_scr[...] = n0_ref[...]
        m_scr[...] = m0_ref[...]

    L, H = CHUNK, N_HEADS
    ri = lax.broadcasted_iota(jnp.int32, (L, L), 0)
    ci = lax.broadcasted_iota(jnp.int32, (L, L), 1)
    seen_t = (ri >= ci) if reverse else (ri <= ci)
    last = 0 if reverse else L - 1

    b = rows_ref[0:H, :]
    r = rows_ref[H:2 * H, :]
    big_r = rows_ref[2 * H:3 * H, :]
    m = m_scr[...]
    r_last = big_r[:, last:last + 1]
    big_m = jnp.maximum(big_r, m)
    a = jnp.exp(m - big_m)
    sc = jnp.exp(big_r - big_m)
    floor = jnp.exp(-(b + big_m))
    m_last = jnp.maximum(r_last, m)
    d1 = jnp.exp(m - m_last)
    d2 = jnp.exp(r_last - m_last)
    wk = jnp.exp(r - r_last)
    m_scr[...] = b[:, last:last + 1] + m_last

    a_mat = amat_ref[:, 0:4 * H]
    nr_terms = [t.astype(F32) for t in _split3(-big_r)]
    sub = lax.broadcasted_iota(jnp.int32, (H, L), 0)
    ones16 = jnp.ones((16, L), BF16)

    def head(h):
        sl = slice(h * HEAD_DIM, (h + 1) * HEAD_DIM)
        row = slice(h, h + 1)
        qh = q_ref[:, sl]
        kh = k_ref[:, sl]
        vt = v_ref[:, sl].T
        sel = (sub == h).astype(F32)
        dyn = jnp.where(sub == 0, nr_terms[0][row], jnp.where(sub == 1, nr_terms[1][row],
                        jnp.where(sub == 2, nr_terms[2][row], 0.0)))
        b_mat = jnp.concatenate([sel, sel, sel, dyn], axis=0).astype(BF16)
        arg = _dot(a_mat, b_mat)
        kq = _dot_nt(kh, qh)
        yield
        st = (kq * jnp.exp(jnp.where(seen_t, arg, NEG))).astype(BF16)
        c_prev = c_scr[h]
        n_prev = n_scr[row, :]
        wk16 = jnp.broadcast_to(wk[row], (16, L)).astype(BF16)
        vtw = (vt.astype(F32) * wk[row]).astype(BF16)
        upd = _dot(jnp.concatenate([vtw, wk16], axis=0), kh)
        if mode != "state":
            n16 = jnp.broadcast_to(n_prev, (16, HEAD_DIM)).astype(BF16)
            intra = _dot(jnp.concatenate([vt, ones16], axis=0), st)
            inter = _dot_nt(jnp.concatenate([c_prev.astype(BF16), n16], axis=0), qh)
        yield
        c_scr[h] = d1[row] * c_prev + d2[row] * upd[:L]
        n_scr[row, :] = d1[row] * n_prev + d2[row] * upd[L:L + 1]
        if mode != "state":
            num = a[row] * inter[:L] + sc[row] * intra[:L]
            den = a[row] * inter[L:L + 1] + sc[row] * intra[L:L + 1]
            ht = num * (1.0 / jnp.maximum(jnp.abs(den), floor[row]))
            if mode == "h":
                h_out[h] = ht.astype(h_out.dtype)
            else:
                hs = ht + hb_ref[h].astype(F32)
                mu = jnp.mean(hs, axis=0, keepdims=True)
                hc = hs - mu
                var = jnp.mean(hc * hc, axis=0, keepdims=True)
                hn = (hc * lax.rsqrt(var + LN_EPS)).T
                y = _sigmoid(ob_ref[:, sl].astype(F32)) * (
                    hn * mhg_ref[:, sl] + skip_ref[:, sl] * xc_ref[:, sl].astype(F32))
                y_out[:, sl] = y.astype(y_out.dtype)
        yield

    _interleave([head(h) for h in range(H)])

    if mode == "state":
        c_out[...] = c_scr[...]
        n_out[...] = n_scr[...]
        m_out[...] = m_scr[...]


def _mlstm(mode, reverse, q, k, z, v_blk, rows, amat, state, extra=()):
    n = q.shape[0]
    nc = n // CHUNK
    w = N_HEADS * HEAD_DIM
    c0, n0, m0 = state
    d = int(reverse)
    pos = (lambda c: nc - 1 - c) if reverse else (lambda c: c)
    row = lambda blk: pl.BlockSpec((CHUNK, w), lambda c: (pos(c), blk))
    full = lambda a: pl.BlockSpec(a.shape, lambda c: (0,) * a.ndim)
    ht_spec = pl.BlockSpec((None, N_HEADS, HEAD_DIM, CHUNK), lambda c: (pos(c), 0, 0, 0))
    in_specs = [row(0), row(0), row(v_blk),
                pl.BlockSpec((None, 3 * N_HEADS, CHUNK), lambda c: (d, 0, pos(c))),
                pl.BlockSpec((None, CHUNK, LANES), lambda c: (d, pos(c), 0)),
                full(c0), full(n0), full(m0)]
    args = [q, k, z, rows, amat, c0, n0, m0]
    scratch = [pltpu.VMEM(c0.shape, F32), pltpu.VMEM(n0.shape, F32), pltpu.VMEM(m0.shape, F32)]
    if mode == "state":
        out_specs = [full(c0), full(n0), full(m0)]
        out_shape = [jax.ShapeDtypeStruct(a.shape, F32) for a in state]
    elif mode == "h":
        out_specs = ht_spec
        out_shape = jax.ShapeDtypeStruct((nc, N_HEADS, HEAD_DIM, CHUNK), BF16)
    else:
        out_specs = row(0)
        out_shape = jax.ShapeDtypeStruct((n, w), BF16)
        hb, xc, ob_blk, mhg, skip = extra
        in_specs += [ht_spec, row(0), row(ob_blk), full(mhg), full(skip)]
        args += [hb, xc, z, mhg, skip]
    return pl.pallas_call(
        functools.partial(_mlstm_kernel, reverse, mode),
        grid=(nc,),
        in_specs=in_specs, out_specs=out_specs, out_shape=out_shape,
        scratch_shapes=scratch,
        compiler_params=_cparams(("arbitrary",)),
        name="mlstm_%s_%s" % (mode, "bwd" if reverse else "fwd"),
    )(*args)


def _merge_kernel(ua_ref, va_ref, yb_ref, ga0_ref, ga1_ref, gb0_ref, gb1_ref, x_ref,
                  ws_ref, bs_ref, sg_ref, sb_ref, pa_ref, pb_ref, wo_ref,
                  g1_ref, l1g_ref, l1b_ref, sc2_ref, sh2_ref, wr_ref, br_ref,
                  x1_ref, u2_ref, route_ref, route_t_ref, cnt_ref, a_scr, run_scr):
    tm = x_ref.shape[0]
    sub = MERGE_SUB

    @pl.when(pl.program_id(0) == 0)
    def _():
        run_scr[...] = jnp.zeros_like(run_scr)

    run = [run_scr[0:1, :]]

    def rows_of(r0):
        rs = slice(r0, r0 + sub)
        vn = (_ln(_gelu_tanh(va_ref[rs, :].astype(F32))) * sg_ref[...] + sb_ref[...]).astype(BF16)
        for c in range(sub // CHUNK):
            rows = slice(c * CHUNK, (c + 1) * CHUNK)
            dst = slice(r0 + c * CHUNK, r0 + (c + 1) * CHUNK)
            for g in range(N_HEADS):
                cols = slice(g * HEAD_DIM, (g + 1) * HEAD_DIM)
                mixed = _dot(ws_ref[g], vn[rows, cols]) + bs_ref[:, cols]
                a_scr[dst, cols] = (_gelu_tanh(ua_ref[dst, cols].astype(F32)) * mixed).astype(BF16)
        yield
        pa = _dot(a_scr[rs, :], pa_ref[...])
        pb = _dot(yb_ref[rs, :], pb_ref[...])
        yield
        ga = jnp.concatenate([ga0_ref[rs, :], ga1_ref[rs, :]], axis=1).astype(F32)
        gb = jnp.concatenate([gb0_ref[rs, :], gb1_ref[rs, :]], axis=1).astype(F32)
        mrg = (_sigmoid(ga) * pa + _sigmoid(gb) * pb).astype(BF16)
        yield
        y = _dot(mrg, wo_ref[...])
        yield
        x1 = _ln(ALPHA * x_ref[rs, :] + g1_ref[...] * y) * l1g_ref[...] + l1b_ref[...]
        x1_ref[rs, :] = x1
        u2 = _ln(x1) * (1.0 + sc2_ref[...]) + sh2_ref[...]
        _slab_store(u2_ref, (), _pack_bf16(u2), r0)
        yield
        logit = _dot(u2.astype(BF16), wr_ref[...]) + br_ref[...]
        lane = lax.broadcasted_iota(jnp.int32, logit.shape, 1)
        lane_f = lane.astype(F32)
        is_g = lane < N_GROUPS
        gmax = jnp.max(jnp.where(is_g, logit, NEG), axis=-1, keepdims=True)
        g_sel = jnp.min(jnp.where(is_g & (logit == gmax), lane_f, 1e9), axis=-1, keepdims=True)
        p_g = 1.0 / jnp.sum(jnp.where(is_g, jnp.exp(logit - gmax), 0.0), axis=-1, keepdims=True)
        lo = N_GROUPS + EXP_PER_GROUP * g_sel
        in_grp = (lane_f >= lo) & (lane_f < lo + EXP_PER_GROUP)
        el = jnp.where(in_grp, logit, NEG)
        e1max = jnp.max(el, axis=-1, keepdims=True)
        l1 = jnp.min(jnp.where(in_grp & (el == e1max), lane_f, 1e9), axis=-1, keepdims=True)
        el2 = jnp.where(lane_f == l1, NEG, el)
        e2max = jnp.max(el2, axis=-1, keepdims=True)
        l2 = jnp.min(jnp.where(in_grp & (el2 == e2max), lane_f, 1e9), axis=-1, keepdims=True)
        zsum = jnp.sum(jnp.where(in_grp, jnp.exp(el - e1max), 0.0), axis=-1, keepdims=True)
        p1 = 1.0 / zsum
        p2 = jnp.exp(e2max - e1max) / zsum
        w1 = p_g * p1 / (p1 + p2)
        w2 = p_g * p2 / (p1 + p2)
        e1 = l1 - N_GROUPS
        e2 = l2 - N_GROUPS
        oh1 = (lane_f == e1).astype(BF16)
        oh2 = (lane_f == e2).astype(BF16)
        ri = lax.broadcasted_iota(jnp.int32, (sub, sub), 0)
        ci = lax.broadcasted_iota(jnp.int32, (sub, sub), 1)
        strict = (ci < ri).astype(BF16)
        cnt1 = jnp.sum(oh1.astype(F32), axis=0, keepdims=True)
        cnt2 = jnp.sum(oh2.astype(F32), axis=0, keepdims=True)
        pre1 = _dot(strict, oh1) + run[0]
        pre2 = _dot(strict, oh2) + run[0] + cnt1
        rank1 = jnp.sum(oh1.astype(F32) * pre1, axis=-1, keepdims=True)
        rank2 = jnp.sum(oh2.astype(F32) * pre2, axis=-1, keepdims=True)
        run[0] = run[0] + cnt1 + cnt2
        route = jnp.where(lane == 0, e1, 0.0)
        route = jnp.where(lane == 1, e2, route)
        route = jnp.where(lane == 2, w1, route)
        route = jnp.where(lane == 3, w2, route)
        route = jnp.where(lane == 4, rank1, route)
        route = jnp.where(lane == 5, rank2, route)
        route_ref[rs, :] = route
        route_t_ref[:, rs] = route.T[:route_t_ref.shape[0], :]
        yield

    _interleave([rows_of(r0) for r0 in range(0, tm, sub)])
    run_scr[...] = jnp.broadcast_to(run[0], run_scr.shape)
    cnt_ref[...] = jnp.broadcast_to(run[0], cnt_ref.shape)


def _merge(z, yb, x, ws, bs, sg, sb, pa, pb, wo, g1, l1g, l1b, sc2, sh2, wr, br, tm):
    n, d = x.shape
    w = N_HEADS * HEAD_DIM
    zc = lambda blk: pl.BlockSpec((tm, w), lambda i: (i, blk))
    full = lambda a: pl.BlockSpec(a.shape, lambda i: (0,) * a.ndim)
    consts = [ws, bs, sg, sb, pa, pb, wo, g1, l1g, l1b, sc2, sh2, wr, br]
    return pl.pallas_call(
        _merge_kernel,
        grid=(n // tm,),
        in_specs=[zc(0), zc(1), pl.BlockSpec((tm, w), lambda i: (i, 0)),
                  zc(5), zc(6), zc(7), zc(8),
                  pl.BlockSpec((tm, d), lambda i: (i, 0))] + [full(a) for a in consts],
        out_specs=[pl.BlockSpec((tm, d), lambda i: (i, 0)),
                   pl.BlockSpec(_slab_shape(tm, d), lambda i: (i, 0)),
                   pl.BlockSpec((tm, LANES), lambda i: (i, 0)),
                   pl.BlockSpec((8, tm), lambda i: (0, i)),
                   pl.BlockSpec((8, LANES), lambda i: (0, 0))],
        out_shape=[jax.ShapeDtypeStruct((n, d), F32),
                   jax.ShapeDtypeStruct(_slab_shape(n, d), U32),
                   jax.ShapeDtypeStruct((n, LANES), F32),
                   jax.ShapeDtypeStruct((8, n), F32),
                   jax.ShapeDtypeStruct((8, LANES), F32)],
        scratch_shapes=[pltpu.VMEM((tm, w), BF16), pltpu.VMEM((8, LANES), F32)],
        compiler_params=_cparams(("arbitrary",)),
        name="merge",
    )(z, z, yb, z, z, z, z, x, *consts)


def _moe_kernel(be_ref, slot_ref, kidx_ref, nk_ref, nxt_ref, ubase_ref, nvalid_ref, nb_ref, stok_ref, sdst_ref,
                u_hbm, w1_hbm, w3_hbm, w2_hbm, y_hbm,
                xbuf, obuf, w1b, w3b, w2b, st1, st3, st2, gsem, ssem, wsem, zsem):
    b = pl.program_id(0)
    nb = nb_ref[0]
    bm = xbuf.shape[1] // SLAB
    d = xbuf.shape[2] * SLAB * 2
    kch = st1.shape[1]
    fch = st2.shape[1]
    nf = w2b.shape[1] // fch
    pad_rows = y_hbm.shape[0] - 2 * bm * SLAB

    def gather_start(blk, par):
        base = ubase_ref[blk]
        for t in range(bm):
            row = pl.multiple_of(stok_ref[base + t], SLAB)
            pltpu.make_async_copy(u_hbm.at[pl.ds(row, SLAB), :],
                                  xbuf.at[par, pl.ds(t * SLAB, SLAB), :], gsem.at[par]).start()

    def gather_wait(par):
        pltpu.make_async_copy(u_hbm.at[pl.ds(0, bm * SLAB), :], xbuf.at[par], gsem.at[par]).wait()

    def scatter_start(blk, par):
        base = ubase_ref[blk]
        nvalid = nvalid_ref[blk]
        pad = pad_rows + par * (bm * SLAB)

        def issue(full):
            for t in range(bm):
                row = sdst_ref[base + t] if full else jnp.where(t < nvalid, sdst_ref[base + t], pad + t * SLAB)
                pltpu.make_async_copy(obuf.at[par, pl.ds(t * SLAB, SLAB), :],
                                      y_hbm.at[pl.ds(pl.multiple_of(row, SLAB), SLAB), :],
                                      ssem.at[par]).start(priority=t % 2)

        pl.when(nvalid == bm)(functools.partial(issue, True))
        pl.when(nvalid != bm)(functools.partial(issue, False))

    def scatter_wait(par):
        pltpu.make_async_copy(obuf.at[par], y_hbm.at[pl.ds(0, bm * SLAB), :], ssem.at[par]).wait()

    def w_rows(f):
        hint = (lambda v, m: v) if isinstance(f, int) else pl.multiple_of
        return (pl.ds(hint(f * kch, kch), kch), pl.ds(hint(f * fch, fch), fch))

    def w_copies(e, f, i):
        r13, r2 = w_rows(f)
        return (pltpu.make_async_copy(w1_hbm.at[e, r13, :], st1.at[i], wsem.at[i, 0]),
                pltpu.make_async_copy(w3_hbm.at[e, r13, :], st3.at[i], wsem.at[i, 1]),
                pltpu.make_async_copy(w2_hbm.at[e, r2, :], st2.at[i], wsem.at[i, 2]))

    def w_start(e, f, i):
        for cp in w_copies(e, f, i):
            cp.start(priority=1)

    def w_wait(e, f, i):
        for cp in w_copies(e, f, i):
            cp.wait()

    def w_cast(s, f, i, valid=None):
        r13, r2 = w_rows(f)
        for dst, rows, st in ((w1b, r13, st1), (w3b, r13, st3), (w2b, r2, st2)):
            new = st[i].astype(BF16)
            dst[s, rows, :] = new if valid is None else jnp.where(valid, new, dst[s, rows, :])

    def w_plan(blk):
        k = kidx_ref[blk]
        nk = nk_ref[blk]
        g_lo = (k * nf) // nk
        return nxt_ref[blk], g_lo, jnp.where(nxt_ref[blk] < N_EXPERTS, ((k + 1) * nf) // nk - g_lo, 0)

    @pl.when(b < nb)
    def _():
        par = lax.rem(b, 2)
        e = be_ref[b]
        s = slot_ref[b]
        e_next, g_lo, n_groups = w_plan(b)

        @pl.when(b == 0)
        def _():
            obuf[0] = jnp.zeros(obuf.shape[1:], obuf.dtype)
            zero = [pltpu.make_async_copy(obuf.at[0], y_hbm.at[pl.ds(pad_rows + i * bm * SLAB, bm * SLAB), :], zsem)
                    for i in range(2)]
            for cp in zero:
                cp.start()
            gather_start(0, 0)
            for f in range(nf):
                w_start(e, f, f % 2)
                w_wait(e, f, f % 2)
                w_cast(s, f, f % 2)
            for cp in zero:
                cp.wait()

        gather_wait(par)

        @pl.when(b >= 2)
        def _():
            scatter_wait(par)

        e_prev, g_prev, n_prev = w_plan(jnp.maximum(b - 1, 0))
        pending = (b >= 1) & (n_prev == 1)

        @pl.when(pending)
        def _():
            w_wait(e_prev, g_prev, 1 - par)

        @pl.when(n_groups > 0)
        def _():
            w_start(e_next, g_lo, par)

        gather_start(jnp.minimum(b + 1, nb - 1), 1 - par)
        w_cast(1 - slot_ref[jnp.maximum(b - 1, 0)], g_prev, 1 - par, valid=pending)
        x = _unpack_bf16(_slab_load(xbuf, (par,), bm, d // 2)).astype(BF16)
        hidden = (_silu(_dot(x, w1b[s])) * _dot(x, w3b[s])).astype(BF16)
        _slab_store(obuf, (par,), _pack_bf16(_dot(hidden, w2b[s])))
        scatter_start(b, par)

        @pl.when(n_groups > 1)
        def _():
            w_wait(e_next, g_lo, par)
            w_cast(1 - s, g_lo, par)

            def more(f, carry):
                w_start(e_next, f, par)
                w_wait(e_next, f, par)
                w_cast(1 - s, f, par)
                return carry

            lax.fori_loop(g_lo + 1, g_lo + n_groups, more, 0)

        @pl.when(b == nb - 1)
        def _():
            gather_wait(1 - par)
            scatter_wait(par)

            @pl.when(nb >= 2)
            def _():
                scatter_wait(1 - par)


def _moe(tables, u2, w1, w3, w2, max_blocks, n_out_rows):
    d = w1.shape[1]
    de = w1.shape[2]
    bm = MOE_BLOCK
    nf = MOE_WGROUPS
    any_spec = pl.BlockSpec(memory_space=pl.ANY)
    return pl.pallas_call(
        _moe_kernel,
        grid_spec=pltpu.PrefetchScalarGridSpec(
            num_scalar_prefetch=len(tables),
            grid=(max_blocks,),
            in_specs=[any_spec] * 4,
            out_specs=any_spec,
            scratch_shapes=[pltpu.VMEM((2,) + _slab_shape(bm, d), U32), pltpu.VMEM((2,) + _slab_shape(bm, d), U32),
                            pltpu.VMEM((2, d, de), BF16), pltpu.VMEM((2, d, de), BF16),
                            pltpu.VMEM((2, de, d), BF16),
                            pltpu.VMEM((2, d // nf, de), F32), pltpu.VMEM((2, d // nf, de), F32),
                            pltpu.VMEM((2, de // nf, d), F32),
                            pltpu.SemaphoreType.DMA((2,)), pltpu.SemaphoreType.DMA((2,)),
                            pltpu.SemaphoreType.DMA((2, 3)), pltpu.SemaphoreType.DMA(())]),
        out_shape=jax.ShapeDtypeStruct(_slab_shape(n_out_rows, d), U32),
        compiler_params=_cparams(("arbitrary",)),
        name="moe",
    )(*tables, u2, w1, w3, w2)


def _combine_kernel(x1_ref, route_ref, y0_ref, y1_ref, g2_ref, lg_ref, lb_ref, o_ref):
    route = route_ref[...]
    tm, d = x1_ref.shape
    y0 = _unpack_bf16(_slab_load(y0_ref, (), tm, d // 2))
    y1 = _unpack_bf16(_slab_load(y1_ref, (), tm, d // 2))
    f = route[:, 2:3] * y0 + route[:, 3:4] * y1
    o_ref[...] = _ln(ALPHA * x1_ref[...] + g2_ref[...] * f) * lg_ref[...] + lb_ref[...]


def _combine(x1, route, y2, g2, lg, lb, tm):
    n, d = x1.shape
    vec = pl.BlockSpec((1, d), lambda i: (0, 0))
    nt = n // tm
    return pl.pallas_call(
        _combine_kernel,
        grid=(nt,),
        in_specs=[pl.BlockSpec((tm, d), lambda i: (i, 0)),
                  pl.BlockSpec((tm, LANES), lambda i: (i, 0)),
                  pl.BlockSpec(_slab_shape(tm, d), lambda i: (i, 0)),
                  pl.BlockSpec(_slab_shape(tm, d), lambda i: (i + nt, 0)),
                  vec, vec, vec],
        out_specs=pl.BlockSpec((tm, d), lambda i: (i, 0)),
        out_shape=jax.ShapeDtypeStruct((n, d), F32),
        compiler_params=_cparams(("arbitrary",)),
        name="combine",
    )(x1, route, y2, y2, g2, lg, lb)


def _moe_tables(counts, nblk_e, blk_end, dest1, dest2, n, max_blocks):
    bm = MOE_BLOCK
    i32 = jnp.int32
    blk = jnp.arange(max_blocks)
    be = jnp.minimum(jnp.sum(blk[:, None] >= blk_end[None, :], axis=1), N_EXPERTS - 1)
    kidx = blk - (blk_end - nblk_e)[be]
    nk = jnp.maximum(nblk_e[be], 1)
    live = nblk_e > 0
    slot = ((jnp.cumsum(live) - 1)[be]) % 2
    first_live_from = jnp.flip(lax.cummin(jnp.flip(jnp.where(live, jnp.arange(N_EXPERTS), N_EXPERTS))))
    nxt = jnp.concatenate([first_live_from[1:], jnp.full((1,), N_EXPERTS)])[be]
    ubase = (jnp.cumsum(counts) - counts)[be] + kidx * bm
    nvalid = jnp.clip(counts[be] - kidx * bm, 0, bm)
    nb = blk_end[-1:]
    j = jnp.arange(2 * n, dtype=i32)
    _, stok, sdst = lax.sort((jnp.concatenate([dest1, dest2]), (j % n) * SLAB, j * SLAB), num_keys=1)
    tail = jnp.zeros((bm,), i32)
    stok = jnp.concatenate([stok, tail])
    sdst = jnp.concatenate([sdst, tail])
    return [t.astype(i32) for t in (be, slot, kidx, nk, nxt, ubase, nvalid, nb, stok, sdst)]


def _layer(x, ctx, c, c_ctx, w_ada, b_ada, w_in, b_in, w_s, b_s, sgu_g, sgu_b, conv_w, conv_b,
           w_q, w_k, mh_g, skip, p_a, p_b, w_o, ln1_g, ln1_b, w_rg, b_rg, w_re, b_re,
           w1, w3, w2, ln2_g, ln2_b):
    n, d = x.shape
    w = N_HEADS * HEAD_DIM
    nz = 9 * w
    H = N_HEADS
    row = lambda a: a.reshape(1, -1)

    cc = jnp.zeros((8, d), F32).at[0].set(c[0]).at[1].set(c_ctx)
    mod = _ada(cc, w_ada, row(b_ada))
    sh1, sc1, g1, sh2, sc2, g2 = [mod[0:1, i * d:(i + 1) * d] for i in range(6)]
    sh1c, sc1c = mod[1:2, 0:d], mod[1:2, d:2 * d]

    w_t = w_in.T
    wg = _gate_weights(w_t, 4 * H)
    bg2 = jnp.pad(row(b_in)[:, nz:], ((0, 0), (0, LANES - 4 * H)))
    w_main = w_t.astype(BF16)
    b_main = row(b_in)

    z, gt = _in_proj(x, sc1, sh1, w_main, b_main, wg, bg2, tm=min(n, 1024))
    zc, gct = _in_proj(ctx, sc1c, sh1c, w_main, b_main, wg, bg2, tm=ctx.shape[0])

    cw, cb = conv_w, row(conv_b)
    wq, wk = w_q.astype(BF16), w_k.astype(BF16)
    xc, q, k = _conv_qk(z, 2, cw, cb, wq, wk, tm=min(n, 1024))
    _, qc, kc = _conv_qk(zc, 2, cw, cb, wq, wk, tm=ctx.shape[0])

    zero = (jnp.zeros((H, HEAD_DIM, HEAD_DIM), F32), jnp.zeros((H, HEAD_DIM), F32),
            jnp.full((H, LANES), NEG, F32))
    rows_c, amat_c = _gate_prep(gct, tg=ctx.shape[0])
    rows_x, amat_x = _gate_prep(gt, tg=min(n, 1024))
    st_f = _mlstm("state", False, qc, kc, zc, 3, rows_c, amat_c, zero)
    st_b = _mlstm("state", True, qc, kc, zc, 3, rows_c, amat_c, zero)
    hb = _mlstm("h", True, q, k, z, 3, rows_x, amat_x, st_b)
    yb = _mlstm("out", False, q, k, z, 3, rows_x, amat_x, st_f, extra=(hb, xc, 4, row(mh_g), row(skip)))

    bs_full = jnp.repeat(b_s.T, HEAD_DIM, axis=1)
    wr = jnp.zeros((d, LANES), F32).at[:, :N_GROUPS].set(w_rg).at[:, N_GROUPS:N_GROUPS + N_EXPERTS].set(w_re)
    br = jnp.zeros((1, LANES), F32).at[0, :N_GROUPS].set(b_rg).at[0, N_GROUPS:N_GROUPS + N_EXPERTS].set(b_re)
    x1, u2, route, route_t, cnt = _merge(z, yb, x, w_s.astype(BF16), bs_full, row(sgu_g), row(sgu_b),
                                         p_a.astype(BF16), p_b.astype(BF16), w_o.astype(BF16),
                                         g1, row(ln1_g), row(ln1_b), sc2, sh2, wr.astype(BF16), br, tm=min(n, 256))

    bm = MOE_BLOCK
    counts = cnt[0, :N_EXPERTS].astype(jnp.int32)
    nblk_e = (counts + bm - 1) // bm
    blk_end = jnp.cumsum(nblk_e)
    row_start = (blk_end - nblk_e) * bm
    eid = jnp.arange(N_EXPERTS, dtype=F32)[:, None]

    def dest_rows(e, rank):
        start = jnp.sum(jnp.where(e[None, :] == eid, row_start[:, None], 0), axis=0)
        return start + rank.astype(jnp.int32)

    dest1 = dest_rows(route_t[0], route_t[4])
    dest2 = dest_rows(route_t[1], route_t[5])
    max_blocks = (2 * n + N_EXPERTS * (bm - 1)) // bm
    tables = _moe_tables(counts, nblk_e, blk_end, dest1, dest2, n, max_blocks)
    y2 = _moe(tables, u2, w1, w3, w2, max_blocks, 2 * n + 2 * bm)
    return _combine(x1, route, y2, g2, row(ln2_g), row(ln2_b), tm=min(n, 512))


def kernel(x, c, ctx, c_ctx, w_ada, b_ada, w_in, b_in, w_s, b_s, sgu_g, sgu_b, conv_w, conv_b, w_q, w_k, mh_g, skip, p_a, p_b, w_o, ln1_g, ln1_b, w_rg, b_rg, w_re, b_re, w1, w3, w2, ln2_g, ln2_b):
    assert x.shape[0] == 1 and w_ada.shape[0] == DEPTH == 1
    out = _layer(x[0], ctx[0], c, c_ctx, w_ada[0], b_ada[0], w_in[0], b_in[0], w_s[0], b_s[0],
                 sgu_g[0], sgu_b[0], conv_w[0], conv_b[0], w_q[0], w_k[0], mh_g[0], skip[0],
                 p_a[0], p_b[0], w_o[0], ln1_g[0], ln1_b[0], w_rg[0], b_rg[0], w_re[0], b_re[0],
                 w1[0], w3[0], w2[0], ln2_g[0], ln2_b[0])
    return out[None]
```

```python
import functools

import jax
import jax.numpy as jnp
from jax import lax
from jax.experimental import pallas as pl
from jax.experimental.pallas import tpu as pltpu

F32 = jnp.float32
BF16 = jnp.bfloat16

CHUNK = 128
N_HEADS = 8
HEAD_DIM = 128
N_GROUPS = 4
EXP_PER_GROUP = 8
N_EXPERTS = N_GROUPS * EXP_PER_GROUP
LN_EPS = 1e-5
NEG = -1e30
DEPTH = 1
ALPHA = (2 * DEPTH) ** 0.25
LANES = 128
VMEM_LIMIT = 56 * 1024 * 1024

INPROJ_TN = 1536
MERGE_SUB = 128
MOE_BLOCK = 256
MOE_WGROUPS = 4


def _cparams(sem):
    return pltpu.CompilerParams(dimension_semantics=sem, vmem_limit_bytes=VMEM_LIMIT)


def _ln(x):
    mu = jnp.mean(x, axis=-1, keepdims=True)
    xc = x - mu
    var = jnp.mean(xc * xc, axis=-1, keepdims=True)
    return xc * lax.rsqrt(var + LN_EPS)


def _sigmoid(x):
    return 1.0 / (1.0 + jnp.exp(-x))


def _silu(x):
    return x * _sigmoid(x)


def _gelu_tanh(x):
    return 0.5 * x * (1.0 + jnp.tanh(0.7978845608028654 * (x + 0.044715 * (x * x * x))))


def _log_sigmoid(x):
    return jnp.minimum(x, 0.0) - jnp.log(1.0 + jnp.exp(-jnp.abs(x)))


def _dot(a, b):
    return jnp.dot(a, b, preferred_element_type=F32)


def _dot_nt(a, b):
    return lax.dot_general(a, b, (((1,), (1,)), ((), ())), preferred_element_type=F32)


def _split2(a):
    hi = a.astype(BF16)
    lo = (a - hi.astype(F32)).astype(BF16)
    return hi, lo


def _split3(a):
    hi = a.astype(BF16)
    r = a - hi.astype(F32)
    mid = r.astype(BF16)
    lo = (r - mid.astype(F32)).astype(BF16)
    return hi, mid, lo


SLAB = 8
U32 = jnp.uint32


def _pack_bf16(v):
    half = v.shape[1] // 2
    lo = lax.bitcast_convert_type(v[:, :half].astype(BF16).astype(F32), U32)
    hi = lax.bitcast_convert_type(v[:, half:].astype(BF16).astype(F32), U32)
    return (lo >> 16) | (hi & jnp.uint32(0xFFFF0000))


def _unpack_bf16(w):
    lo = lax.bitcast_convert_type(w << 16, F32)
    hi = lax.bitcast_convert_type(w & jnp.uint32(0xFFFF0000), F32)
    return jnp.concatenate([lo, hi], axis=1)


def _slab_shape(rows, d):
    return (rows * SLAB, d // 2 // SLAB)


def _slab_store(ref, lead, x, r0=0):
    rows, d = x.shape
    per = d // SLAB // LANES
    for c in range(d // LANES):
        ref[lead + (pl.ds(r0 * SLAB + c // per, rows, stride=SLAB),
                    slice((c % per) * LANES, (c % per + 1) * LANES))] = x[:, c * LANES:(c + 1) * LANES]


def _interleave(gens):
    live = list(gens)
    while live:
        for g in list(live):
            try:
                next(g)
            except StopIteration:
                live.remove(g)


def _slab_load(ref, lead, rows, d):
    per = d // SLAB // LANES
    return jnp.concatenate(
        [ref[lead + (pl.ds(c // per, rows, stride=SLAB), slice((c % per) * LANES, (c % per + 1) * LANES))]
         for c in range(d // LANES)], axis=1)


def _ada_kernel(c_ref, w_ref, b_ref, o_ref):
    s_hi, s_lo = _split2(_silu(c_ref[...]))
    w_hi, w_lo = _split2(w_ref[...])
    o_ref[...] = _dot(s_hi, w_hi) + _dot(s_lo, w_hi) + _dot(s_hi, w_lo) + b_ref[...]


def _ada(cc, w, b):
    d, n = w.shape
    tn = 1024
    return pl.pallas_call(
        _ada_kernel,
        grid=(n // tn,),
        in_specs=[pl.BlockSpec((8, d), lambda j: (0, 0)),
                  pl.BlockSpec((d, tn), lambda j: (0, j)),
                  pl.BlockSpec((1, tn), lambda j: (0, j))],
        out_specs=pl.BlockSpec((8, tn), lambda j: (0, j)),
        out_shape=jax.ShapeDtypeStruct((8, n), F32),
        compiler_params=_cparams(("arbitrary",)),
        name="ada",
    )(cc, w, b)


def _gate_weights_kernel(wt_hbm, o_ref, buf, sem):
    ng = buf.shape[0]
    cp = pltpu.make_async_copy(wt_hbm.at[pl.ds(wt_hbm.shape[0] - ng, ng), :], buf, sem)
    cp.start()
    cp.wait()
    wg = jnp.concatenate([buf[...], jnp.zeros((LANES - ng, buf.shape[1]), F32)], axis=0)
    hi, lo = _split2(wg)
    o_ref[...] = jnp.concatenate([hi, lo], axis=0)


def _gate_weights(wt, ng):
    d = wt.shape[1]
    return pl.pallas_call(
        _gate_weights_kernel,
        in_specs=[pl.BlockSpec(memory_space=pl.ANY)],
        out_specs=pl.BlockSpec((2 * LANES, d), lambda: (0, 0)),
        out_shape=jax.ShapeDtypeStruct((2 * LANES, d), BF16),
        scratch_shapes=[pltpu.VMEM((ng, d), F32), pltpu.SemaphoreType.DMA(())],
        compiler_params=pltpu.CompilerParams(vmem_limit_bytes=VMEM_LIMIT),
        name="gate_weights",
    )(wt)


def _inproj_kernel(x_ref, sc_ref, sh_ref, w_ref, b_ref, wg_ref, bg_ref, z_ref, g_ref, u_scr):
    @pl.when(pl.program_id(1) == 0)
    def _():
        u = _ln(x_ref[...]) * (1.0 + sc_ref[...]) + sh_ref[...]
        u_hi, u_lo = _split2(u)
        u_scr[...] = u_hi
        wg = wg_ref[...]
        p = _dot_nt(u_hi, wg)
        g = p[:, :LANES] + p[:, LANES:] + _dot_nt(u_lo, wg[:LANES, :]) + bg_ref[...]
        g_ref[...] = g.T[:g_ref.shape[0], :]

    z_ref[...] = (_dot_nt(u_scr[...], w_ref[...]) + b_ref[...]).astype(z_ref.dtype)


def _in_proj(x, sc, sh, wt, b, wg, bg, tm):
    n, d = x.shape
    tn = INPROJ_TN
    nz = wt.shape[0] // tn * tn
    ng = bg.shape[1]
    return pl.pallas_call(
        _inproj_kernel,
        grid=(n // tm, nz // tn),
        in_specs=[pl.BlockSpec((tm, d), lambda i, j: (i, 0)),
                  pl.BlockSpec((1, d), lambda i, j: (0, 0)),
                  pl.BlockSpec((1, d), lambda i, j: (0, 0)),
                  pl.BlockSpec((tn, d), lambda i, j: (j, 0)),
                  pl.BlockSpec((1, tn), lambda i, j: (0, j)),
                  pl.BlockSpec((2 * ng, d), lambda i, j: (0, 0)),
                  pl.BlockSpec((1, ng), lambda i, j: (0, 0))],
        out_specs=[pl.BlockSpec((tm, tn), lambda i, j: (i, j)),
                   pl.BlockSpec((4 * N_HEADS, tm), lambda i, j: (0, i))],
        out_shape=[jax.ShapeDtypeStruct((n, nz), BF16),
                   jax.ShapeDtypeStruct((4 * N_HEADS, n), F32)],
        scratch_shapes=[pltpu.VMEM((tm, d), BF16)],
        compiler_params=_cparams(("arbitrary", "arbitrary")),
        name="in_proj",
    )(x, sc, sh, wt, b, wg, bg)


def _convqk_kernel(xm_ref, prev_ref, next_ref, cw_ref, cb_ref, wq_ref, wk_ref,
                   xc_ref, q_ref, k_ref):
    i = pl.program_id(0)
    last = pl.num_programs(0) - 1
    tm = xm_ref.shape[0]
    xm = xm_ref[...].astype(F32)
    prev_row = jnp.where(i == 0, 0.0, prev_ref[...].astype(F32)[-1:, :])
    next_row = jnp.where(i == last, 0.0, next_ref[...].astype(F32)[:1, :])
    row = lax.broadcasted_iota(jnp.int32, xm.shape, 0)
    x_prev = jnp.where(row == 0, prev_row, pltpu.roll(xm, 1, 0))
    x_next = jnp.where(row == tm - 1, next_row, pltpu.roll(xm, tm - 1, 0))
    cw = cw_ref[...]
    y = cw[0:1, :] * x_prev + cw[1:2, :] * xm + cw[2:3, :] * x_next + cb_ref[...]
    xc = _silu(y).astype(BF16)
    xc_ref[...] = xc
    for h in range(N_HEADS):
        sl = slice(h * HEAD_DIM, (h + 1) * HEAD_DIM)
        q_ref[:, sl] = _dot(xc[:, sl], wq_ref[h]).astype(BF16)
        k_ref[:, sl] = (_dot(xc[:, sl], wk_ref[h]) * (HEAD_DIM ** -0.5)).astype(BF16)


def _conv_qk(z, xm_blk, cw, cb, wq, wk, tm):
    n = z.shape[0]
    w = N_HEADS * HEAD_DIM
    halo = 16
    nb = n // halo
    per = tm // halo
    out = jax.ShapeDtypeStruct((n, w), BF16)
    return pl.pallas_call(
        _convqk_kernel,
        grid=(n // tm,),
        in_specs=[pl.BlockSpec((tm, w), lambda i: (i, xm_blk)),
                  pl.BlockSpec((halo, w), lambda i: (jnp.maximum(i * per - 1, 0), xm_blk)),
                  pl.BlockSpec((halo, w), lambda i: (jnp.minimum((i + 1) * per, nb - 1), xm_blk)),
                  pl.BlockSpec((3, w), lambda i: (0, 0)),
                  pl.BlockSpec((1, w), lambda i: (0, 0)),
                  pl.BlockSpec((N_HEADS, HEAD_DIM, HEAD_DIM), lambda i: (0, 0, 0)),
                  pl.BlockSpec((N_HEADS, HEAD_DIM, HEAD_DIM), lambda i: (0, 0, 0))],
        out_specs=[pl.BlockSpec((tm, w), lambda i: (i, 0))] * 3,
        out_shape=[out, out, out],
        compiler_params=_cparams(("arbitrary",)),
        name="conv_qk",
    )(z, z, z, cw, cb, wq, wk)


def _dot3_right(a, t_bf16):
    hi, mid, lo = _split3(a)
    return _dot(hi, t_bf16) + _dot(mid, t_bf16) + _dot(lo, t_bf16)


def _gate_prep_kernel(gt_ref, rows_ref, amat_ref):
    L, H = CHUNK, N_HEADS
    ri = lax.broadcasted_iota(jnp.int32, (L, L), 0)
    ci = lax.broadcasted_iota(jnp.int32, (L, L), 1)
    eye = (ri == ci).astype(BF16)
    ones8 = jnp.ones((H, L), F32)
    pad = jnp.zeros((LANES - 4 * H, L), F32)
    lane = lax.broadcasted_iota(jnp.int32, (H, L), 1)

    def chunk(d, c):
        reverse = bool(d)
        lanes = slice(c * L, (c + 1) * L)
        tri = ((ri >= ci) if reverse else (ri <= ci)).astype(BF16)
        li = gt_ref[2 * d * H:(2 * d + 1) * H, lanes]
        lf = _log_sigmoid(gt_ref[(2 * d + 1) * H:(2 * d + 2) * H, lanes])
        b = _dot3_right(lf, tri)
        yield
        r = li - b
        terms = jnp.concatenate([t.astype(F32) for t in _split3(r)] + [ones8, pad], axis=0).astype(BF16)
        amat_ref[d, lanes, :] = _dot_nt(eye, terms).astype(BF16)
        big_r = r
        s = 1
        while s < L:
            yield
            if reverse:
                big_r = jnp.maximum(big_r, jnp.where(lane < L - s, pltpu.roll(big_r, L - s, 1), NEG))
            else:
                big_r = jnp.maximum(big_r, jnp.where(lane >= s, pltpu.roll(big_r, s, 1), NEG))
            s *= 2
        rows_ref[d, :, lanes] = jnp.concatenate([b, r, big_r], axis=0)

    _interleave([chunk(d, c) for d in (0, 1) for c in range(gt_ref.shape[1] // L)])


def _gate_prep(gt, tg):
    n = gt.shape[1]
    return pl.pallas_call(
        _gate_prep_kernel,
        grid=(n // tg,),
        in_specs=[pl.BlockSpec((4 * N_HEADS, tg), lambda i: (0, i))],
        out_specs=[pl.BlockSpec((2, 3 * N_HEADS, tg), lambda i: (0, 0, i)),
                   pl.BlockSpec((2, tg, LANES), lambda i: (0, i, 0))],
        out_shape=[jax.ShapeDtypeStruct((2, 3 * N_HEADS, n), F32),
                   jax.ShapeDtypeStruct((2, n, LANES), BF16)],
        compiler_params=_cparams(("arbitrary",)),
        name="gate_prep",
    )(gt)


def _mlstm_kernel(reverse, mode, *refs):
    q_ref, k_ref, v_ref, rows_ref, amat_ref, c0_ref, n0_ref, m0_ref = refs[:8]
    rest = refs[8:]
    if mode == "state":
        c_out, n_out, m_out, c_scr, n_scr, m_scr = rest
    elif mode == "h":
        h_out, c_scr, n_scr, m_scr = rest
    else:
        hb_ref, xc_ref, ob_ref, mhg_ref, skip_ref, y_out, c_scr, n_scr, m_scr = rest

    @pl.when(pl.program_id(0) == 0)
    def _():
        c_scr[...] = c0_ref[...]
        n_scr[...] = n0_ref[...]
        m_scr[...] = m0_ref[...]

    L, H = CHUNK, N_HEADS
    ri = lax.broadcasted_iota(jnp.int32, (L, L), 0)
    ci = lax.broadcasted_iota(jnp.int32, (L, L), 1)
    seen_t = (ri >= ci) if reverse else (ri <= ci)
    last = 0 if reverse else L - 1

    b = rows_ref[0:H, :]
    r = rows_ref[H:2 * H, :]
    big_r = rows_ref[2 * H:3 * H, :]
    m = m_scr[...]
    r_last = big_r[:, last:last + 1]
    big_m = jnp.maximum(big_r, m)
    a = jnp.exp(m - big_m)
    sc = jnp.exp(big_r - big_m)
    floor = jnp.exp(-(b + big_m))
    m_last = jnp.maximum(r_last, m)
    d1 = jnp.exp(m - m_last)
    d2 = jnp.exp(r_last - m_last)
    wk = jnp.exp(r - r_last)
    m_scr[...] = b[:, last:last + 1] + m_last

    a_mat = amat_ref[:, 0:4 * H]
    nr_terms = [t.astype(F32) for t in _split3(-big_r)]
    sub = lax.broadcasted_iota(jnp.int32, (H, L), 0)
    ones16 = jnp.ones((16, L), BF16)

    def head(h):
        sl = slice(h * HEAD_DIM, (h + 1) * HEAD_DIM)
        row = slice(h, h + 1)
        qh = q_ref[:, sl]
        kh = k_ref[:, sl]
        vt = v_ref[:, sl].T
        sel = (sub == h).astype(F32)
        dyn = jnp.where(sub == 0, nr_terms[0][row], jnp.where(sub == 1, nr_terms[1][row],
                        jnp.where(sub == 2, nr_terms[2][row], 0.0)))
        b_mat = jnp.concatenate([sel, sel, sel, dyn], axis=0).astype(BF16)
        arg = _dot(a_mat, b_mat)
        kq = _dot_nt(kh, qh)
        yield
        st = (kq * jnp.exp(jnp.where(seen_t, arg, NEG))).astype(BF16)
        c_prev = c_scr[h]
        n_prev = n_scr[row, :]
        wk16 = jnp.broadcast_to(wk[row], (16, L)).astype(BF16)
        vtw = (vt.astype(F32) * wk[row]).astype(BF16)
        upd = _dot(jnp.concatenate([vtw, wk16], axis=0), kh)
        if mode != "state":
            n16 = jnp.broadcast_to(n_prev, (16, HEAD_DIM)).astype(BF16)
            intra = _dot(jnp.concatenate([vt, ones16], axis=0), st)
            inter = _dot_nt(jnp.concatenate([c_prev.astype(BF16), n16], axis=0), qh)
        yield
        c_scr[h] = d1[row] * c_prev + d2[row] * upd[:L]
        n_scr[row, :] = d1[row] * n_prev + d2[row] * upd[L:L + 1]
        if mode != "state":
            num = a[row] * inter[:L] + sc[row] * intra[:L]
            den = a[row] * inter[L:L + 1] + sc[row] * intra[L:L + 1]
            ht = num * (1.0 / jnp.maximum(jnp.abs(den), floor[row]))
            if mode == "h":
                h_out[h] = ht.astype(h_out.dtype)
            else:
                hs = ht + hb_ref[h].astype(F32)
                mu = jnp.mean(hs, axis=0, keepdims=True)
                hc = hs - mu
                var = jnp.mean(hc * hc, axis=0, keepdims=True)
                hn = (hc * lax.rsqrt(var + LN_EPS)).T
                y = _sigmoid(ob_ref[:, sl].astype(F32)) * (
                    hn * mhg_ref[:, sl] + skip_ref[:, sl] * xc_ref[:, sl].astype(F32))
                y_out[:, sl] = y.astype(y_out.dtype)
        yield

    _interleave([head(h) for h in range(H)])

    if mode == "state":
        c_out[...] = c_scr[...]
        n_out[...] = n_scr[...]
        m_out[...] = m_scr[...]


def _mlstm(mode, reverse, q, k, z, v_blk, rows, amat, state, extra=()):
    n = q.shape[0]
    nc = n // CHUNK
    w = N_HEADS * HEAD_DIM
    c0, n0, m0 = state
    d = int(reverse)
    pos = (lambda c: nc - 1 - c) if reverse else (lambda c: c)
    row = lambda blk: pl.BlockSpec((CHUNK, w), lambda c: (pos(c), blk))
    full = lambda a: pl.BlockSpec(a.shape, lambda c: (0,) * a.ndim)
    ht_spec = pl.BlockSpec((None, N_HEADS, HEAD_DIM, CHUNK), lambda c: (pos(c), 0, 0, 0))
    in_specs = [row(0), row(0), row(v_blk),
                pl.BlockSpec((None, 3 * N_HEADS, CHUNK), lambda c: (d, 0, pos(c))),
                pl.BlockSpec((None, CHUNK, LANES), lambda c: (d, pos(c), 0)),
                full(c0), full(n0), full(m0)]
    args = [q, k, z, rows, amat, c0, n0, m0]
    scratch = [pltpu.VMEM(c0.shape, F32), pltpu.VMEM(n0.shape, F32), pltpu.VMEM(m0.shape, F32)]
    if mode == "state":
        out_specs = [full(c0), full(n0), full(m0)]
        out_shape = [jax.ShapeDtypeStruct(a.shape, F32) for a in state]
    elif mode == "h":
        out_specs = ht_spec
        out_shape = jax.ShapeDtypeStruct((nc, N_HEADS, HEAD_DIM, CHUNK), BF16)
    else:
        out_specs = row(0)
        out_shape = jax.ShapeDtypeStruct((n, w), BF16)
        hb, xc, ob_blk, mhg, skip = extra
        in_specs += [ht_spec, row(0), row(ob_blk), full(mhg), full(skip)]
        args += [hb, xc, z, mhg, skip]
    return pl.pallas_call(
        functools.partial(_mlstm_kernel, reverse, mode),
        grid=(nc,),
        in_specs=in_specs, out_specs=out_specs, out_shape=out_shape,
        scratch_shapes=scratch,
        compiler_params=_cparams(("arbitrary",)),
        name="mlstm_%s_%s" % (mode, "bwd" if reverse else "fwd"),
    )(*args)


def _merge_kernel(ua_ref, va_ref, yb_ref, ga0_ref, ga1_ref, gb0_ref, gb1_ref, x_ref,
                  ws_ref, bs_ref, sg_ref, sb_ref, pa_ref, pb_ref, wo_ref,
                  g1_ref, l1g_ref, l1b_ref, sc2_ref, sh2_ref, wr_ref, br_ref,
                  x1_ref, u2_ref, route_ref, route_t_ref, cnt_ref, a_scr, run_scr):
    tm = x_ref.shape[0]
    sub = MERGE_SUB

    @pl.when(pl.program_id(0) == 0)
    def _():
        run_scr[...] = jnp.zeros_like(run_scr)

    run = [run_scr[0:1, :]]

    def rows_of(r0):
        rs = slice(r0, r0 + sub)
        vn = (_ln(_gelu_tanh(va_ref[rs, :].astype(F32))) * sg_ref[...] + sb_ref[...]).astype(BF16)
        for c in range(sub // CHUNK):
            rows = slice(c * CHUNK, (c + 1) * CHUNK)
            dst = slice(r0 + c * CHUNK, r0 + (c + 1) * CHUNK)
            for g in range(N_HEADS):
                cols = slice(g * HEAD_DIM, (g + 1) * HEAD_DIM)
                mixed = _dot(ws_ref[g], vn[rows, cols]) + bs_ref[:, cols]
                a_scr[dst, cols] = (_gelu_tanh(ua_ref[dst, cols].astype(F32)) * mixed).astype(BF16)
        yield
        pa = _dot(a_scr[rs, :], pa_ref[...])
        pb = _dot(yb_ref[rs, :], pb_ref[...])
        yield
        ga = jnp.concatenate([ga0_ref[rs, :], ga1_ref[rs, :]], axis=1).astype(F32)
        gb = jnp.concatenate([gb0_ref[rs, :], gb1_ref[rs, :]], axis=1).astype(F32)
        mrg = (_sigmoid(ga) * pa + _sigmoid(gb) * pb).astype(BF16)
        yield
        y = _dot(mrg, wo_ref[...])
        yield
        x1 = _ln(ALPHA * x_ref[rs, :] + g1_ref[...] * y) * l1g_ref[...] + l1b_ref[...]
        x1_ref[rs, :] = x1
        u2 = _ln(x1) * (1.0 + sc2_ref[...]) + sh2_ref[...]
        _slab_store(u2_ref, (), _pack_bf16(u2), r0)
        yield
        logit = _dot(u2.astype(BF16), wr_ref[...]) + br_ref[...]
        lane = lax.broadcasted_iota(jnp.int32, logit.shape, 1)
        lane_f = lane.astype(F32)
        is_g = lane < N_GROUPS
        gmax = jnp.max(jnp.where(is_g, logit, NEG), axis=-1, keepdims=True)
        g_sel = jnp.min(jnp.where(is_g & (logit == gmax), lane_f, 1e9), axis=-1, keepdims=True)
        p_g = 1.0 / jnp.sum(jnp.where(is_g, jnp.exp(logit - gmax), 0.0), axis=-1, keepdims=True)
        lo = N_GROUPS + EXP_PER_GROUP * g_sel
        in_grp = (lane_f >= lo) & (lane_f < lo + EXP_PER_GROUP)
        el = jnp.where(in_grp, logit, NEG)
        e1max = jnp.max(el, axis=-1, keepdims=True)
        l1 = jnp.min(jnp.where(in_grp & (el == e1max), lane_f, 1e9), axis=-1, keepdims=True)
        el2 = jnp.where(lane_f == l1, NEG, el)
        e2max = jnp.max(el2, axis=-1, keepdims=True)
        l2 = jnp.min(jnp.where(in_grp & (el2 == e2max), lane_f, 1e9), axis=-1, keepdims=True)
        zsum = jnp.sum(jnp.where(in_grp, jnp.exp(el - e1max), 0.0), axis=-1, keepdims=True)
        p1 = 1.0 / zsum
        p2 = jnp.exp(e2max - e1max) / zsum
        w1 = p_g * p1 / (p1 + p2)
        w2 = p_g * p2 / (p1 + p2)
        e1 = l1 - N_GROUPS
        e2 = l2 - N_GROUPS
        oh1 = (lane_f == e1).astype(BF16)
        oh2 = (lane_f == e2).astype(BF16)
        ri = lax.broadcasted_iota(jnp.int32, (sub, sub), 0)
        ci = lax.broadcasted_iota(jnp.int32, (sub, sub), 1)
        strict = (ci < ri).astype(BF16)
        cnt1 = jnp.sum(oh1.astype(F32), axis=0, keepdims=True)
        cnt2 = jnp.sum(oh2.astype(F32), axis=0, keepdims=True)
        pre1 = _dot(strict, oh1) + run[0]
        pre2 = _dot(strict, oh2) + run[0] + cnt1
        rank1 = jnp.sum(oh1.astype(F32) * pre1, axis=-1, keepdims=True)
        rank2 = jnp.sum(oh2.astype(F32) * pre2, axis=-1, keepdims=True)
        run[0] = run[0] + cnt1 + cnt2
        route = jnp.where(lane == 0, e1, 0.0)
        route = jnp.where(lane == 1, e2, route)
        route = jnp.where(lane == 2, w1, route)
        route = jnp.where(lane == 3, w2, route)
        route = jnp.where(lane == 4, rank1, route)
        route = jnp.where(lane == 5, rank2, route)
        route_ref[rs, :] = route
        route_t_ref[:, rs] = route.T[:route_t_ref.shape[0], :]
        yield

    _interleave([rows_of(r0) for r0 in range(0, tm, sub)])
    run_scr[...] = jnp.broadcast_to(run[0], run_scr.shape)
    cnt_ref[...] = jnp.broadcast_to(run[0], cnt_ref.shape)


def _merge(z, yb, x, ws, bs, sg, sb, pa, pb, wo, g1, l1g, l1b, sc2, sh2, wr, br, tm):
    n, d = x.shape
    w = N_HEADS * HEAD_DIM
    zc = lambda blk: pl.BlockSpec((tm, w), lambda i: (i, blk))
    full = lambda a: pl.BlockSpec(a.shape, lambda i: (0,) * a.ndim)
    consts = [ws, bs, sg, sb, pa, pb, wo, g1, l1g, l1b, sc2, sh2, wr, br]
    return pl.pallas_call(
        _merge_kernel,
        grid=(n // tm,),
        in_specs=[zc(0), zc(1), pl.BlockSpec((tm, w), lambda i: (i, 0)),
                  zc(5), zc(6), zc(7), zc(8),
                  pl.BlockSpec((tm, d), lambda i: (i, 0))] + [full(a) for a in consts],
        out_specs=[pl.BlockSpec((tm, d), lambda i: (i, 0)),
                   pl.BlockSpec(_slab_shape(tm, d), lambda i: (i, 0)),
                   pl.BlockSpec((tm, LANES), lambda i: (i, 0)),
                   pl.BlockSpec((8, tm), lambda i: (0, i)),
                   pl.BlockSpec((8, LANES), lambda i: (0, 0))],
        out_shape=[jax.ShapeDtypeStruct((n, d), F32),
                   jax.ShapeDtypeStruct(_slab_shape(n, d), U32),
                   jax.ShapeDtypeStruct((n, LANES), F32),
                   jax.ShapeDtypeStruct((8, n), F32),
                   jax.ShapeDtypeStruct((8, LANES), F32)],
        scratch_shapes=[pltpu.VMEM((tm, w), BF16), pltpu.VMEM((8, LANES), F32)],
        compiler_params=_cparams(("arbitrary",)),
        name="merge",
    )(z, z, yb, z, z, z, z, x, *consts)


def _moe_kernel(be_ref, slot_ref, kidx_ref, nk_ref, nxt_ref, ubase_ref, nvalid_ref, nb_ref, stok_ref, sdst_ref,
                u_hbm, w1_hbm, w3_hbm, w2_hbm, y_hbm,
                xbuf, obuf, w1b, w3b, w2b, st1, st3, st2, gsem, ssem, wsem, zsem):
    b = pl.program_id(0)
    nb = nb_ref[0]
    bm = xbuf.shape[1] // SLAB
    d = xbuf.shape[2] * SLAB * 2
    kch = st1.shape[1]
    fch = st2.shape[1]
    nf = w2_hbm.shape[1] // fch
    pad_rows = y_hbm.shape[0] - 2 * bm * SLAB

    def gather_start(blk, par):
        base = ubase_ref[blk]
        for t in range(bm):
            row = pl.multiple_of(stok_ref[base + t], SLAB)
            pltpu.make_async_copy(u_hbm.at[pl.ds(row, SLAB), :],
                                  xbuf.at[par, pl.ds(t * SLAB, SLAB), :], gsem.at[par]).start()

    def gather_wait(par):
        pltpu.make_async_copy(u_hbm.at[pl.ds(0, bm * SLAB), :], xbuf.at[par], gsem.at[par]).wait()

    def scatter_start(blk, par):
        base = ubase_ref[blk]
        nvalid = nvalid_ref[blk]
        pad = pad_rows + par * (bm * SLAB)

        def issue(full):
            for t in range(bm):
                row = sdst_ref[base + t] if full else jnp.where(t < nvalid, sdst_ref[base + t], pad + t * SLAB)
                pltpu.make_async_copy(obuf.at[par, pl.ds(t * SLAB, SLAB), :],
                                      y_hbm.at[pl.ds(pl.multiple_of(row, SLAB), SLAB), :],
                                      ssem.at[par]).start(priority=t % 2)

        pl.when(nvalid == bm)(functools.partial(issue, True))
        pl.when(nvalid != bm)(functools.partial(issue, False))

    def scatter_wait(par):
        pltpu.make_async_copy(obuf.at[par], y_hbm.at[pl.ds(0, bm * SLAB), :], ssem.at[par]).wait()

    def w_rows(f):
        hint = (lambda v, m: v) if isinstance(f, int) else pl.multiple_of
        return (pl.ds(hint(f * kch, kch), kch), pl.ds(hint(f * fch, fch), fch))

    def w_copies(e, f, i):
        r13, r2 = w_rows(f)
        return (pltpu.make_async_copy(w1_hbm.at[e, r13, :], st1.at[i], wsem.at[i, 0]),
                pltpu.make_async_copy(w3_hbm.at[e, r13, :], st3.at[i], wsem.at[i, 1]),
                pltpu.make_async_copy(w2_hbm.at[e, r2, :], st2.at[i], wsem.at[i, 2]))

    def w_start(e, f, i):
        for cp in w_copies(e, f, i):
            cp.start(priority=1)

    def w_wait(e, f, i):
        for cp in w_copies(e, f, i):
            cp.wait()

    def w_cast(s, f, i):
        r13, r2 = w_rows(f)
        w1b[s, r13, :] = st1[i].astype(BF16)
        w3b[s, r13, :] = st3[i].astype(BF16)
        w2b[s, r2, :] = st2[i].astype(BF16)

    def w_plan(blk):
        k = kidx_ref[blk]
        nk = nk_ref[blk]
        g_lo = (k * nf) // nk
        return nxt_ref[blk], g_lo, jnp.where(nxt_ref[blk] < N_EXPERTS, ((k + 1) * nf) // nk - g_lo, 0)

    @pl.when(b < nb)
    def _():
        par = lax.rem(b, 2)
        e = be_ref[b]
        s = slot_ref[b]
        e_next, g_lo, n_groups = w_plan(b)

        @pl.when(b == 0)
        def _():
            obuf[0] = jnp.zeros(obuf.shape[1:], obuf.dtype)
            zero = [pltpu.make_async_copy(obuf.at[0], y_hbm.at[pl.ds(pad_rows + i * bm * SLAB, bm * SLAB), :], zsem)
                    for i in range(2)]
            for cp in zero:
                cp.start()
            gather_start(0, 0)
            for f in range(nf):
                w_start(e, f, f % 2)
                w_wait(e, f, f % 2)
                w_cast(s, f, f % 2)
            for cp in zero:
                cp.wait()

        gather_wait(par)

        @pl.when(b >= 2)
        def _():
            scatter_wait(par)

        e_prev, g_prev, n_prev = w_plan(jnp.maximum(b - 1, 0))
        pending = (b >= 1) & (n_prev == 1)

        @pl.when(pending)
        def _():
            w_wait(e_prev, g_prev, 1 - par)

        @pl.when(n_groups > 0)
        def _():
            w_start(e_next, g_lo, par)

        gather_start(jnp.minimum(b + 1, nb - 1), 1 - par)
        w_cast(1 - slot_ref[jnp.maximum(b - 1, 0)], jnp.where(pending, g_prev, nf), 1 - par)
        x = _unpack_bf16(_slab_load(xbuf, (par,), bm, d // 2)).astype(BF16)
        de = w2_hbm.shape[1]
        hidden = (_silu(_dot(x, w1b[s, :d, :])) * _dot(x, w3b[s, :d, :])).astype(BF16)
        _slab_store(obuf, (par,), _pack_bf16(_dot(hidden, w2b[s, :de, :])))
        scatter_start(b, par)

        @pl.when(n_groups > 1)
        def _():
            w_wait(e_next, g_lo, par)
            w_cast(1 - s, g_lo, par)

            def more(f, carry):
                w_start(e_next, f, par)
                w_wait(e_next, f, par)
                w_cast(1 - s, f, par)
                return carry

            lax.fori_loop(g_lo + 1, g_lo + n_groups, more, 0)

        @pl.when(b == nb - 1)
        def _():
            gather_wait(1 - par)
            scatter_wait(par)

            @pl.when(nb >= 2)
            def _():
                scatter_wait(1 - par)


def _moe(tables, u2, w1, w3, w2, max_blocks, n_out_rows):
    d = w1.shape[1]
    de = w1.shape[2]
    bm = MOE_BLOCK
    nf = MOE_WGROUPS
    any_spec = pl.BlockSpec(memory_space=pl.ANY)
    return pl.pallas_call(
        _moe_kernel,
        grid_spec=pltpu.PrefetchScalarGridSpec(
            num_scalar_prefetch=len(tables),
            grid=(max_blocks,),
            in_specs=[any_spec] * 4,
            out_specs=any_spec,
            scratch_shapes=[pltpu.VMEM((2,) + _slab_shape(bm, d), U32), pltpu.VMEM((2,) + _slab_shape(bm, d), U32),
                            pltpu.VMEM((2, d + d // nf, de), BF16), pltpu.VMEM((2, d + d // nf, de), BF16),
                            pltpu.VMEM((2, de + de // nf, d), BF16),
                            pltpu.VMEM((2, d // nf, de), F32), pltpu.VMEM((2, d // nf, de), F32),
                            pltpu.VMEM((2, de // nf, d), F32),
                            pltpu.SemaphoreType.DMA((2,)), pltpu.SemaphoreType.DMA((2,)),
                            pltpu.SemaphoreType.DMA((2, 3)), pltpu.SemaphoreType.DMA(())]),
        out_shape=jax.ShapeDtypeStruct(_slab_shape(n_out_rows, d), U32),
        compiler_params=_cparams(("arbitrary",)),
        name="moe",
    )(*tables, u2, w1, w3, w2)


def _combine_kernel(x1_ref, route_ref, y0_ref, y1_ref, g2_ref, lg_ref, lb_ref, o_ref):
    route = route_ref[...]
    tm, d = x1_ref.shape
    y0 = _unpack_bf16(_slab_load(y0_ref, (), tm, d // 2))
    y1 = _unpack_bf16(_slab_load(y1_ref, (), tm, d // 2))
    f = route[:, 2:3] * y0 + route[:, 3:4] * y1
    o_ref[...] = _ln(ALPHA * x1_ref[...] + g2_ref[...] * f) * lg_ref[...] + lb_ref[...]


def _combine(x1, route, y2, g2, lg, lb, tm):
    n, d = x1.shape
    vec = pl.BlockSpec((1, d), lambda i: (0, 0))
    nt = n // tm
    return pl.pallas_call(
        _combine_kernel,
        grid=(nt,),
        in_specs=[pl.BlockSpec((tm, d), lambda i: (i, 0)),
                  pl.BlockSpec((tm, LANES), lambda i: (i, 0)),
                  pl.BlockSpec(_slab_shape(tm, d), lambda i: (i, 0)),
                  pl.BlockSpec(_slab_shape(tm, d), lambda i: (i + nt, 0)),
                  vec, vec, vec],
        out_specs=pl.BlockSpec((tm, d), lambda i: (i, 0)),
        out_shape=jax.ShapeDtypeStruct((n, d), F32),
        compiler_params=_cparams(("arbitrary",)),
        name="combine",
    )(x1, route, y2, y2, g2, lg, lb)


def _moe_tables(counts, nblk_e, blk_end, dest1, dest2, n, max_blocks):
    bm = MOE_BLOCK
    i32 = jnp.int32
    blk = jnp.arange(max_blocks)
    be = jnp.minimum(jnp.sum(blk[:, None] >= blk_end[None, :], axis=1), N_EXPERTS - 1)
    kidx = blk - (blk_end - nblk_e)[be]
    nk = jnp.maximum(nblk_e[be], 1)
    live = nblk_e > 0
    slot = ((jnp.cumsum(live) - 1)[be]) % 2
    first_live_from = jnp.flip(lax.cummin(jnp.flip(jnp.where(live, jnp.arange(N_EXPERTS), N_EXPERTS))))
    nxt = jnp.concatenate([first_live_from[1:], jnp.full((1,), N_EXPERTS)])[be]
    ubase = (jnp.cumsum(counts) - counts)[be] + kidx * bm
    nvalid = jnp.clip(counts[be] - kidx * bm, 0, bm)
    nb = blk_end[-1:]
    j = jnp.arange(2 * n, dtype=i32)
    _, stok, sdst = lax.sort((jnp.concatenate([dest1, dest2]), (j % n) * SLAB, j * SLAB), num_keys=1)
    tail = jnp.zeros((bm,), i32)
    stok = jnp.concatenate([stok, tail])
    sdst = jnp.concatenate([sdst, tail])
    return [t.astype(i32) for t in (be, slot, kidx, nk, nxt, ubase, nvalid, nb, stok, sdst)]


def _layer(x, ctx, c, c_ctx, w_ada, b_ada, w_in, b_in, w_s, b_s, sgu_g, sgu_b, conv_w, conv_b,
           w_q, w_k, mh_g, skip, p_a, p_b, w_o, ln1_g, ln1_b, w_rg, b_rg, w_re, b_re,
           w1, w3, w2, ln2_g, ln2_b):
    n, d = x.shape
    w = N_HEADS * HEAD_DIM
    nz = 9 * w
    H = N_HEADS
    row = lambda a: a.reshape(1, -1)

    cc = jnp.zeros((8, d), F32).at[0].set(c[0]).at[1].set(c_ctx)
    mod = _ada(cc, w_ada, row(b_ada))
    sh1, sc1, g1, sh2, sc2, g2 = [mod[0:1, i * d:(i + 1) * d] for i in range(6)]
    sh1c, sc1c = mod[1:2, 0:d], mod[1:2, d:2 * d]

    w_t = w_in.T
    wg = _gate_weights(w_t, 4 * H)
    bg2 = jnp.pad(row(b_in)[:, nz:], ((0, 0), (0, LANES - 4 * H)))
    w_main = w_t.astype(BF16)
    b_main = row(b_in)

    z, gt = _in_proj(x, sc1, sh1, w_main, b_main, wg, bg2, tm=min(n, 1024))
    zc, gct = _in_proj(ctx, sc1c, sh1c, w_main, b_main, wg, bg2, tm=ctx.shape[0])

    cw, cb = conv_w, row(conv_b)
    wq, wk = w_q.astype(BF16), w_k.astype(BF16)
    xc, q, k = _conv_qk(z, 2, cw, cb, wq, wk, tm=min(n, 1024))
    _, qc, kc = _conv_qk(zc, 2, cw, cb, wq, wk, tm=ctx.shape[0])

    zero = (jnp.zeros((H, HEAD_DIM, HEAD_DIM), F32), jnp.zeros((H, HEAD_DIM), F32),
            jnp.full((H, LANES), NEG, F32))
    rows_c, amat_c = _gate_prep(gct, tg=ctx.shape[0])
    rows_x, amat_x = _gate_prep(gt, tg=min(n, 1024))
    st_f = _mlstm("state", False, qc, kc, zc, 3, rows_c, amat_c, zero)
    st_b = _mlstm("state", True, qc, kc, zc, 3, rows_c, amat_c, zero)
    hb = _mlstm("h", True, q, k, z, 3, rows_x, amat_x, st_b)
    yb = _mlstm("out", False, q, k, z, 3, rows_x, amat_x, st_f, extra=(hb, xc, 4, row(mh_g), row(skip)))

    bs_full = jnp.repeat(b_s.T, HEAD_DIM, axis=1)
    wr = jnp.zeros((d, LANES), F32).at[:, :N_GROUPS].set(w_rg).at[:, N_GROUPS:N_GROUPS + N_EXPERTS].set(w_re)
    br = jnp.zeros((1, LANES), F32).at[0, :N_GROUPS].set(b_rg).at[0, N_GROUPS:N_GROUPS + N_EXPERTS].set(b_re)
    x1, u2, route, route_t, cnt = _merge(z, yb, x, w_s.astype(BF16), bs_full, row(sgu_g), row(sgu_b),
                                         p_a.astype(BF16), p_b.astype(BF16), w_o.astype(BF16),
                                         g1, row(ln1_g), row(ln1_b), sc2, sh2, wr.astype(BF16), br, tm=min(n, 256))

    bm = MOE_BLOCK
    counts = cnt[0, :N_EXPERTS].astype(jnp.int32)
    nblk_e = (counts + bm - 1) // bm
    blk_end = jnp.cumsum(nblk_e)
    row_start = (blk_end - nblk_e) * bm
    eid = jnp.arange(N_EXPERTS, dtype=F32)[:, None]

    def dest_rows(e, rank):
        start = jnp.sum(jnp.where(e[None, :] == eid, row_start[:, None], 0), axis=0)
        return start + rank.astype(jnp.int32)

    dest1 = dest_rows(route_t[0], route_t[4])
    dest2 = dest_rows(route_t[1], route_t[5])
    max_blocks = (2 * n + N_EXPERTS * (bm - 1)) // bm
    tables = _moe_tables(counts, nblk_e, blk_end, dest1, dest2, n, max_blocks)
    y2 = _moe(tables, u2, w1, w3, w2, max_blocks, 2 * n + 2 * bm)
    return _combine(x1, route, y2, g2, row(ln2_g), row(ln2_b), tm=min(n, 512))


def kernel(x, c, ctx, c_ctx, w_ada, b_ada, w_in, b_in, w_s, b_s, sgu_g, sgu_b, conv_w, conv_b, w_q, w_k, mh_g, skip, p_a, p_b, w_o, ln1_g, ln1_b, w_rg, b_rg, w_re, b_re, w1, w3, w2, ln2_g, ln2_b):
    assert x.shape[0] == 1 and w_ada.shape[0] == DEPTH == 1
    out = _layer(x[0], ctx[0], c, c_ctx, w_ada[0], b_ada[0], w_in[0], b_in[0], w_s[0], b_s[0],
                 sgu_g[0], sgu_b[0], conv_w[0], conv_b[0], w_q[0], w_k[0], mh_g[0], skip[0],
                 p_a[0], p_b[0], w_o[0], ln1_g[0], ln1_b[0], w_rg[0], b_rg[0], w_re[0], b_re[0],
                 w1[0], w3[0], w2[0], ln2_g[0], ln2_b[0])
    return out[None]
```

```python
import functools

import jax
import jax.numpy as jnp
from jax import lax
from jax.experimental import pallas as pl
from jax.experimental.pallas import tpu as pltpu

F32 = jnp.float32
BF16 = jnp.bfloat16

CHUNK = 128
N_HEADS = 8
HEAD_DIM = 128
N_GROUPS = 4
EXP_PER_GROUP = 8
N_EXPERTS = N_GROUPS * EXP_PER_GROUP
LN_EPS = 1e-5
NEG = -1e30
DEPTH = 1
ALPHA = (2 * DEPTH) ** 0.25
LANES = 128
VMEM_LIMIT = 56 * 1024 * 1024

INPROJ_TN = 1536
MERGE_SUB = 128
MOE_BLOCK = 256
MOE_WGROUPS = 4


def _cparams(sem):
    return pltpu.CompilerParams(dimension_semantics=sem, vmem_limit_bytes=VMEM_LIMIT)


def _ln(x):
    mu = jnp.mean(x, axis=-1, keepdims=True)
    xc = x - mu
    var = jnp.mean(xc * xc, axis=-1, keepdims=True)
    return xc * lax.rsqrt(var + LN_EPS)


def _sigmoid(x):
    return 1.0 / (1.0 + jnp.exp(-x))


def _silu(x):
    return x * _sigmoid(x)


def _gelu_tanh(x):
    return 0.5 * x * (1.0 + jnp.tanh(0.7978845608028654 * (x + 0.044715 * (x * x * x))))


def _log_sigmoid(x):
    return jnp.minimum(x, 0.0) - jnp.log(1.0 + jnp.exp(-jnp.abs(x)))


def _dot(a, b):
    return jnp.dot(a, b, preferred_element_type=F32)


def _dot_nt(a, b):
    return lax.dot_general(a, b, (((1,), (1,)), ((), ())), preferred_element_type=F32)


def _split2(a):
    hi = a.astype(BF16)
    lo = (a - hi.astype(F32)).astype(BF16)
    return hi, lo


def _split3(a):
    hi = a.astype(BF16)
    r = a - hi.astype(F32)
    mid = r.astype(BF16)
    lo = (r - mid.astype(F32)).astype(BF16)
    return hi, mid, lo


SLAB = 8
U32 = jnp.uint32


def _pack_bf16(v):
    half = v.shape[1] // 2
    lo = lax.bitcast_convert_type(v[:, :half].astype(BF16).astype(F32), U32)
    hi = lax.bitcast_convert_type(v[:, half:].astype(BF16).astype(F32), U32)
    return (lo >> 16) | (hi & jnp.uint32(0xFFFF0000))


def _unpack_bf16(w):
    lo = lax.bitcast_convert_type(w << 16, F32)
    hi = lax.bitcast_convert_type(w & jnp.uint32(0xFFFF0000), F32)
    return jnp.concatenate([lo, hi], axis=1)


def _slab_shape(rows, d):
    return (rows * SLAB, d // 2 // SLAB)


def _slab_store(ref, lead, x, r0=0):
    rows, d = x.shape
    per = d // SLAB // LANES
    for c in range(d // LANES):
        ref[lead + (pl.ds(r0 * SLAB + c // per, rows, stride=SLAB),
                    slice((c % per) * LANES, (c % per + 1) * LANES))] = x[:, c * LANES:(c + 1) * LANES]


def _interleave(gens):
    live = list(gens)
    while live:
        for g in list(live):
            try:
                next(g)
            except StopIteration:
                live.remove(g)


def _slab_load(ref, lead, rows, d):
    per = d // SLAB // LANES
    return jnp.concatenate(
        [ref[lead + (pl.ds(c // per, rows, stride=SLAB), slice((c % per) * LANES, (c % per + 1) * LANES))]
         for c in range(d // LANES)], axis=1)


def _ada_kernel(c_ref, w_ref, b_ref, o_ref):
    s_hi, s_lo = _split2(_silu(c_ref[...]))
    w_hi, w_lo = _split2(w_ref[...])
    o_ref[...] = _dot(s_hi, w_hi) + _dot(s_lo, w_hi) + _dot(s_hi, w_lo) + b_ref[...]


def _ada(cc, w, b):
    d, n = w.shape
    tn = 1024
    return pl.pallas_call(
        _ada_kernel,
        grid=(n // tn,),
        in_specs=[pl.BlockSpec((8, d), lambda j: (0, 0)),
                  pl.BlockSpec((d, tn), lambda j: (0, j)),
                  pl.BlockSpec((1, tn), lambda j: (0, j))],
        out_specs=pl.BlockSpec((8, tn), lambda j: (0, j)),
        out_shape=jax.ShapeDtypeStruct((8, n), F32),
        compiler_params=_cparams(("arbitrary",)),
        name="ada",
    )(cc, w, b)


def _gate_weights_kernel(wt_hbm, o_ref, buf, sem):
    ng = buf.shape[0]
    cp = pltpu.make_async_copy(wt_hbm.at[pl.ds(wt_hbm.shape[0] - ng, ng), :], buf, sem)
    cp.start()
    cp.wait()
    wg = jnp.concatenate([buf[...], jnp.zeros((LANES - ng, buf.shape[1]), F32)], axis=0)
    hi, lo = _split2(wg)
    o_ref[...] = jnp.concatenate([hi, lo], axis=0)


def _gate_weights(wt, ng):
    d = wt.shape[1]
    return pl.pallas_call(
        _gate_weights_kernel,
        in_specs=[pl.BlockSpec(memory_space=pl.ANY)],
        out_specs=pl.BlockSpec((2 * LANES, d), lambda: (0, 0)),
        out_shape=jax.ShapeDtypeStruct((2 * LANES, d), BF16),
        scratch_shapes=[pltpu.VMEM((ng, d), F32), pltpu.SemaphoreType.DMA(())],
        compiler_params=pltpu.CompilerParams(vmem_limit_bytes=VMEM_LIMIT),
        name="gate_weights",
    )(wt)


def _inproj_kernel(x_ref, sc_ref, sh_ref, w_ref, b_ref, wg_ref, bg_ref, z_ref, g_ref, u_scr):
    @pl.when(pl.program_id(1) == 0)
    def _():
        u = _ln(x_ref[...]) * (1.0 + sc_ref[...]) + sh_ref[...]
        u_hi = u.astype(BF16)
        u_scr[...] = u_hi
        g = _dot_nt(u_hi, wg_ref[:LANES, :]) + bg_ref[...]
        g_ref[...] = g.T[:g_ref.shape[0], :]

    z_ref[...] = (_dot_nt(u_scr[...], w_ref[...]) + b_ref[...]).astype(z_ref.dtype)


def _in_proj(x, sc, sh, wt, b, wg, bg, tm):
    n, d = x.shape
    tn = INPROJ_TN
    nz = wt.shape[0] // tn * tn
    ng = bg.shape[1]
    return pl.pallas_call(
        _inproj_kernel,
        grid=(n // tm, nz // tn),
        in_specs=[pl.BlockSpec((tm, d), lambda i, j: (i, 0)),
                  pl.BlockSpec((1, d), lambda i, j: (0, 0)),
                  pl.BlockSpec((1, d), lambda i, j: (0, 0)),
                  pl.BlockSpec((tn, d), lambda i, j: (j, 0)),
                  pl.BlockSpec((1, tn), lambda i, j: (0, j)),
                  pl.BlockSpec((2 * ng, d), lambda i, j: (0, 0)),
                  pl.BlockSpec((1, ng), lambda i, j: (0, 0))],
        out_specs=[pl.BlockSpec((tm, tn), lambda i, j: (i, j)),
                   pl.BlockSpec((4 * N_HEADS, tm), lambda i, j: (0, i))],
        out_shape=[jax.ShapeDtypeStruct((n, nz), BF16),
                   jax.ShapeDtypeStruct((4 * N_HEADS, n), F32)],
        scratch_shapes=[pltpu.VMEM((tm, d), BF16)],
        compiler_params=_cparams(("arbitrary", "arbitrary")),
        name="in_proj",
    )(x, sc, sh, wt, b, wg, bg)


def _convqk_kernel(xm_ref, prev_ref, next_ref, cw_ref, cb_ref, wq_ref, wk_ref,
                   xc_ref, q_ref, k_ref):
    i = pl.program_id(0)
    last = pl.num_programs(0) - 1
    tm = xm_ref.shape[0]
    xm = xm_ref[...].astype(F32)
    prev_row = jnp.where(i == 0, 0.0, prev_ref[...].astype(F32)[-1:, :])
    next_row = jnp.where(i == last, 0.0, next_ref[...].astype(F32)[:1, :])
    row = lax.broadcasted_iota(jnp.int32, xm.shape, 0)
    x_prev = jnp.where(row == 0, prev_row, pltpu.roll(xm, 1, 0))
    x_next = jnp.where(row == tm - 1, next_row, pltpu.roll(xm, tm - 1, 0))
    cw = cw_ref[...]
    y = cw[0:1, :] * x_prev + cw[1:2, :] * xm + cw[2:3, :] * x_next + cb_ref[...]
    xc = _silu(y).astype(BF16)
    xc_ref[...] = xc
    for h in range(N_HEADS):
        sl = slice(h * HEAD_DIM, (h + 1) * HEAD_DIM)
        q_ref[:, sl] = _dot(xc[:, sl], wq_ref[h]).astype(BF16)
        k_ref[:, sl] = (_dot(xc[:, sl], wk_ref[h]) * (HEAD_DIM ** -0.5)).astype(BF16)


def _conv_qk(z, xm_blk, cw, cb, wq, wk, tm):
    n = z.shape[0]
    w = N_HEADS * HEAD_DIM
    halo = 16
    nb = n // halo
    per = tm // halo
    out = jax.ShapeDtypeStruct((n, w), BF16)
    return pl.pallas_call(
        _convqk_kernel,
        grid=(n // tm,),
        in_specs=[pl.BlockSpec((tm, w), lambda i: (i, xm_blk)),
                  pl.BlockSpec((halo, w), lambda i: (jnp.maximum(i * per - 1, 0), xm_blk)),
                  pl.BlockSpec((halo, w), lambda i: (jnp.minimum((i + 1) * per, nb - 1), xm_blk)),
                  pl.BlockSpec((3, w), lambda i: (0, 0)),
                  pl.BlockSpec((1, w), lambda i: (0, 0)),
                  pl.BlockSpec((N_HEADS, HEAD_DIM, HEAD_DIM), lambda i: (0, 0, 0)),
                  pl.BlockSpec((N_HEADS, HEAD_DIM, HEAD_DIM), lambda i: (0, 0, 0))],
        out_specs=[pl.BlockSpec((tm, w), lambda i: (i, 0))] * 3,
        out_shape=[out, out, out],
        compiler_params=_cparams(("arbitrary",)),
        name="conv_qk",
    )(z, z, z, cw, cb, wq, wk)


def _dot3_right(a, t_bf16):
    hi, mid, lo = _split3(a)
    return _dot(hi, t_bf16) + _dot(mid, t_bf16) + _dot(lo, t_bf16)


def _gate_prep_kernel(gt_ref, rows_ref, amat_ref):
    L, H = CHUNK, N_HEADS
    ri = lax.broadcasted_iota(jnp.int32, (L, L), 0)
    ci = lax.broadcasted_iota(jnp.int32, (L, L), 1)
    eye = (ri == ci).astype(BF16)
    ones8 = jnp.ones((H, L), F32)
    pad = jnp.zeros((LANES - 4 * H, L), F32)
    lane = lax.broadcasted_iota(jnp.int32, (H, L), 1)

    def chunk(d, c):
        reverse = bool(d)
        lanes = slice(c * L, (c + 1) * L)
        tri = ((ri >= ci) if reverse else (ri <= ci)).astype(BF16)
        li = gt_ref[2 * d * H:(2 * d + 1) * H, lanes]
        lf = _log_sigmoid(gt_ref[(2 * d + 1) * H:(2 * d + 2) * H, lanes])
        b = _dot3_right(lf, tri)
        yield
        r = li - b
        terms = jnp.concatenate([t.astype(F32) for t in _split3(r)] + [ones8, pad], axis=0).astype(BF16)
        amat_ref[d, lanes, :] = _dot_nt(eye, terms).astype(BF16)
        big_r = r
        s = 1
        while s < L:
            yield
            if reverse:
                big_r = jnp.maximum(big_r, jnp.where(lane < L - s, pltpu.roll(big_r, L - s, 1), NEG))
            else:
                big_r = jnp.maximum(big_r, jnp.where(lane >= s, pltpu.roll(big_r, s, 1), NEG))
            s *= 2
        rows_ref[d, :, lanes] = jnp.concatenate([b, r, big_r], axis=0)

    _interleave([chunk(d, c) for d in (0, 1) for c in range(gt_ref.shape[1] // L)])


def _gate_prep(gt, tg):
    n = gt.shape[1]
    return pl.pallas_call(
        _gate_prep_kernel,
        grid=(n // tg,),
        in_specs=[pl.BlockSpec((4 * N_HEADS, tg), lambda i: (0, i))],
        out_specs=[pl.BlockSpec((2, 3 * N_HEADS, tg), lambda i: (0, 0, i)),
                   pl.BlockSpec((2, tg, LANES), lambda i: (0, i, 0))],
        out_shape=[jax.ShapeDtypeStruct((2, 3 * N_HEADS, n), F32),
                   jax.ShapeDtypeStruct((2, n, LANES), BF16)],
        compiler_params=_cparams(("arbitrary",)),
        name="gate_prep",
    )(gt)


def _mlstm_kernel(reverse, mode, *refs):
    q_ref, k_ref, v_ref, rows_ref, amat_ref, c0_ref, n0_ref, m0_ref = refs[:8]
    rest = refs[8:]
    if mode == "state":
        c_out, n_out, m_out, c_scr, n_scr, m_scr = rest
    elif mode == "h":
        h_out, c_scr, n_scr, m_scr = rest
    else:
        hb_ref, xc_ref, ob_ref, mhg_ref, skip_ref, y_out, c_scr, n_scr, m_scr = rest

    @pl.when(pl.program_id(0) == 0)
    def _():
        c_scr[...] = c0_ref[...]
        n_scr[...] = n0_ref[...]
        m_scr[...] = m0_ref[...]

    L, H = CHUNK, N_HEADS
    ri = lax.broadcasted_iota(jnp.int32, (L, L), 0)
    ci = lax.broadcasted_iota(jnp.int32, (L, L), 1)
    seen_t = (ri >= ci) if reverse else (ri <= ci)
    last = 0 if reverse else L - 1

    b = rows_ref[0:H, :]
    r = rows_ref[H:2 * H, :]
    big_r = rows_ref[2 * H:3 * H, :]
    m = m_scr[...]
    r_last = big_r[:, last:last + 1]
    big_m = jnp.maximum(big_r, m)
    a = jnp.exp(m - big_m)
    sc = jnp.exp(big_r - big_m)
    floor = jnp.exp(-(b + big_m))
    m_last = jnp.maximum(r_last, m)
    d1 = jnp.exp(m - m_last)
    d2 = jnp.exp(r_last - m_last)
    wk = jnp.exp(r - r_last)
    m_scr[...] = b[:, last:last + 1] + m_last

    a_mat = amat_ref[:, 0:4 * H]
    nr_terms = [t.astype(F32) for t in _split3(-big_r)]
    sub = lax.broadcasted_iota(jnp.int32, (H, L), 0)
    ones16 = jnp.ones((16, L), BF16)

    def head(h):
        sl = slice(h * HEAD_DIM, (h + 1) * HEAD_DIM)
        row = slice(h, h + 1)
        qh = q_ref[:, sl]
        kh = k_ref[:, sl]
        vt = v_ref[:, sl].T
        sel = (sub == h).astype(F32)
        dyn = jnp.where(sub == 0, nr_terms[0][row], jnp.where(sub == 1, nr_terms[1][row],
                        jnp.where(sub == 2, nr_terms[2][row], 0.0)))
        b_mat = jnp.concatenate([sel, sel, sel, dyn], axis=0).astype(BF16)
        arg = _dot(a_mat, b_mat)
        kq = _dot_nt(kh, qh)
        yield
        st = (kq * jnp.exp(jnp.where(seen_t, arg, NEG))).astype(BF16)
        c_prev = c_scr[h]
        n_prev = n_scr[row, :]
        wk16 = jnp.broadcast_to(wk[row], (16, L)).astype(BF16)
        vtw = (vt.astype(F32) * wk[row]).astype(BF16)
        upd = _dot(jnp.concatenate([vtw, wk16], axis=0), kh)
        if mode != "state":
            n16 = jnp.broadcast_to(n_prev, (16, HEAD_DIM)).astype(BF16)
            intra = _dot(jnp.concatenate([vt, ones16], axis=0), st)
            inter = _dot_nt(jnp.concatenate([c_prev.astype(BF16), n16], axis=0), qh)
        yield
        c_scr[h] = d1[row] * c_prev + d2[row] * upd[:L]
        n_scr[row, :] = d1[row] * n_prev + d2[row] * upd[L:L + 1]
        if mode != "state":
            num = a[row] * inter[:L] + sc[row] * intra[:L]
            den = a[row] * inter[L:L + 1] + sc[row] * intra[L:L + 1]
            ht = num * (1.0 / jnp.maximum(jnp.abs(den), floor[row]))
            if mode == "h":
                h_out[h] = ht.astype(h_out.dtype)
            else:
                hs = ht + hb_ref[h].astype(F32)
                mu = jnp.mean(hs, axis=0, keepdims=True)
                hc = hs - mu
                var = jnp.mean(hc * hc, axis=0, keepdims=True)
                hn = (hc * lax.rsqrt(var + LN_EPS)).T
                y = _sigmoid(ob_ref[:, sl].astype(F32)) * (
                    hn * mhg_ref[:, sl] + skip_ref[:, sl] * xc_ref[:, sl].astype(F32))
                y_out[:, sl] = y.astype(y_out.dtype)
        yield

    _interleave([head(h) for h in range(H)])

    if mode == "state":
        c_out[...] = c_scr[...]
        n_out[...] = n_scr[...]
        m_out[...] = m_scr[...]


def _mlstm(mode, reverse, q, k, z, v_blk, rows, amat, state, extra=()):
    n = q.shape[0]
    nc = n // CHUNK
    w = N_HEADS * HEAD_DIM
    c0, n0, m0 = state
    d = int(reverse)
    pos = (lambda c: nc - 1 - c) if reverse else (lambda c: c)
    row = lambda blk: pl.BlockSpec((CHUNK, w), lambda c: (pos(c), blk))
    full = lambda a: pl.BlockSpec(a.shape, lambda c: (0,) * a.ndim)
    ht_spec = pl.BlockSpec((None, N_HEADS, HEAD_DIM, CHUNK), lambda c: (pos(c), 0, 0, 0))
    in_specs = [row(0), row(0), row(v_blk),
                pl.BlockSpec((None, 3 * N_HEADS, CHUNK), lambda c: (d, 0, pos(c))),
                pl.BlockSpec((None, CHUNK, LANES), lambda c: (d, pos(c), 0)),
                full(c0), full(n0), full(m0)]
    args = [q, k, z, rows, amat, c0, n0, m0]
    scratch = [pltpu.VMEM(c0.shape, F32), pltpu.VMEM(n0.shape, F32), pltpu.VMEM(m0.shape, F32)]
    if mode == "state":
        out_specs = [full(c0), full(n0), full(m0)]
        out_shape = [jax.ShapeDtypeStruct(a.shape, F32) for a in state]
    elif mode == "h":
        out_specs = ht_spec
        out_shape = jax.ShapeDtypeStruct((nc, N_HEADS, HEAD_DIM, CHUNK), BF16)
    else:
        out_specs = row(0)
        out_shape = jax.ShapeDtypeStruct((n, w), BF16)
        hb, xc, ob_blk, mhg, skip = extra
        in_specs += [ht_spec, row(0), row(ob_blk), full(mhg), full(skip)]
        args += [hb, xc, z, mhg, skip]
    return pl.pallas_call(
        functools.partial(_mlstm_kernel, reverse, mode),
        grid=(nc,),
        in_specs=in_specs, out_specs=out_specs, out_shape=out_shape,
        scratch_shapes=scratch,
        compiler_params=_cparams(("arbitrary",)),
        name="mlstm_%s_%s" % (mode, "bwd" if reverse else "fwd"),
    )(*args)


def _merge_kernel(ua_ref, va_ref, yb_ref, ga0_ref, ga1_ref, gb0_ref, gb1_ref, x_ref,
                  ws_ref, bs_ref, sg_ref, sb_ref, pa_ref, pb_ref, wo_ref,
                  g1_ref, l1g_ref, l1b_ref, sc2_ref, sh2_ref, wr_ref, br_ref,
                  x1_ref, u2_ref, route_ref, route_t_ref, cnt_ref, a_scr, run_scr):
    tm = x_ref.shape[0]
    sub = MERGE_SUB

    @pl.when(pl.program_id(0) == 0)
    def _():
        run_scr[...] = jnp.zeros_like(run_scr)

    run = [run_scr[0:1, :]]

    def rows_of(r0):
        rs = slice(r0, r0 + sub)
        vn = (_ln(_gelu_tanh(va_ref[rs, :].astype(F32))) * sg_ref[...] + sb_ref[...]).astype(BF16)
        for c in range(sub // CHUNK):
            rows = slice(c * CHUNK, (c + 1) * CHUNK)
            dst = slice(r0 + c * CHUNK, r0 + (c + 1) * CHUNK)
            for g in range(N_HEADS):
                cols = slice(g * HEAD_DIM, (g + 1) * HEAD_DIM)
                mixed = _dot(ws_ref[g], vn[rows, cols]) + bs_ref[:, cols]
                a_scr[dst, cols] = (_gelu_tanh(ua_ref[dst, cols].astype(F32)) * mixed).astype(BF16)
        yield
        pa = _dot(a_scr[rs, :], pa_ref[...])
        pb = _dot(yb_ref[rs, :], pb_ref[...])
        yield
        ga = jnp.concatenate([ga0_ref[rs, :], ga1_ref[rs, :]], axis=1).astype(F32)
        gb = jnp.concatenate([gb0_ref[rs, :], gb1_ref[rs, :]], axis=1).astype(F32)
        mrg = (_sigmoid(ga) * pa + _sigmoid(gb) * pb).astype(BF16)
        yield
        y = _dot(mrg, wo_ref[...])
        yield
        x1 = _ln(ALPHA * x_ref[rs, :] + g1_ref[...] * y) * l1g_ref[...] + l1b_ref[...]
        x1_ref[rs, :] = x1
        u2 = _ln(x1) * (1.0 + sc2_ref[...]) + sh2_ref[...]
        _slab_store(u2_ref, (), _pack_bf16(u2), r0)
        yield
        logit = _dot(u2.astype(BF16), wr_ref[...]) + br_ref[...]
        lane = lax.broadcasted_iota(jnp.int32, logit.shape, 1)
        lane_f = lane.astype(F32)
        is_g = lane < N_GROUPS
        gmax = jnp.max(jnp.where(is_g, logit, NEG), axis=-1, keepdims=True)
        g_sel = jnp.min(jnp.where(is_g & (logit == gmax), lane_f, 1e9), axis=-1, keepdims=True)
        p_g = 1.0 / jnp.sum(jnp.where(is_g, jnp.exp(logit - gmax), 0.0), axis=-1, keepdims=True)
        lo = N_GROUPS + EXP_PER_GROUP * g_sel
        in_grp = (lane_f >= lo) & (lane_f < lo + EXP_PER_GROUP)
        el = jnp.where(in_grp, logit, NEG)
        e1max = jnp.max(el, axis=-1, keepdims=True)
        l1 = jnp.min(jnp.where(in_grp & (el == e1max), lane_f, 1e9), axis=-1, keepdims=True)
        el2 = jnp.where(lane_f == l1, NEG, el)
        e2max = jnp.max(el2, axis=-1, keepdims=True)
        l2 = jnp.min(jnp.where(in_grp & (el2 == e2max), lane_f, 1e9), axis=-1, keepdims=True)
        zsum = jnp.sum(jnp.where(in_grp, jnp.exp(el - e1max), 0.0), axis=-1, keepdims=True)
        p1 = 1.0 / zsum
        p2 = jnp.exp(e2max - e1max) / zsum
        w1 = p_g * p1 / (p1 + p2)
        w2 = p_g * p2 / (p1 + p2)
        e1 = l1 - N_GROUPS
        e2 = l2 - N_GROUPS
        oh1 = (lane_f == e1).astype(BF16)
        oh2 = (lane_f == e2).astype(BF16)
        ri = lax.broadcasted_iota(jnp.int32, (sub, sub), 0)
        ci = lax.broadcasted_iota(jnp.int32, (sub, sub), 1)
        strict = (ci < ri).astype(BF16)
        cnt1 = jnp.sum(oh1.astype(F32), axis=0, keepdims=True)
        cnt2 = jnp.sum(oh2.astype(F32), axis=0, keepdims=True)
        pre1 = _dot(strict, oh1) + run[0]
        pre2 = _dot(strict, oh2) + run[0] + cnt1
        rank1 = jnp.sum(oh1.astype(F32) * pre1, axis=-1, keepdims=True)
        rank2 = jnp.sum(oh2.astype(F32) * pre2, axis=-1, keepdims=True)
        run[0] = run[0] + cnt1 + cnt2
        route = jnp.where(lane == 0, e1, 0.0)
        route = jnp.where(lane == 1, e2, route)
        route = jnp.where(lane == 2, w1, route)
        route = jnp.where(lane == 3, w2, route)
        route = jnp.where(lane == 4, rank1, route)
        route = jnp.where(lane == 5, rank2, route)
        route_ref[rs, :] = route
        route_t_ref[:, rs] = route.T[:route_t_ref.shape[0], :]
        yield

    _interleave([rows_of(r0) for r0 in range(0, tm, sub)])
    run_scr[...] = jnp.broadcast_to(run[0], run_scr.shape)
    cnt_ref[...] = jnp.broadcast_to(run[0], cnt_ref.shape)


def _merge(z, yb, x, ws, bs, sg, sb, pa, pb, wo, g1, l1g, l1b, sc2, sh2, wr, br, tm):
    n, d = x.shape
    w = N_HEADS * HEAD_DIM
    zc = lambda blk: pl.BlockSpec((tm, w), lambda i: (i, blk))
    full = lambda a: pl.BlockSpec(a.shape, lambda i: (0,) * a.ndim)
    consts = [ws, bs, sg, sb, pa, pb, wo, g1, l1g, l1b, sc2, sh2, wr, br]
    return pl.pallas_call(
        _merge_kernel,
        grid=(n // tm,),
        in_specs=[zc(0), zc(1), pl.BlockSpec((tm, w), lambda i: (i, 0)),
                  zc(5), zc(6), zc(7), zc(8),
                  pl.BlockSpec((tm, d), lambda i: (i, 0))] + [full(a) for a in consts],
        out_specs=[pl.BlockSpec((tm, d), lambda i: (i, 0)),
                   pl.BlockSpec(_slab_shape(tm, d), lambda i: (i, 0)),
                   pl.BlockSpec((tm, LANES), lambda i: (i, 0)),
                   pl.BlockSpec((8, tm), lambda i: (0, i)),
                   pl.BlockSpec((8, LANES), lambda i: (0, 0))],
        out_shape=[jax.ShapeDtypeStruct((n, d), F32),
                   jax.ShapeDtypeStruct(_slab_shape(n, d), U32),
                   jax.ShapeDtypeStruct((n, LANES), F32),
                   jax.ShapeDtypeStruct((8, n), F32),
                   jax.ShapeDtypeStruct((8, LANES), F32)],
        scratch_shapes=[pltpu.VMEM((tm, w), BF16), pltpu.VMEM((8, LANES), F32)],
        compiler_params=_cparams(("arbitrary",)),
        name="merge",
    )(z, z, yb, z, z, z, z, x, *consts)


def _moe_kernel(be_ref, slot_ref, kidx_ref, nk_ref, nxt_ref, ubase_ref, nvalid_ref, nb_ref, stok_ref, sdst_ref,
                u_hbm, w1_hbm, w3_hbm, w2_hbm, y_hbm,
                xbuf, obuf, w1b, w3b, w2b, st1, st3, st2, gsem, ssem, wsem, zsem):
    b = pl.program_id(0)
    nb = nb_ref[0]
    bm = xbuf.shape[1] // SLAB
    d = xbuf.shape[2] * SLAB * 2
    kch = st1.shape[1]
    fch = st2.shape[1]
    nf = w2_hbm.shape[1] // fch
    pad_rows = y_hbm.shape[0] - 2 * bm * SLAB

    def gather_start(blk, par):
        base = ubase_ref[blk]
        for t in range(bm):
            row = pl.multiple_of(stok_ref[base + t], SLAB)
            pltpu.make_async_copy(u_hbm.at[pl.ds(row, SLAB), :],
                                  xbuf.at[par, pl.ds(t * SLAB, SLAB), :], gsem.at[par]).start()

    def gather_wait(par):
        pltpu.make_async_copy(u_hbm.at[pl.ds(0, bm * SLAB), :], xbuf.at[par], gsem.at[par]).wait()

    def scatter_start(blk, par):
        base = ubase_ref[blk]
        nvalid = nvalid_ref[blk]
        pad = pad_rows + par * (bm * SLAB)

        def issue(full):
            for t in range(bm):
                row = sdst_ref[base + t] if full else jnp.where(t < nvalid, sdst_ref[base + t], pad + t * SLAB)
                pltpu.make_async_copy(obuf.at[par, pl.ds(t * SLAB, SLAB), :],
                                      y_hbm.at[pl.ds(pl.multiple_of(row, SLAB), SLAB), :],
                                      ssem.at[par]).start(priority=t % 2)

        pl.when(nvalid == bm)(functools.partial(issue, True))
        pl.when(nvalid != bm)(functools.partial(issue, False))

    def scatter_wait(par):
        pltpu.make_async_copy(obuf.at[par], y_hbm.at[pl.ds(0, bm * SLAB), :], ssem.at[par]).wait()

    def w_rows(f):
        hint = (lambda v, m: v) if isinstance(f, int) else pl.multiple_of
        return (pl.ds(hint(f * kch, kch), kch), pl.ds(hint(f * fch, fch), fch))

    def w_copies(e, f, i):
        r13, r2 = w_rows(f)
        return (pltpu.make_async_copy(w1_hbm.at[e, r13, :], st1.at[i], wsem.at[i, 0]),
                pltpu.make_async_copy(w3_hbm.at[e, r13, :], st3.at[i], wsem.at[i, 1]),
                pltpu.make_async_copy(w2_hbm.at[e, r2, :], st2.at[i], wsem.at[i, 2]))

    def w_start(e, f, i):
        for cp in w_copies(e, f, i):
            cp.start(priority=1)

    def w_wait(e, f, i):
        for cp in w_copies(e, f, i):
            cp.wait()

    def w_cast(s, f, i):
        r13, r2 = w_rows(f)
        w1b[s, r13, :] = st1[i].astype(BF16)
        w3b[s, r13, :] = st3[i].astype(BF16)
        w2b[s, r2, :] = st2[i].astype(BF16)

    def w_plan(blk):
        k = kidx_ref[blk]
        nk = nk_ref[blk]
        g_lo = (k * nf) // nk
        return nxt_ref[blk], g_lo, jnp.where(nxt_ref[blk] < N_EXPERTS, ((k + 1) * nf) // nk - g_lo, 0)

    @pl.when(b < nb)
    def _():
        par = lax.rem(b, 2)
        e = be_ref[b]
        s = slot_ref[b]
        e_next, g_lo, n_groups = w_plan(b)

        @pl.when(b == 0)
        def _():
            obuf[0] = jnp.zeros(obuf.shape[1:], obuf.dtype)
            zero = [pltpu.make_async_copy(obuf.at[0], y_hbm.at[pl.ds(pad_rows + i * bm * SLAB, bm * SLAB), :], zsem)
                    for i in range(2)]
            for cp in zero:
                cp.start()
            gather_start(0, 0)
            for f in range(nf):
                w_start(e, f, f % 2)
                w_wait(e, f, f % 2)
                w_cast(s, f, f % 2)
            for cp in zero:
                cp.wait()

        gather_wait(par)

        @pl.when(b >= 2)
        def _():
            scatter_wait(par)

        e_prev, g_prev, n_prev = w_plan(jnp.maximum(b - 1, 0))
        pending = (b >= 1) & (n_prev == 1)

        @pl.when(pending)
        def _():
            w_wait(e_prev, g_prev, 1 - par)

        @pl.when(n_groups > 0)
        def _():
            w_start(e_next, g_lo, par)

        gather_start(jnp.minimum(b + 1, nb - 1), 1 - par)
        w_cast(1 - slot_ref[jnp.maximum(b - 1, 0)], jnp.where(pending, g_prev, nf), 1 - par)
        x = _unpack_bf16(_slab_load(xbuf, (par,), bm, d // 2)).astype(BF16)
        de = w2_hbm.shape[1]
        hidden = (_silu(_dot(x, w1b[s, :d, :])) * _dot(x, w3b[s, :d, :])).astype(BF16)
        _slab_store(obuf, (par,), _pack_bf16(_dot(hidden, w2b[s, :de, :])))
        scatter_start(b, par)

        @pl.when(n_groups > 1)
        def _():
            w_wait(e_next, g_lo, par)
            w_cast(1 - s, g_lo, par)

            def more(f, carry):
                w_start(e_next, f, par)
                w_wait(e_next, f, par)
                w_cast(1 - s, f, par)
                return carry

            lax.fori_loop(g_lo + 1, g_lo + n_groups, more, 0)

        @pl.when(b == nb - 1)
        def _():
            gather_wait(1 - par)
            scatter_wait(par)

            @pl.when(nb >= 2)
            def _():
                scatter_wait(1 - par)


def _moe(tables, u2, w1, w3, w2, max_blocks, n_out_rows):
    d = w1.shape[1]
    de = w1.shape[2]
    bm = MOE_BLOCK
    nf = MOE_WGROUPS
    any_spec = pl.BlockSpec(memory_space=pl.ANY)
    return pl.pallas_call(
        _moe_kernel,
        grid_spec=pltpu.PrefetchScalarGridSpec(
            num_scalar_prefetch=len(tables),
            grid=(max_blocks,),
            in_specs=[any_spec] * 4,
            out_specs=any_spec,
            scratch_shapes=[pltpu.VMEM((2,) + _slab_shape(bm, d), U32), pltpu.VMEM((2,) + _slab_shape(bm, d), U32),
                            pltpu.VMEM((2, d + d // nf, de), BF16), pltpu.VMEM((2, d + d // nf, de), BF16),
                            pltpu.VMEM((2, de + de // nf, d), BF16),
                            pltpu.VMEM((2, d // nf, de), F32), pltpu.VMEM((2, d // nf, de), F32),
                            pltpu.VMEM((2, de // nf, d), F32),
                            pltpu.SemaphoreType.DMA((2,)), pltpu.SemaphoreType.DMA((2,)),
                            pltpu.SemaphoreType.DMA((2, 3)), pltpu.SemaphoreType.DMA(())]),
        out_shape=jax.ShapeDtypeStruct(_slab_shape(n_out_rows, d), U32),
        compiler_params=_cparams(("arbitrary",)),
        name="moe",
    )(*tables, u2, w1, w3, w2)


def _combine_kernel(x1_ref, route_ref, y0_ref, y1_ref, g2_ref, lg_ref, lb_ref, o_ref):
    route = route_ref[...]
    tm, d = x1_ref.shape
    y0 = _unpack_bf16(_slab_load(y0_ref, (), tm, d // 2))
    y1 = _unpack_bf16(_slab_load(y1_ref, (), tm, d // 2))
    f = route[:, 2:3] * y0 + route[:, 3:4] * y1
    o_ref[...] = _ln(ALPHA * x1_ref[...] + g2_ref[...] * f) * lg_ref[...] + lb_ref[...]


def _combine(x1, route, y2, g2, lg, lb, tm):
    n, d = x1.shape
    vec = pl.BlockSpec((1, d), lambda i: (0, 0))
    nt = n // tm
    return pl.pallas_call(
        _combine_kernel,
        grid=(nt,),
        in_specs=[pl.BlockSpec((tm, d), lambda i: (i, 0)),
                  pl.BlockSpec((tm, LANES), lambda i: (i, 0)),
                  pl.BlockSpec(_slab_shape(tm, d), lambda i: (i, 0)),
                  pl.BlockSpec(_slab_shape(tm, d), lambda i: (i + nt, 0)),
                  vec, vec, vec],
        out_specs=pl.BlockSpec((tm, d), lambda i: (i, 0)),
        out_shape=jax.ShapeDtypeStruct((n, d), F32),
        compiler_params=_cparams(("arbitrary",)),
        name="combine",
    )(x1, route, y2, y2, g2, lg, lb)


def _moe_tables(counts, nblk_e, blk_end, dest1, dest2, n, max_blocks):
    bm = MOE_BLOCK
    i32 = jnp.int32
    blk = jnp.arange(max_blocks)
    be = jnp.minimum(jnp.sum(blk[:, None] >= blk_end[None, :], axis=1), N_EXPERTS - 1)
    kidx = blk - (blk_end - nblk_e)[be]
    nk = jnp.maximum(nblk_e[be], 1)
    live = nblk_e > 0
    slot = ((jnp.cumsum(live) - 1)[be]) % 2
    first_live_from = jnp.flip(lax.cummin(jnp.flip(jnp.where(live, jnp.arange(N_EXPERTS), N_EXPERTS))))
    nxt = jnp.concatenate([first_live_from[1:], jnp.full((1,), N_EXPERTS)])[be]
    ubase = (jnp.cumsum(counts) - counts)[be] + kidx * bm
    nvalid = jnp.clip(counts[be] - kidx * bm, 0, bm)
    nb = blk_end[-1:]
    j = jnp.arange(2 * n, dtype=i32)
    _, stok, sdst = lax.sort((jnp.concatenate([dest1, dest2]), (j % n) * SLAB, j * SLAB), num_keys=1)
    tail = jnp.zeros((bm,), i32)
    stok = jnp.concatenate([stok, tail])
    sdst = jnp.concatenate([sdst, tail])
    return [t.astype(i32) for t in (be, slot, kidx, nk, nxt, ubase, nvalid, nb, stok, sdst)]


def _layer(x, ctx, c, c_ctx, w_ada, b_ada, w_in, b_in, w_s, b_s, sgu_g, sgu_b, conv_w, conv_b,
           w_q, w_k, mh_g, skip, p_a, p_b, w_o, ln1_g, ln1_b, w_rg, b_rg, w_re, b_re,
           w1, w3, w2, ln2_g, ln2_b):
    n, d = x.shape
    w = N_HEADS * HEAD_DIM
    nz = 9 * w
    H = N_HEADS
    row = lambda a: a.reshape(1, -1)

    cc = jnp.zeros((8, d), F32).at[0].set(c[0]).at[1].set(c_ctx)
    mod = _ada(cc, w_ada, row(b_ada))
    sh1, sc1, g1, sh2, sc2, g2 = [mod[0:1, i * d:(i + 1) * d] for i in range(6)]
    sh1c, sc1c = mod[1:2, 0:d], mod[1:2, d:2 * d]

    w_t = w_in.T
    wg = _gate_weights(w_t, 4 * H)
    bg2 = jnp.pad(row(b_in)[:, nz:], ((0, 0), (0, LANES - 4 * H)))
    w_main = w_t.astype(BF16)
    b_main = row(b_in)

    z, gt = _in_proj(x, sc1, sh1, w_main, b_main, wg, bg2, tm=min(n, 1024))
    zc, gct = _in_proj(ctx, sc1c, sh1c, w_main, b_main, wg, bg2, tm=ctx.shape[0])

    cw, cb = conv_w, row(conv_b)
    wq, wk = w_q.astype(BF16), w_k.astype(BF16)
    xc, q, k = _conv_qk(z, 2, cw, cb, wq, wk, tm=min(n, 1024))
    _, qc, kc = _conv_qk(zc, 2, cw, cb, wq, wk, tm=ctx.shape[0])

    zero = (jnp.zeros((H, HEAD_DIM, HEAD_DIM), F32), jnp.zeros((H, HEAD_DIM), F32),
            jnp.full((H, LANES), NEG, F32))
    rows_c, amat_c = _gate_prep(gct, tg=ctx.shape[0])
    rows_x, amat_x = _gate_prep(gt, tg=min(n, 1024))
    st_f = _mlstm("state", False, qc, kc, zc, 3, rows_c, amat_c, zero)
    st_b = _mlstm("state", True, qc, kc, zc, 3, rows_c, amat_c, zero)
    hb = _mlstm("h", True, q, k, z, 3, rows_x, amat_x, st_b)
    yb = _mlstm("out", False, q, k, z, 3, rows_x, amat_x, st_f, extra=(hb, xc, 4, row(mh_g), row(skip)))

    bs_full = jnp.repeat(b_s.T, HEAD_DIM, axis=1)
    wr = jnp.zeros((d, LANES), F32).at[:, :N_GROUPS].set(w_rg).at[:, N_GROUPS:N_GROUPS + N_EXPERTS].set(w_re)
    br = jnp.zeros((1, LANES), F32).at[0, :N_GROUPS].set(b_rg).at[0, N_GROUPS:N_GROUPS + N_EXPERTS].set(b_re)
    x1, u2, route, route_t, cnt = _merge(z, yb, x, w_s.astype(BF16), bs_full, row(sgu_g), row(sgu_b),
                                         p_a.astype(BF16), p_b.astype(BF16), w_o.astype(BF16),
                                         g1, row(ln1_g), row(ln1_b), sc2, sh2, wr.astype(BF16), br, tm=min(n, 256))

    bm = MOE_BLOCK
    counts = cnt[0, :N_EXPERTS].astype(jnp.int32)
    nblk_e = (counts + bm - 1) // bm
    blk_end = jnp.cumsum(nblk_e)
    row_start = (blk_end - nblk_e) * bm
    eid = jnp.arange(N_EXPERTS, dtype=F32)[:, None]

    def dest_rows(e, rank):
        start = jnp.sum(jnp.where(e[None, :] == eid, row_start[:, None], 0), axis=0)
        return start + rank.astype(jnp.int32)

    dest1 = dest_rows(route_t[0], route_t[4])
    dest2 = dest_rows(route_t[1], route_t[5])
    max_blocks = (2 * n + N_EXPERTS * (bm - 1)) // bm
    tables = _moe_tables(counts, nblk_e, blk_end, dest1, dest2, n, max_blocks)
    y2 = _moe(tables, u2, w1, w3, w2, max_blocks, 2 * n + 2 * bm)
    return _combine(x1, route, y2, g2, row(ln2_g), row(ln2_b), tm=min(n, 512))


def kernel(x, c, ctx, c_ctx, w_ada, b_ada, w_in, b_in, w_s, b_s, sgu_g, sgu_b, conv_w, conv_b, w_q, w_k, mh_g, skip, p_a, p_b, w_o, ln1_g, ln1_b, w_rg, b_rg, w_re, b_re, w1, w3, w2, ln2_g, ln2_b):
    assert x.shape[0] == 1 and w_ada.shape[0] == DEPTH == 1
    out = _layer(x[0], ctx[0], c, c_ctx, w_ada[0], b_ada[0], w_in[0], b_in[0], w_s[0], b_s[0],
                 sgu_g[0], sgu_b[0], conv_w[0], conv_b[0], w_q[0], w_k[0], mh_g[0], skip[0],
                 p_a[0], p_b[0], w_o[0], ln1_g[0], ln1_b[0], w_rg[0], b_rg[0], w_re[0], b_re[0],
                 w1[0], w3[0], w2[0], ln2_g[0], ln2_b[0])
    return out[None]
```
